```python
import math
import numpy as np
import jax
import jax.numpy as jnp
from jax import lax

D_MODEL = 2048
BATCH = 1
SEQ = 8192
DEPTH = 4

EPS = 1e-6
NEG = -1e30
MIX_W = D_MODEL

SSD_HEAD_DIM = 64
SSD_HEADS = MIX_W // SSD_HEAD_DIM
SSD_STATE = 128
SSD_GROUPS = 8
SSD_CONV = 4
SSD_CHUNK = 128
SSD_GN = SSD_GROUPS * SSD_STATE
SSD_XBC = MIX_W + 2 * SSD_GN
SSD_IN = MIX_W + SSD_XBC + SSD_HEADS

GLA_HEADS = 4
GLA_DKT = MIX_W // 2
GLA_DK = GLA_DKT // GLA_HEADS
GLA_DV = MIX_W // GLA_HEADS
GLA_RANK = 16
GLA_TAU = 16.0
GLA_CHUNK = 128
GLA_IN = 2 * GLA_DKT + 2 * MIX_W + GLA_RANK

NSA_HEADS = 16
NSA_HEAD_DIM = MIX_W // NSA_HEADS
NSA_KV_GROUPS = 4
NSA_GH = NSA_HEADS // NSA_KV_GROUPS
NSA_KVW = NSA_KV_GROUPS * NSA_HEAD_DIM
NSA_CMP_BLOCK = 32
NSA_CMP_STRIDE = 16
NSA_SLC_BLOCK = 64
NSA_TOPK = 16
NSA_WINDOW = 512
NSA_Q_BLOCK = 64
NSA_FORCE_BONUS = 1e3
NSA_IN = 2 * MIX_W + 6 * NSA_KVW + 3 * NSA_HEADS

GDN_QK_HEADS = 8
GDN_V_HEADS = 16
GDN_DK = 128
GDN_DV = MIX_W // GDN_V_HEADS
GDN_CONV = 4
GDN_CHUNK = 64
GDN_QKW = GDN_QK_HEADS * GDN_DK
GDN_CONV_CH = 2 * GDN_QKW + MIX_W
GDN_IN = GDN_CONV_CH + 2 * GDN_V_HEADS + MIX_W

EVEN_IN = SSD_IN + GLA_IN
ODD_IN = NSA_IN + GDN_IN
N_EVEN = (DEPTH + 1) // 2
N_ODD = DEPTH // 2

kernel_name = 'hybrid_ssd_gla_nsa_gdn'


def rmsnorm(x, w):
    xf = x.astype(jnp.float32)
    y = xf * lax.rsqrt(jnp.mean(xf * xf, axis=-1, keepdims=True) + EPS)
    return (y * w.astype(jnp.float32)).astype(x.dtype)


def l2norm(x):
    xf = x.astype(jnp.float32)
    return xf * lax.rsqrt(jnp.sum(xf * xf, axis=-1, keepdims=True) + EPS)


def causal_conv(x, w, b):
    k_w, ch = w.shape
    y = lax.conv_general_dilated(x, w[:, None, :], window_strides=(1,), padding=[(k_w - 1, 0)],
                                 dimension_numbers=('NWC', 'WIO', 'NWC'), feature_group_count=ch)
    return y + b


def chunk_scan(states, decays):
    def step(h, inp):
        s, d = inp
        return h * d + s, h
    h0 = jnp.zeros_like(states[:, 0])
    _, prev = lax.scan(step, h0, (jnp.moveaxis(states, 1, 0), jnp.moveaxis(decays, 1, 0)))
    return jnp.moveaxis(prev, 0, 1)


def ssd_mixer(u, conv_w, conv_b, dt_bias, a_log, d_skip, norm_w):
    bsz, t_len, _ = u.shape
    G, R, P, N, L = SSD_GROUPS, SSD_HEADS // SSD_GROUPS, SSD_HEAD_DIM, SSD_STATE, SSD_CHUNK
    nc = t_len // L
    z = u[..., :MIX_W]
    xbc = jax.nn.silu(causal_conv(u[..., MIX_W:MIX_W + SSD_XBC], conv_w, conv_b))
    dt = jax.nn.softplus(u[..., MIX_W + SSD_XBC:].astype(jnp.float32) + dt_bias.astype(jnp.float32))
    a = -jnp.exp(a_log.astype(jnp.float32))
    x = xbc[..., :MIX_W].reshape(bsz, nc, L, G, R, P)
    bm = xbc[..., MIX_W:MIX_W + SSD_GN].reshape(bsz, nc, L, G, N)
    cm = xbc[..., MIX_W + SSD_GN:].reshape(bsz, nc, L, G, N)
    dt = dt.reshape(bsz, nc, L, G, R)
    cum = jnp.cumsum(dt * a.reshape(G, R), axis=2)
    xdt = x * dt[..., None]
    causal = np.tril(np.ones((L, L), bool))
    cum_t = jnp.moveaxis(cum, 2, -1)
    decay = jnp.exp(jnp.where(causal, cum_t[..., :, None] - cum_t[..., None, :], -jnp.inf))
    scores = jnp.einsum('bclgn,bcsgn->bcgls', cm, bm)
    y = jnp.einsum('bcgrls,bcsgrp->bclgrp', scores[:, :, :, None] * decay, xdt)
    to_end = jnp.exp(cum[:, :, -1:] - cum)
    states = jnp.einsum('bclgn,bclgrp->bcgrpn', bm, xdt * to_end[..., None])
    prev = chunk_scan(states, jnp.exp(cum[:, :, -1])[..., None, None])
    y = y + jnp.einsum('bclgn,bcgrpn->bclgrp', cm, prev) * jnp.exp(cum)[..., None]
    y = y + x * d_skip.reshape(G, R)[..., None]
    y = y.reshape(bsz, t_len, MIX_W)
    return rmsnorm(y * jax.nn.silu(z), norm_w).astype(u.dtype)


def gla_mixer(u, w_gate, b_gate, norm_w):
    bsz, t_len, _ = u.shape
    H, DK, DV, L = GLA_HEADS, GLA_DK, GLA_DV, GLA_CHUNK
    nc = t_len // L
    q = u[..., :GLA_DKT].reshape(bsz, nc, L, H, DK) * DK ** -0.5
    k = u[..., GLA_DKT:2 * GLA_DKT].reshape(bsz, nc, L, H, DK)
    v = u[..., 2 * GLA_DKT:2 * GLA_DKT + MIX_W].reshape(bsz, nc, L, H, DV)
    off = 2 * GLA_DKT + MIX_W
    glr = u[..., off:off + GLA_RANK]
    r = u[..., off + GLA_RANK:]
    log_a = jax.nn.log_sigmoid((glr @ w_gate + b_gate).astype(jnp.float32)) / GLA_TAU
    cum = jnp.cumsum(log_a.reshape(bsz, nc, L, H, DK), axis=2)
    tot = cum[:, :, -1:]
    half = 0.5 * tot
    att = jnp.einsum('bclhd,bcshd->bchls', q * jnp.exp(cum - half), k * jnp.exp(half - cum))
    att = jnp.where(np.tril(np.ones((L, L), bool)), att, 0.0)
    y = jnp.einsum('bchls,bcshv->bclhv', att, v)
    states = jnp.einsum('bclhd,bclhv->bchdv', k * jnp.exp(tot - cum), v)
    prev = chunk_scan(states, jnp.exp(tot[:, :, 0])[..., None])
    y = y + jnp.einsum('bclhd,bchdv->bclhv', q * jnp.exp(cum), prev)
    y = rmsnorm(y.reshape(bsz, t_len, H, DV), norm_w).reshape(bsz, t_len, MIX_W)
    return (y * jax.nn.silu(r)).astype(u.dtype)


def nsa_mixer(u, cmp_pe, cmp_w1, cmp_w2):
    bsz, t_len, _ = u.shape
    G, HG, dh, Q = NSA_KV_GROUPS, NSA_GH, NSA_HEAD_DIM, NSA_Q_BLOCK
    q = u[..., :MIX_W].reshape(bsz, t_len, G, HG, dh) * dh ** -0.5
    kv = u[..., MIX_W:MIX_W + 6 * NSA_KVW].reshape(bsz, t_len, 6, G, dh)
    goff = MIX_W + 6 * NSA_KVW
    gates = jax.nn.sigmoid(u[..., goff:goff + 3 * NSA_HEADS].astype(jnp.float32)).reshape(bsz, t_len, 3, G, HG)
    gate_path = u[..., goff + 3 * NSA_HEADS:]
    k_cmp, v_cmp, k_slc, v_slc, k_win, v_win = [kv[:, :, i] for i in range(6)]

    n_cmp = (t_len - NSA_CMP_BLOCK) // NSA_CMP_STRIDE + 1
    tok = np.arange(n_cmp)[:, None] * NSA_CMP_STRIDE + np.arange(NSA_CMP_BLOCK)[None]

    def compress(src, pe, w1, w2):
        blocks = src[:, tok] + pe[None, None, :, None, :]
        blocks = blocks.transpose(0, 1, 3, 2, 4).reshape(bsz, n_cmp, G, NSA_CMP_BLOCK * dh)
        return jax.nn.silu(blocks @ w1) @ w2

    kc = compress(k_cmp, cmp_pe[0], cmp_w1[0], cmp_w2[0])
    vc = compress(v_cmp, cmp_pe[1], cmp_w1[1], cmp_w2[1])
    cmp_end = jnp.asarray(np.arange(n_cmp) * NSA_CMP_STRIDE + NSA_CMP_BLOCK - 1)

    n_slc = t_len // NSA_SLC_BLOCK
    cs = np.arange(n_cmp) * NSA_CMP_STRIDE
    ss = np.arange(n_slc) * NSA_SLC_BLOCK
    ov = np.clip(np.minimum(cs[:, None] + NSA_CMP_BLOCK, ss[None] + NSA_SLC_BLOCK) - np.maximum(cs[:, None], ss[None]), 0, None)
    overlap = jnp.asarray(ov / NSA_CMP_BLOCK, dtype=jnp.float32)
    top = min(NSA_TOPK, n_slc)

    ks = k_slc.reshape(bsz, n_slc, NSA_SLC_BLOCK, G, dh).transpose(0, 3, 1, 2, 4)
    vs = v_slc.reshape(bsz, n_slc, NSA_SLC_BLOCK, G, dh).transpose(0, 3, 1, 2, 4)
    kwp = jnp.pad(k_win, ((0, 0), (NSA_WINDOW, 0), (0, 0), (0, 0)))
    vwp = jnp.pad(v_win, ((0, 0), (NSA_WINDOW, 0), (0, 0), (0, 0)))
    bidx = jnp.arange(bsz)[:, None, None, None]
    gidx = jnp.arange(G)[None, None, :, None]
    jsl = jnp.arange(n_slc)

    def block_fn(args):
        qb, blk = args
        t = blk * Q + jnp.arange(Q)
        s = jnp.einsum('bqghd,bngd->bqghn', qb, kc).astype(jnp.float32)
        vmask = (cmp_end[None, :] <= t[:, None])[None, :, None, None, :]
        p_cmp = jnp.where(vmask, jax.nn.softmax(jnp.where(vmask, s, NEG), axis=-1), 0.0)
        o_cmp = jnp.einsum('bqghn,bngd->bqghd', p_cmp, vc)
        imp = jnp.einsum('bqghn,nj->bqgj', p_cmp, overlap)
        cur = t // NSA_SLC_BLOCK
        forced = (jsl[None] == 0) | (jsl[None] == cur[:, None]) | (jsl[None] == cur[:, None] - 1)
        slc_ok = jsl[None] * NSA_SLC_BLOCK <= t[:, None]
        imp = jnp.where(slc_ok[None, :, None, :], imp + jnp.where(forced, NSA_FORCE_BONUS, 0.0)[None, :, None, :], NEG)
        top_val, top_idx = lax.top_k(imp, top)
        chosen = top_val > 0.5 * NEG
        kg = ks[bidx, gidx, top_idx]
        vg = vs[bidx, gidx, top_idx]
        key_pos = top_idx[..., None] * NSA_SLC_BLOCK + jnp.arange(NSA_SLC_BLOCK)
        smask = chosen[..., None] & (key_pos <= t[None, :, None, None, None])
        s = jnp.einsum('bqghd,bqgkld->bqghkl', qb, kg).astype(jnp.float32)
        s = jnp.where(smask[:, :, :, None], s, NEG).reshape(bsz, Q, G, HG, top * NSA_SLC_BLOCK)
        p_slc = jax.nn.softmax(s, axis=-1).reshape(bsz, Q, G, HG, top, NSA_SLC_BLOCK)
        o_slc = jnp.einsum('bqghkl,bqgkld->bqghd', p_slc, vg)
        kw = lax.dynamic_slice_in_dim(kwp, blk * Q, NSA_WINDOW + Q, axis=1)
        vw = lax.dynamic_slice_in_dim(vwp, blk * Q, NSA_WINDOW + Q, axis=1)
        wpos = blk * Q - NSA_WINDOW + jnp.arange(NSA_WINDOW + Q)
        wmask = (wpos[None, :] >= 0) & (wpos[None, :] <= t[:, None]) & (wpos[None, :] > t[:, None] - NSA_WINDOW)
        s = jnp.einsum('bqghd,bkgd->bqghk', qb, kw).astype(jnp.float32)
        p_win = jax.nn.softmax(jnp.where(wmask[None, :, None, None, :], s, NEG), axis=-1)
        o_win = jnp.einsum('bqghk,bkgd->bqghd', p_win, vw)
        return o_cmp, o_slc, o_win

    nqb = t_len // Q
    q_blocks = jnp.moveaxis(q.reshape(bsz, nqb, Q, G, HG, dh), 1, 0)
    o_cmp, o_slc, o_win = lax.map(block_fn, (q_blocks, jnp.arange(nqb)))
    back = lambda o: jnp.moveaxis(o, 0, 1).reshape(bsz, t_len, G, HG, dh)
    o = (gates[:, :, 0, :, :, None] * back(o_cmp) + gates[:, :, 1, :, :, None] * back(o_slc)
         + gates[:, :, 2, :, :, None] * back(o_win))
    return (o.reshape(bsz, t_len, MIX_W) * jax.nn.silu(gate_path)).astype(u.dtype)


def gdn_mixer(u, conv_w, conv_b, a_log, dt_bias, norm_w):
    bsz, t_len, _ = u.shape
    H, DK, DV, L = GDN_V_HEADS, GDN_DK, GDN_DV, GDN_CHUNK
    rep = GDN_V_HEADS // GDN_QK_HEADS
    nc = t_len // L
    qkv = jax.nn.silu(causal_conv(u[..., :GDN_CONV_CH], conv_w, conv_b))
    a = u[..., GDN_CONV_CH:GDN_CONV_CH + H].astype(jnp.float32)
    b = u[..., GDN_CONV_CH + H:GDN_CONV_CH + 2 * H].astype(jnp.float32)
    z = u[..., GDN_CONV_CH + 2 * H:]
    q = l2norm(qkv[..., :GDN_QKW].reshape(bsz, t_len, GDN_QK_HEADS, DK))
    k = l2norm(qkv[..., GDN_QKW:2 * GDN_QKW].reshape(bsz, t_len, GDN_QK_HEADS, DK))
    q = jnp.repeat(q, rep, axis=2).reshape(bsz, nc, L, H, DK) * DK ** -0.5
    k = jnp.repeat(k, rep, axis=2).reshape(bsz, nc, L, H, DK)
    v = qkv[..., 2 * GDN_QKW:].reshape(bsz, nc, L, H, DV)
    beta = jax.nn.sigmoid(b).reshape(bsz, nc, L, H)
    g = (-jnp.exp(a_log.astype(jnp.float32)) * jax.nn.softplus(a + dt_bias.astype(jnp.float32))).reshape(bsz, nc, L, H)
    gc = jnp.cumsum(g, axis=2)
    gct = jnp.moveaxis(gc, 2, -1)
    incl = np.tril(np.ones((L, L), bool))
    strict = np.tril(np.ones((L, L), bool), -1)
    decay = jnp.exp(jnp.where(incl, gct[..., :, None] - gct[..., None, :], -jnp.inf))
    kb = k * beta[..., None]
    vb = v * beta[..., None]
    lower = jnp.where(strict, jnp.einsum('bclhd,bcshd->bchls', kb, k) * decay, 0.0)
    eye = jnp.eye(L, dtype=lower.dtype)
    tmat = lax.linalg.triangular_solve(lower + eye, jnp.broadcast_to(eye, lower.shape),
                                       left_side=True, lower=True, unit_diagonal=True)
    u_val = jnp.einsum('bchls,bcshv->bclhv', tmat, vb)
    w_val = jnp.einsum('bchls,bcshd->bclhd', tmat, kb * jnp.exp(gc)[..., None])
    qk = jnp.einsum('bclhd,bcshd->bchls', q, k) * decay
    q_dec = q * jnp.exp(gc)[..., None]
    k_dec = k * jnp.exp(gc[:, :, -1:] - gc)[..., None]
    c_dec = jnp.exp(gc[:, :, -1])

    def step(state, xs):
        u_c, w_c, qd_c, qk_c, kd_c, cd_c = xs
        v_new = u_c - jnp.einsum('blhd,bhdv->blhv', w_c, state)
        o_c = jnp.einsum('blhd,bhdv->blhv', qd_c, state) + jnp.einsum('bhls,bshv->blhv', qk_c, v_new)
        state = state * cd_c[..., None, None] + jnp.einsum('blhd,blhv->bhdv', kd_c, v_new)
        return state, o_c

    s0 = jnp.zeros((bsz, H, DK, DV), u_val.dtype)
    xs = tuple(jnp.moveaxis(t_, 1, 0) for t_ in (u_val, w_val, q_dec, qk, k_dec, c_dec))
    _, o = lax.scan(step, s0, xs)
    o = jnp.moveaxis(o, 0, 1).reshape(bsz, t_len, H, DV)
    o = rmsnorm(o, norm_w) * jax.nn.silu(z.reshape(bsz, t_len, H, DV))
    return o.reshape(bsz, t_len, MIX_W).astype(u.dtype)


def setup_inputs(seed: int = 0) -> dict:
    key = jax.random.key(seed)
    ks = iter(jax.random.split(key, 32))
    nrm = lambda shape, scale: jax.random.normal(next(ks), shape, jnp.float32) * scale
    gain = lambda shape: 1.0 + nrm(shape, 0.02)

    def dt_bias(shape):
        dt = jnp.exp(jax.random.uniform(next(ks), shape, jnp.float32) * (math.log(0.1) - math.log(1e-3)) + math.log(1e-3))
        return dt + jnp.log(-jnp.expm1(-dt))

    def a_log(shape):
        return jnp.log(jax.random.uniform(next(ks), shape, jnp.float32, 1.0, 16.0))

    return {
        'x': nrm((BATCH, SEQ, D_MODEL), 1.0),
        'ln_w': gain((DEPTH, D_MODEL)),
        'final_ln_w': gain((D_MODEL,)),
        'ev_w_in': nrm((N_EVEN, D_MODEL, EVEN_IN), D_MODEL ** -0.5),
        'ev_w_out': nrm((N_EVEN, 2 * MIX_W, D_MODEL), (2 * MIX_W) ** -0.5),
        'ssd_conv_w': nrm((N_EVEN, SSD_CONV, SSD_XBC), SSD_CONV ** -0.5),
        'ssd_conv_b': nrm((N_EVEN, SSD_XBC), 0.02),
        'ssd_dt_bias': dt_bias((N_EVEN, SSD_HEADS)),
        'ssd_a_log': a_log((N_EVEN, SSD_HEADS)),
        'ssd_d': gain((N_EVEN, SSD_HEADS)),
        'ssd_norm_w': gain((N_EVEN, MIX_W)),
        'gla_w_gate': nrm((N_EVEN, GLA_RANK, GLA_DKT), GLA_RANK ** -0.5),
        'gla_b_gate': nrm((N_EVEN, GLA_DKT), 0.02),
        'gla_norm_w': gain((N_EVEN, GLA_DV)),
        'od_w_in': nrm((N_ODD, D_MODEL, ODD_IN), D_MODEL ** -0.5),
        'od_w_out': nrm((N_ODD, 2 * MIX_W, D_MODEL), (2 * MIX_W) ** -0.5),
        'nsa_cmp_pe': nrm((N_ODD, 2, NSA_CMP_BLOCK, NSA_HEAD_DIM), 0.02),
        'nsa_cmp_w1': nrm((N_ODD, 2, NSA_CMP_BLOCK * NSA_HEAD_DIM, NSA_HEAD_DIM), (NSA_CMP_BLOCK * NSA_HEAD_DIM) ** -0.5),
        'nsa_cmp_w2': nrm((N_ODD, 2, NSA_HEAD_DIM, NSA_HEAD_DIM), NSA_HEAD_DIM ** -0.5),
        'gdn_conv_w': nrm((N_ODD, GDN_CONV, GDN_CONV_CH), GDN_CONV ** -0.5),
        'gdn_conv_b': nrm((N_ODD, GDN_CONV_CH), 0.02),
        'gdn_a_log': a_log((N_ODD, GDN_V_HEADS)),
        'gdn_dt_bias': dt_bias((N_ODD, GDN_V_HEADS)),
        'gdn_norm_w': gain((N_ODD, GDN_DV)),
    }


def reference(x, ln_w, final_ln_w, ev_w_in, ev_w_out, ssd_conv_w, ssd_conv_b, ssd_dt_bias, ssd_a_log, ssd_d,
              ssd_norm_w, gla_w_gate, gla_b_gate, gla_norm_w, od_w_in, od_w_out, nsa_cmp_pe, nsa_cmp_w1,
              nsa_cmp_w2, gdn_conv_w, gdn_conv_b, gdn_a_log, gdn_dt_bias, gdn_norm_w):
    for layer in range(DEPTH):
        i = layer // 2
        h = rmsnorm(x, ln_w[layer])
        if layer % 2 == 0:
            u = h @ ev_w_in[i]
            ya = ssd_mixer(u[..., :SSD_IN], ssd_conv_w[i], ssd_conv_b[i], ssd_dt_bias[i], ssd_a_log[i], ssd_d[i], ssd_norm_w[i])
            yb = gla_mixer(u[..., SSD_IN:], gla_w_gate[i], gla_b_gate[i], gla_norm_w[i])
            x = x + jnp.concatenate([ya, yb], axis=-1) @ ev_w_out[i]
        else:
            u = h @ od_w_in[i]
            yc = nsa_mixer(u[..., :NSA_IN], nsa_cmp_pe[i], nsa_cmp_w1[i], nsa_cmp_w2[i])
            yd = gdn_mixer(u[..., NSA_IN:], gdn_conv_w[i], gdn_conv_b[i], gdn_a_log[i], gdn_dt_bias[i], gdn_norm_w[i])
            x = x + jnp.concatenate([yc, yd], axis=-1) @ od_w_out[i]
    return rmsnorm(x, final_ln_w)
```

```python
import functools
import math

import numpy as np
import jax
import jax.numpy as jnp
from jax import lax
from jax.experimental import pallas as pl
from jax.experimental.pallas import tpu as pltpu

D_MODEL = 2048
DEPTH = 4
EPS = 1e-6
NEG = -1e30
MIX_W = D_MODEL

SSD_HEAD_DIM = 64
SSD_HEADS = MIX_W // SSD_HEAD_DIM
SSD_STATE = 128
SSD_GROUPS = 8
SSD_CONV = 4
SSD_CHUNK = 128
SSD_GN = SSD_GROUPS * SSD_STATE
SSD_XBC = MIX_W + 2 * SSD_GN
SSD_IN = MIX_W + SSD_XBC + SSD_HEADS

GLA_HEADS = 4
GLA_DKT = MIX_W // 2
GLA_DK = GLA_DKT // GLA_HEADS
GLA_DV = MIX_W // GLA_HEADS
GLA_RANK = 16
GLA_TAU = 16.0
GLA_CHUNK = 128
GLA_IN = 2 * GLA_DKT + 2 * MIX_W + GLA_RANK

NSA_HEADS = 16
NSA_HEAD_DIM = MIX_W // NSA_HEADS
NSA_KV_GROUPS = 4
NSA_GH = NSA_HEADS // NSA_KV_GROUPS
NSA_KVW = NSA_KV_GROUPS * NSA_HEAD_DIM
NSA_CMP_BLOCK = 32
NSA_CMP_STRIDE = 16
NSA_SLC_BLOCK = 64
NSA_TOPK = 16
NSA_WINDOW = 512
NSA_Q_BLOCK = 64
NSA_FORCE_BONUS = 1e3
NSA_IN = 2 * MIX_W + 6 * NSA_KVW + 3 * NSA_HEADS

GDN_QK_HEADS = 8
GDN_V_HEADS = 16
GDN_DK = 128
GDN_DV = MIX_W // GDN_V_HEADS
GDN_CONV = 4
GDN_CHUNK = 64
GDN_QKW = GDN_QK_HEADS * GDN_DK
GDN_CONV_CH = 2 * GDN_QKW + MIX_W
GDN_IN = GDN_CONV_CH + 2 * GDN_V_HEADS + MIX_W

LANES = 128
VMEM_LIMIT = 56 * 1024 * 1024


def _round_up(n, m):
    return (n + m - 1) // m * m


def _layout(segments, total_multiple):
    out, pos = {}, 0
    for name, _, width in segments:
        out[name] = (pos, width)
        pos += _round_up(width, LANES)
    return out, _round_up(pos, total_multiple)


_EV_SEGS = [
    ("z", 0, MIX_W),
    ("xbc", MIX_W, SSD_XBC),
    ("dt", MIX_W + SSD_XBC, SSD_HEADS),
    ("q", SSD_IN, GLA_DKT),
    ("k", SSD_IN + GLA_DKT, GLA_DKT),
    ("v", SSD_IN + 2 * GLA_DKT, MIX_W),
    ("glr", SSD_IN + 2 * GLA_DKT + MIX_W, GLA_RANK),
    ("r", SSD_IN + 2 * GLA_DKT + MIX_W + GLA_RANK, MIX_W),
]
_OD_SEGS = [
    ("q", 0, MIX_W),
    ("kv", MIX_W, 6 * NSA_KVW),
    ("gates", MIX_W + 6 * NSA_KVW, 3 * NSA_HEADS),
    ("gp", MIX_W + 6 * NSA_KVW + 3 * NSA_HEADS, MIX_W),
    ("conv", NSA_IN, GDN_CONV_CH),
    ("ab", NSA_IN + GDN_CONV_CH, 2 * GDN_V_HEADS),
    ("z", NSA_IN + GDN_CONV_CH + 2 * GDN_V_HEADS, MIX_W),
]
EV_TN = 1792
OD_TN = 1536
_EV_LAYOUT, EV_PAD = _layout(_EV_SEGS, EV_TN)
_OD_LAYOUT, OD_PAD = _layout(_OD_SEGS, OD_TN)


def _pad_weight(w, segs, layout, total):
    cols, pos = [], 0
    for name, src, width in segs:
        dst, _ = layout[name]
        if dst > pos:
            cols.append(jnp.zeros((w.shape[0], dst - pos), w.dtype))
        cols.append(w[:, src:src + width])
        pos = dst + width
    if total > pos:
        cols.append(jnp.zeros((w.shape[0], total - pos), w.dtype))
    return jnp.concatenate(cols, axis=1).astype(jnp.bfloat16)


def _norm_proj_kernel(x_ref, g_ref, w_ref, o_ref, h_ref):
    @pl.when(pl.program_id(1) == 0)
    def _():
        x = x_ref[...]
        ms = jnp.mean(x * x, axis=-1, keepdims=True)
        h_ref[...] = (x * lax.rsqrt(ms + EPS) * g_ref[...]).astype(h_ref.dtype)

    o_ref[...] = jnp.dot(h_ref[...], w_ref[...], preferred_element_type=jnp.float32)


def _norm_proj(x, g, w, tn, tm=512):
    t_len, d = x.shape
    n = w.shape[1]
    return pl.pallas_call(
        _norm_proj_kernel,
        out_shape=jax.ShapeDtypeStruct((t_len, n), jnp.float32),
        grid=(t_len // tm, n // tn),
        in_specs=[
            pl.BlockSpec((tm, d), lambda i, j: (i, 0)),
            pl.BlockSpec((1, d), lambda i, j: (0, 0)),
            pl.BlockSpec((d, tn), lambda i, j: (0, j)),
        ],
        out_specs=pl.BlockSpec((tm, tn), lambda i, j: (i, j)),
        scratch_shapes=[pltpu.VMEM((tm, d), jnp.bfloat16)],
        compiler_params=pltpu.CompilerParams(
            dimension_semantics=("parallel", "arbitrary"), vmem_limit_bytes=VMEM_LIMIT),
        name="norm_proj",
    )(x, g.reshape(1, d), w)


def _out_proj_kernel(x_ref, ya_ref, yb_ref, wa_ref, wb_ref, o_ref):
    acc = jnp.dot(ya_ref[...].astype(jnp.bfloat16), wa_ref[...], preferred_element_type=jnp.float32)
    acc += jnp.dot(yb_ref[...].astype(jnp.bfloat16), wb_ref[...], preferred_element_type=jnp.float32)
    o_ref[...] = x_ref[...] + acc


def _out_proj(x, ya, yb, w, tm=256, tn=1024):
    t_len, d = x.shape
    k = ya.shape[1]
    wa = w[:k].astype(jnp.bfloat16)
    wb = w[k:].astype(jnp.bfloat16)
    return pl.pallas_call(
        _out_proj_kernel,
        out_shape=jax.ShapeDtypeStruct((t_len, d), jnp.float32),
        grid=(d // tn, t_len // tm),
        in_specs=[
            pl.BlockSpec((tm, tn), lambda j, i: (i, j)),
            pl.BlockSpec((tm, k), lambda j, i: (i, 0)),
            pl.BlockSpec((tm, k), lambda j, i: (i, 0)),
            pl.BlockSpec((k, tn), lambda j, i: (0, j)),
            pl.BlockSpec((k, tn), lambda j, i: (0, j)),
        ],
        out_specs=pl.BlockSpec((tm, tn), lambda j, i: (i, j)),
        compiler_params=pltpu.CompilerParams(
            dimension_semantics=("parallel", "parallel"), vmem_limit_bytes=VMEM_LIMIT),
        name="out_proj",
    )(x, ya, yb, wa, wb)


def _rmsnorm_kernel(x_ref, g_ref, o_ref):
    x = x_ref[...]
    ms = jnp.mean(x * x, axis=-1, keepdims=True)
    o_ref[...] = x * lax.rsqrt(ms + EPS) * g_ref[...]


def _final_norm(x, g, tm=512):
    t_len, d = x.shape
    return pl.pallas_call(
        _rmsnorm_kernel,
        out_shape=jax.ShapeDtypeStruct((t_len, d), jnp.float32),
        grid=(t_len // tm,),
        in_specs=[pl.BlockSpec((tm, d), lambda i: (i, 0)), pl.BlockSpec((1, d), lambda i: (0, 0))],
        out_specs=pl.BlockSpec((tm, d), lambda i: (i, 0)),
        compiler_params=pltpu.CompilerParams(dimension_semantics=("parallel",)),
        name="final_norm",
    )(x, g.reshape(1, d))


def _rmsnorm(x, w):
    return x * lax.rsqrt(jnp.mean(x * x, axis=-1, keepdims=True) + EPS) * w


def _l2norm(x):
    return x * lax.rsqrt(jnp.sum(x * x, axis=-1, keepdims=True) + EPS)


def _causal_conv(x, w, b):
    k_w, ch = w.shape
    y = lax.conv_general_dilated(x, w[:, None, :], window_strides=(1,), padding=[(k_w - 1, 0)],
                                 dimension_numbers=('NWC', 'WIO', 'NWC'), feature_group_count=ch)
    return y + b


def _chunk_scan(states, decays):
    def step(h, inp):
        s, d = inp
        return h * d + s, h
    h0 = jnp.zeros_like(states[:, 0])
    _, prev = lax.scan(step, h0, (jnp.moveaxis(states, 1, 0), jnp.moveaxis(decays, 1, 0)))
    return jnp.moveaxis(prev, 0, 1)


def _ssd_mixer(z, xbc_raw, dt_raw, conv_w, conv_b, dt_bias, a_log, d_skip, norm_w):
    bsz, t_len, _ = z.shape
    G, R, P, N, L = SSD_GROUPS, SSD_HEADS // SSD_GROUPS, SSD_HEAD_DIM, SSD_STATE, SSD_CHUNK
    nc = t_len // L
    xbc = jax.nn.silu(_causal_conv(xbc_raw, conv_w, conv_b))
    dt = jax.nn.softplus(dt_raw + dt_bias)
    a = -jnp.exp(a_log)
    x = xbc[..., :MIX_W].reshape(bsz, nc, L, G, R, P)
    bm = xbc[..., MIX_W:MIX_W + SSD_GN].reshape(bsz, nc, L, G, N)
    cm = xbc[..., MIX_W + SSD_GN:].reshape(bsz, nc, L, G, N)
    dt = dt.reshape(bsz, nc, L, G, R)
    cum = jnp.cumsum(dt * a.reshape(G, R), axis=2)
    xdt = x * dt[..., None]
    causal = np.tril(np.ones((L, L), bool))
    cum_t = jnp.moveaxis(cum, 2, -1)
    decay = jnp.exp(jnp.where(causal, cum_t[..., :, None] - cum_t[..., None, :], -jnp.inf))
    scores = jnp.einsum('bclgn,bcsgn->bcgls', cm, bm)
    y = jnp.einsum('bcgrls,bcsgrp->bclgrp', scores[:, :, :, None] * decay, xdt)
    to_end = jnp.exp(cum[:, :, -1:] - cum)
    states = jnp.einsum('bclgn,bclgrp->bcgrpn', bm, xdt * to_end[..., None])
    prev = _chunk_scan(states, jnp.exp(cum[:, :, -1])[..., None, None])
    y = y + jnp.einsum('bclgn,bcgrpn->bclgrp', cm, prev) * jnp.exp(cum)[..., None]
    y = y + x * d_skip.reshape(G, R)[..., None]
    y = y.reshape(bsz, t_len, MIX_W)
    return _rmsnorm(y * jax.nn.silu(z), norm_w)


def _gla_mixer(q, k, v, glr, r, w_gate, b_gate, norm_w):
    bsz, t_len, _ = q.shape
    H, DK, DV, L = GLA_HEADS, GLA_DK, GLA_DV, GLA_CHUNK
    nc = t_len // L
    q = q.reshape(bsz, nc, L, H, DK) * DK ** -0.5
    k = k.reshape(bsz, nc, L, H, DK)
    v = v.reshape(bsz, nc, L, H, DV)
    log_a = jax.nn.log_sigmoid(glr @ w_gate + b_gate) / GLA_TAU
    cum = jnp.cumsum(log_a.reshape(bsz, nc, L, H, DK), axis=2)
    tot = cum[:, :, -1:]
    half = 0.5 * tot
    att = jnp.einsum('bclhd,bcshd->bchls', q * jnp.exp(cum - half), k * jnp.exp(half - cum))
    att = jnp.where(np.tril(np.ones((L, L), bool)), att, 0.0)
    y = jnp.einsum('bchls,bcshv->bclhv', att, v)
    states = jnp.einsum('bclhd,bclhv->bchdv', k * jnp.exp(tot - cum), v)
    prev = _chunk_scan(states, jnp.exp(tot[:, :, 0])[..., None])
    y = y + jnp.einsum('bclhd,bchdv->bclhv', q * jnp.exp(cum), prev)
    y = _rmsnorm(y.reshape(bsz, t_len, H, DV), norm_w).reshape(bsz, t_len, MIX_W)
    return y * jax.nn.silu(r)


def _nsa_mixer(q, kv, gates_raw, gate_path, cmp_pe, cmp_w1, cmp_w2):
    bsz, t_len, _ = q.shape
    G, HG, dh, Q = NSA_KV_GROUPS, NSA_GH, NSA_HEAD_DIM, NSA_Q_BLOCK
    q = q.reshape(bsz, t_len, G, HG, dh) * dh ** -0.5
    kv = kv.reshape(bsz, t_len, 6, G, dh)
    gates = jax.nn.sigmoid(gates_raw).reshape(bsz, t_len, 3, G, HG)
    k_cmp, v_cmp, k_slc, v_slc, k_win, v_win = [kv[:, :, i] for i in range(6)]
    n_cmp = (t_len - NSA_CMP_BLOCK) // NSA_CMP_STRIDE + 1
    tok = np.arange(n_cmp)[:, None] * NSA_CMP_STRIDE + np.arange(NSA_CMP_BLOCK)[None]

    def compress(src, pe, w1, w2):
        blocks = src[:, tok] + pe[None, None, :, None, :]
        blocks = blocks.transpose(0, 1, 3, 2, 4).reshape(bsz, n_cmp, G, NSA_CMP_BLOCK * dh)
        return jax.nn.silu(blocks @ w1) @ w2

    kc = compress(k_cmp, cmp_pe[0], cmp_w1[0], cmp_w2[0])
    vc = compress(v_cmp, cmp_pe[1], cmp_w1[1], cmp_w2[1])
    cmp_end = jnp.asarray(np.arange(n_cmp) * NSA_CMP_STRIDE + NSA_CMP_BLOCK - 1)
    n_slc = t_len // NSA_SLC_BLOCK
    cs = np.arange(n_cmp) * NSA_CMP_STRIDE
    ss = np.arange(n_slc) * NSA_SLC_BLOCK
    ov = np.clip(np.minimum(cs[:, None] + NSA_CMP_BLOCK, ss[None] + NSA_SLC_BLOCK) - np.maximum(cs[:, None], ss[None]), 0, None)
    overlap = jnp.asarray(ov / NSA_CMP_BLOCK, dtype=jnp.float32)
    top = min(NSA_TOPK, n_slc)
    ks = k_slc.reshape(bsz, n_slc, NSA_SLC_BLOCK, G, dh).transpose(0, 3, 1, 2, 4)
    vs = v_slc.reshape(bsz, n_slc, NSA_SLC_BLOCK, G, dh).transpose(0, 3, 1, 2, 4)
    kwp = jnp.pad(k_win, ((0, 0), (NSA_WINDOW, 0), (0, 0), (0, 0)))
    vwp = jnp.pad(v_win, ((0, 0), (NSA_WINDOW, 0), (0, 0), (0, 0)))
    bidx = jnp.arange(bsz)[:, None, None, None]
    gidx = jnp.arange(G)[None, None, :, None]
    jsl = jnp.arange(n_slc)

    def block_fn(args):
        qb, blk = args
        t = blk * Q + jnp.arange(Q)
        s = jnp.einsum('bqghd,bngd->bqghn', qb, kc)
        vmask = (cmp_end[None, :] <= t[:, None])[None, :, None, None, :]
        p_cmp = jnp.where(vmask, jax.nn.softmax(jnp.where(vmask, s, NEG), axis=-1), 0.0)
        o_cmp = jnp.einsum('bqghn,bngd->bqghd', p_cmp, vc)
        imp = jnp.einsum('bqghn,nj->bqgj', p_cmp, overlap)
        cur = t // NSA_SLC_BLOCK
        forced = (jsl[None] == 0) | (jsl[None] == cur[:, None]) | (jsl[None] == cur[:, None] - 1)
        slc_ok = jsl[None] * NSA_SLC_BLOCK <= t[:, None]
        imp = jnp.where(slc_ok[None, :, None, :], imp + jnp.where(forced, NSA_FORCE_BONUS, 0.0)[None, :, None, :], NEG)
        top_val, top_idx = lax.top_k(imp, top)
        chosen = top_val > 0.5 * NEG
        kg = ks[bidx, gidx, top_idx]
        vg = vs[bidx, gidx, top_idx]
        key_pos = top_idx[..., None] * NSA_SLC_BLOCK + jnp.arange(NSA_SLC_BLOCK)
        smask = chosen[..., None] & (key_pos <= t[None, :, None, None, None])
        s = jnp.einsum('bqghd,bqgkld->bqghkl', qb, kg)
        s = jnp.where(smask[:, :, :, None], s, NEG).reshape(bsz, Q, G, HG, top * NSA_SLC_BLOCK)
        p_slc = jax.nn.softmax(s, axis=-1).reshape(bsz, Q, G, HG, top, NSA_SLC_BLOCK)
        o_slc = jnp.einsum('bqghkl,bqgkld->bqghd', p_slc, vg)
        kw = lax.dynamic_slice_in_dim(kwp, blk * Q, NSA_WINDOW + Q, axis=1)
        vw = lax.dynamic_slice_in_dim(vwp, blk * Q, NSA_WINDOW + Q, axis=1)
        wpos = blk * Q - NSA_WINDOW + jnp.arange(NSA_WINDOW + Q)
        wmask = (wpos[None, :] >= 0) & (wpos[None, :] <= t[:, None]) & (wpos[None, :] > t[:, None] - NSA_WINDOW)
        s = jnp.einsum('bqghd,bkgd->bqghk', qb, kw)
        p_win = jax.nn.softmax(jnp.where(wmask[None, :, None, None, :], s, NEG), axis=-1)
        o_win = jnp.einsum('bqghk,bkgd->bqghd', p_win, vw)
        return o_cmp, o_slc, o_win

    nqb = t_len // Q
    q_blocks = jnp.moveaxis(q.reshape(bsz, nqb, Q, G, HG, dh), 1, 0)
    o_cmp, o_slc, o_win = lax.map(block_fn, (q_blocks, jnp.arange(nqb)))
    back = lambda o: jnp.moveaxis(o, 0, 1).reshape(bsz, t_len, G, HG, dh)
    o = (gates[:, :, 0, :, :, None] * back(o_cmp) + gates[:, :, 1, :, :, None] * back(o_slc)
         + gates[:, :, 2, :, :, None] * back(o_win))
    return o.reshape(bsz, t_len, MIX_W) * jax.nn.silu(gate_path)


def _gdn_mixer(conv_in, a, b, z, conv_w, conv_b, a_log, dt_bias, norm_w):
    bsz, t_len, _ = z.shape
    H, DK, DV, L = GDN_V_HEADS, GDN_DK, GDN_DV, GDN_CHUNK
    rep = GDN_V_HEADS // GDN_QK_HEADS
    nc = t_len // L
    qkv = jax.nn.silu(_causal_conv(conv_in, conv_w, conv_b))
    q = _l2norm(qkv[..., :GDN_QKW].reshape(bsz, t_len, GDN_QK_HEADS, DK))
    k = _l2norm(qkv[..., GDN_QKW:2 * GDN_QKW].reshape(bsz, t_len, GDN_QK_HEADS, DK))
    q = jnp.repeat(q, rep, axis=2).reshape(bsz, nc, L, H, DK) * DK ** -0.5
    k = jnp.repeat(k, rep, axis=2).reshape(bsz, nc, L, H, DK)
    v = qkv[..., 2 * GDN_QKW:].reshape(bsz, nc, L, H, DV)
    beta = jax.nn.sigmoid(b).reshape(bsz, nc, L, H)
    g = (-jnp.exp(a_log) * jax.nn.softplus(a + dt_bias)).reshape(bsz, nc, L, H)
    gc = jnp.cumsum(g, axis=2)
    gct = jnp.moveaxis(gc, 2, -1)
    incl = np.tril(np.ones((L, L), bool))
    strict = np.tril(np.ones((L, L), bool), -1)
    decay = jnp.exp(jnp.where(incl, gct[..., :, None] - gct[..., None, :], -jnp.inf))
    kb = k * beta[..., None]
    vb = v * beta[..., None]
    lower = jnp.where(strict, jnp.einsum('bclhd,bcshd->bchls', kb, k) * decay, 0.0)
    eye = jnp.eye(L, dtype=lower.dtype)
    tmat = lax.linalg.triangular_solve(lower + eye, jnp.broadcast_to(eye, lower.shape),
                                       left_side=True, lower=True, unit_diagonal=True)
    u_val = jnp.einsum('bchls,bcshv->bclhv', tmat, vb)
    w_val = jnp.einsum('bchls,bcshd->bclhd', tmat, kb * jnp.exp(gc)[..., None])
    qk = jnp.einsum('bclhd,bcshd->bchls', q, k) * decay
    q_dec = q * jnp.exp(gc)[..., None]
    k_dec = k * jnp.exp(gc[:, :, -1:] - gc)[..., None]
    c_dec = jnp.exp(gc[:, :, -1])

    def step(state, xs):
        u_c, w_c, qd_c, qk_c, kd_c, cd_c = xs
        v_new = u_c - jnp.einsum('blhd,bhdv->blhv', w_c, state)
        o_c = jnp.einsum('blhd,bhdv->blhv', qd_c, state) + jnp.einsum('bhls,bshv->blhv', qk_c, v_new)
        state = state * cd_c[..., None, None] + jnp.einsum('blhd,blhv->bhdv', kd_c, v_new)
        return state, o_c

    s0 = jnp.zeros((bsz, H, DK, DV), u_val.dtype)
    xs = tuple(jnp.moveaxis(t_, 1, 0) for t_ in (u_val, w_val, q_dec, qk, k_dec, c_dec))
    _, o = lax.scan(step, s0, xs)
    o = jnp.moveaxis(o, 0, 1).reshape(bsz, t_len, H, DV)
    o = _rmsnorm(o, norm_w) * jax.nn.silu(z.reshape(bsz, t_len, H, DV))
    return o.reshape(bsz, t_len, MIX_W)


def _seg(u, layout, name):
    start, width = layout[name]
    return u[:, start:start + width]


def kernel(x, ln_w, final_ln_w, ev_w_in, ev_w_out, ssd_conv_w, ssd_conv_b, ssd_dt_bias, ssd_a_log, ssd_d,
           ssd_norm_w, gla_w_gate, gla_b_gate, gla_norm_w, od_w_in, od_w_out, nsa_cmp_pe, nsa_cmp_w1,
           nsa_cmp_w2, gdn_conv_w, gdn_conv_b, gdn_a_log, gdn_dt_bias, gdn_norm_w):
    bsz, t_len, d = x.shape
    assert bsz == 1 and d == D_MODEL
    xs = x.reshape(t_len, d)
    for layer in range(DEPTH):
        i = layer // 2
        if layer % 2 == 0:
            w = _pad_weight(ev_w_in[i], _EV_SEGS, _EV_LAYOUT, EV_PAD)
            u = _norm_proj(xs, ln_w[layer], w, EV_TN)
            s = lambda name: _seg(u, _EV_LAYOUT, name)[None]
            ya = _ssd_mixer(s("z"), s("xbc"), s("dt"), ssd_conv_w[i], ssd_conv_b[i], ssd_dt_bias[i],
                            ssd_a_log[i], ssd_d[i], ssd_norm_w[i])
            yb = _gla_mixer(s("q"), s("k"), s("v"), s("glr"), s("r"), gla_w_gate[i], gla_b_gate[i], gla_norm_w[i])
            xs = _out_proj(xs, ya[0], yb[0], ev_w_out[i])
        else:
            w = _pad_weight(od_w_in[i], _OD_SEGS, _OD_LAYOUT, OD_PAD)
            u = _norm_proj(xs, ln_w[layer], w, OD_TN)
            s = lambda name: _seg(u, _OD_LAYOUT, name)[None]
            yc = _nsa_mixer(s("q"), s("kv"), s("gates"), s("gp"), nsa_cmp_pe[i], nsa_cmp_w1[i], nsa_cmp_w2[i])
            ab = s("ab")
            yd = _gdn_mixer(s("conv"), ab[..., :GDN_V_HEADS], ab[..., GDN_V_HEADS:], s("z"), gdn_conv_w[i],
                            gdn_conv_b[i], gdn_a_log[i], gdn_dt_bias[i], gdn_norm_w[i])
            xs = _out_proj(xs, yc[0], yd[0], od_w_out[i])
    return _final_norm(xs, final_ln_w).reshape(bsz, t_len, d)
```

```python
import functools
import math

import numpy as np
import jax
import jax.numpy as jnp
from jax import lax
from jax.experimental import pallas as pl
from jax.experimental.pallas import tpu as pltpu

D_MODEL = 2048
DEPTH = 4
EPS = 1e-6
NEG = -1e30
MIX_W = D_MODEL

SSD_HEAD_DIM = 64
SSD_HEADS = MIX_W // SSD_HEAD_DIM
SSD_STATE = 128
SSD_GROUPS = 8
SSD_CONV = 4
SSD_CHUNK = 128
SSD_GN = SSD_GROUPS * SSD_STATE
SSD_XBC = MIX_W + 2 * SSD_GN
SSD_IN = MIX_W + SSD_XBC + SSD_HEADS

GLA_HEADS = 4
GLA_DKT = MIX_W // 2
GLA_DK = GLA_DKT // GLA_HEADS
GLA_DV = MIX_W // GLA_HEADS
GLA_RANK = 16
GLA_TAU = 16.0
GLA_CHUNK = 128
GLA_IN = 2 * GLA_DKT + 2 * MIX_W + GLA_RANK

NSA_HEADS = 16
NSA_HEAD_DIM = MIX_W // NSA_HEADS
NSA_KV_GROUPS = 4
NSA_GH = NSA_HEADS // NSA_KV_GROUPS
NSA_KVW = NSA_KV_GROUPS * NSA_HEAD_DIM
NSA_CMP_BLOCK = 32
NSA_CMP_STRIDE = 16
NSA_SLC_BLOCK = 64
NSA_TOPK = 16
NSA_WINDOW = 512
NSA_FORCE_BONUS = 1e3
NSA_IN = 2 * MIX_W + 6 * NSA_KVW + 3 * NSA_HEADS

GDN_QK_HEADS = 8
GDN_V_HEADS = 16
GDN_DK = 128
GDN_DV = MIX_W // GDN_V_HEADS
GDN_CONV = 4
GDN_CHUNK = 64
GDN_QKW = GDN_QK_HEADS * GDN_DK
GDN_CONV_CH = 2 * GDN_QKW + MIX_W
GDN_IN = GDN_CONV_CH + 2 * GDN_V_HEADS + MIX_W

LANES = 128
VMEM_LIMIT = 56 * 1024 * 1024
M_INIT = 0.1 * NEG

BF16 = jnp.bfloat16
F32 = jnp.float32


def _round_up(n, m):
    return (n + m - 1) // m * m


def _width(pieces):
    return sum(w for _, w in pieces)


def _gather_cols(w, pieces):
    cols = [jnp.zeros((w.shape[0], wd), w.dtype) if s is None else w[:, s:s + wd] for s, wd in pieces]
    return jnp.concatenate(cols, axis=1).astype(BF16)


_KV0 = MIX_W
_GATES0 = MIX_W + 6 * NSA_KVW

_OD_A = [(0, MIX_W), (_KV0 + 2 * NSA_KVW, 4 * NSA_KVW)]
OD_A_W = _width(_OD_A)
OD_A_KSLC, OD_A_VSLC, OD_A_KWIN, OD_A_VWIN = (MIX_W // LANES + i * NSA_KV_GROUPS for i in range(4))

_OD_B = [(_KV0, 2 * NSA_KVW)]
for _g in range(NSA_KV_GROUPS):
    _OD_B += [(_GATES0 + br * NSA_HEADS + _g * NSA_GH, NSA_GH) for br in range(3)] + [(None, LANES - 3 * NSA_GH)]
_OD_B += [(_GATES0 + 3 * NSA_HEADS, MIX_W), (NSA_IN, GDN_CONV_CH), (NSA_IN + GDN_CONV_CH, 2 * GDN_V_HEADS),
          (None, LANES - 2 * GDN_V_HEADS), (NSA_IN + GDN_CONV_CH + 2 * GDN_V_HEADS, MIX_W)]
OD_B_W = _width(_OD_B)
OD_B_GATES = 2 * NSA_KVW // LANES
OD_B_GP = 2 * NSA_KVW + NSA_KV_GROUPS * LANES
OD_B_CONV = OD_B_GP + MIX_W
OD_B_AB = OD_B_CONV + GDN_CONV_CH
OD_B_Z = OD_B_AB + LANES

_EV = [(0, MIX_W + SSD_XBC), (MIX_W + SSD_XBC, SSD_HEADS), (None, LANES - SSD_HEADS),
       (SSD_IN, 2 * GLA_DKT + MIX_W), (SSD_IN + 2 * GLA_DKT + MIX_W, GLA_RANK), (None, LANES - GLA_RANK),
       (SSD_IN + 2 * GLA_DKT + MIX_W + GLA_RANK, MIX_W)]
EV_W = _width(_EV)
EV_Z, EV_XBC, EV_DT = 0, MIX_W, MIX_W + SSD_XBC
EV_Q = EV_DT + LANES
EV_K = EV_Q + GLA_DKT
EV_V = EV_K + GLA_DKT
EV_GLR = EV_V + MIX_W
EV_R = EV_GLR + LANES


def _pick_tn(n, cap=2048):
    best = LANES
    for k in range(1, n // LANES + 1):
        if (n // LANES) % k == 0 and k * LANES <= cap:
            best = k * LANES
    return best


def _norm_proj_kernel(x_ref, g_ref, w_ref, cs_ref, o_ref, h_ref):
    @pl.when(pl.program_id(1) == 0)
    def _():
        x = x_ref[...]
        ms = jnp.mean(x * x, axis=-1, keepdims=True)
        h_ref[...] = (x * lax.rsqrt(ms + EPS) * g_ref[...]).astype(h_ref.dtype)

    acc = jnp.dot(h_ref[...], w_ref[...], preferred_element_type=F32)
    o_ref[...] = (acc * cs_ref[...]).astype(o_ref.dtype)


def _norm_proj(x, g, w, colscale, out_dtype, tm=512):
    t_len, d = x.shape
    n = w.shape[1]
    tn = _pick_tn(n)
    return pl.pallas_call(
        _norm_proj_kernel,
        out_shape=jax.ShapeDtypeStruct((t_len, n), out_dtype),
        grid=(t_len // tm, n // tn),
        in_specs=[
            pl.BlockSpec((tm, d), lambda i, j: (i, 0)),
            pl.BlockSpec((1, d), lambda i, j: (0, 0)),
            pl.BlockSpec((d, tn), lambda i, j: (0, j)),
            pl.BlockSpec((1, tn), lambda i, j: (0, j)),
        ],
        out_specs=pl.BlockSpec((tm, tn), lambda i, j: (i, j)),
        scratch_shapes=[pltpu.VMEM((tm, d), BF16)],
        compiler_params=pltpu.CompilerParams(
            dimension_semantics=("parallel", "arbitrary"), vmem_limit_bytes=VMEM_LIMIT),
        name="norm_proj",
    )(x, g.reshape(1, d), w, colscale)


def _out_proj_kernel(x_ref, ya_ref, yb_ref, wa_ref, wb_ref, o_ref):
    acc = jnp.dot(ya_ref[...].astype(BF16), wa_ref[...], preferred_element_type=F32)
    acc += jnp.dot(yb_ref[...].astype(BF16), wb_ref[...], preferred_element_type=F32)
    o_ref[...] = x_ref[...] + acc


def _out_proj(x, ya, yb, w, tm=256, tn=1024):
    t_len, d = x.shape
    k = ya.shape[1]
    wa = w[:k].astype(BF16)
    wb = w[k:].astype(BF16)
    return pl.pallas_call(
        _out_proj_kernel,
        out_shape=jax.ShapeDtypeStruct((t_len, d), F32),
        grid=(d // tn, t_len // tm),
        in_specs=[
            pl.BlockSpec((tm, tn), lambda j, i: (i, j)),
            pl.BlockSpec((tm, k), lambda j, i: (i, 0)),
            pl.BlockSpec((tm, k), lambda j, i: (i, 0)),
            pl.BlockSpec((k, tn), lambda j, i: (0, j)),
            pl.BlockSpec((k, tn), lambda j, i: (0, j)),
        ],
        out_specs=pl.BlockSpec((tm, tn), lambda j, i: (i, j)),
        compiler_params=pltpu.CompilerParams(
            dimension_semantics=("parallel", "parallel"), vmem_limit_bytes=VMEM_LIMIT),
        name="out_proj",
    )(x, ya, yb, wa, wb)


def _rmsnorm_kernel(x_ref, g_ref, o_ref):
    x = x_ref[...]
    ms = jnp.mean(x * x, axis=-1, keepdims=True)
    o_ref[...] = x * lax.rsqrt(ms + EPS) * g_ref[...]


def _final_norm(x, g, tm=512):
    t_len, d = x.shape
    return pl.pallas_call(
        _rmsnorm_kernel,
        out_shape=jax.ShapeDtypeStruct((t_len, d), F32),
        grid=(t_len // tm,),
        in_specs=[pl.BlockSpec((tm, d), lambda i: (i, 0)), pl.BlockSpec((1, d), lambda i: (0, 0))],
        out_specs=pl.BlockSpec((tm, d), lambda i: (i, 0)),
        compiler_params=pltpu.CompilerParams(dimension_semantics=("parallel",)),
        name="final_norm",
    )(x, g.reshape(1, d))


NSA_TQ = 128
NSA_KT = 256


def _nsa_compress_kernel(u_ref, pe_ref, w1_ref, w2_ref, o_ref):
    nc = o_ref.shape[2]
    dh = NSA_HEAD_DIM
    acc_a = jnp.zeros((nc, dh), F32)
    acc_b = jnp.zeros((nc, dh), F32)
    for l in range(NSA_CMP_STRIDE):
        x = u_ref[pl.ds(l, nc, stride=NSA_CMP_STRIDE), :]
        xa = (x + pe_ref[0, l:l + 1, :]).astype(BF16)
        xb = (x + pe_ref[0, NSA_CMP_STRIDE + l:NSA_CMP_STRIDE + l + 1, :]).astype(BF16)
        acc_a += jnp.dot(xa, w1_ref[0, l * dh:(l + 1) * dh, :].astype(BF16), preferred_element_type=F32)
        acc_b += jnp.dot(xb, w1_ref[0, (NSA_CMP_STRIDE + l) * dh:(NSA_CMP_STRIDE + l + 1) * dh, :].astype(BF16),
                         preferred_element_type=F32)
    h = acc_a + pltpu.roll(acc_b, nc - 1, axis=0)
    h = h * jax.nn.sigmoid(h)
    o_ref[0, 0] = jnp.dot(h.astype(BF16), w2_ref[0].astype(BF16), preferred_element_type=F32).astype(o_ref.dtype)


def _nsa_compress(ub, pe, w1, w2):
    t_len = ub.shape[0]
    nc = t_len // NSA_CMP_STRIDE
    g_n, dh = NSA_KV_GROUPS, NSA_HEAD_DIM
    return pl.pallas_call(
        _nsa_compress_kernel,
        out_shape=jax.ShapeDtypeStruct((2, g_n, nc, dh), BF16),
        grid=(2, g_n),
        in_specs=[
            pl.BlockSpec((t_len, dh), lambda c, g: (0, c * g_n + g)),
            pl.BlockSpec((1, NSA_CMP_BLOCK, dh), lambda c, g: (c, 0, 0)),
            pl.BlockSpec((1, NSA_CMP_BLOCK * dh, dh), lambda c, g: (c, 0, 0)),
            pl.BlockSpec((1, dh, dh), lambda c, g: (c, 0, 0)),
        ],
        out_specs=pl.BlockSpec((1, 1, nc, dh), lambda c, g: (c, g, 0, 0)),
        compiler_params=pltpu.CompilerParams(
            dimension_semantics=("parallel", "parallel"), vmem_limit_bytes=VMEM_LIMIT),
        name="nsa_compress",
    )(ub, pe, w1, w2)


def _nsa_cmp_kernel(q_ref, kc_ref, vc_ref, ov_ref, ocmp_ref, sel_ref):
    tq = q_ref.shape[0]
    nc = kc_ref.shape[2]
    ns = ov_ref.shape[1]
    dh = NSA_HEAD_DIM
    t = pl.program_id(1) * tq + lax.broadcasted_iota(jnp.int32, (tq, 1), 0)
    n_idx = lax.broadcasted_iota(jnp.int32, (1, nc), 1)
    valid = n_idx * NSA_CMP_STRIDE + (NSA_CMP_BLOCK - 1) <= t
    kc = kc_ref[0, 0]
    vc = vc_ref[0, 0]
    psum = jnp.zeros((tq, nc), F32)
    for h in range(NSA_GH):
        qh = q_ref[:, h * dh:(h + 1) * dh]
        s = lax.dot_general(qh, kc, (((1,), (1,)), ((), ())), preferred_element_type=F32)
        s = jnp.where(valid, s, NEG)
        m = jnp.max(s, axis=-1, keepdims=True)
        p = jnp.where(valid, jnp.exp(s - m), 0.0)
        l = jnp.sum(p, axis=-1, keepdims=True)
        p = p / jnp.where(l > 0.0, l, 1.0)
        ocmp_ref[:, h * dh:(h + 1) * dh] = jnp.dot(p.astype(BF16), vc, preferred_element_type=F32)
        psum += p
    hi = psum.astype(BF16)
    lo = (psum - hi.astype(F32)).astype(BF16)
    ov = ov_ref[...]
    imp = jnp.dot(hi, ov, preferred_element_type=F32) + jnp.dot(lo, ov, preferred_element_type=F32)
    j = lax.broadcasted_iota(jnp.int32, (1, ns), 1)
    cur = lax.shift_right_logical(t, int(math.log2(NSA_SLC_BLOCK)))
    forced = (j == 0) | (j == cur) | (j == cur - 1)
    ok = j * NSA_SLC_BLOCK <= t
    imp = jnp.where(ok, imp + jnp.where(forced, NSA_FORCE_BONUS, 0.0), NEG)
    jf = j.astype(F32)

    def pick(_, carry):
        imp, sel = carry
        mx = jnp.max(imp, axis=-1, keepdims=True)
        first = jnp.min(jnp.where(imp == mx, jf, float(ns)), axis=-1, keepdims=True)
        hit = jf == first
        sel = jnp.where(hit & (mx > 0.5 * NEG), 1.0, sel)
        imp = jnp.where(hit, -jnp.inf, imp)
        return imp, sel

    _, sel = lax.fori_loop(0, min(NSA_TOPK, ns), pick, (imp, jnp.zeros((tq, ns), F32)))
    sel_ref[0] = sel.astype(sel_ref.dtype)


def _nsa_cmp(qa, kvc, overlap):
    t_len = qa.shape[0]
    tq = min(NSA_TQ, t_len)
    nc, ns = overlap.shape
    g_n, dh = NSA_KV_GROUPS, NSA_HEAD_DIM
    gw = NSA_GH * dh
    return pl.pallas_call(
        _nsa_cmp_kernel,
        out_shape=(jax.ShapeDtypeStruct((t_len, MIX_W), F32), jax.ShapeDtypeStruct((g_n, t_len, ns), BF16)),
        grid=(g_n, t_len // tq),
        in_specs=[
            pl.BlockSpec((tq, gw), lambda g, i: (i, g)),
            pl.BlockSpec((1, 1, nc, dh), lambda g, i: (0, g, 0, 0)),
            pl.BlockSpec((1, 1, nc, dh), lambda g, i: (1, g, 0, 0)),
            pl.BlockSpec((nc, ns), lambda g, i: (0, 0)),
        ],
        out_specs=(pl.BlockSpec((tq, gw), lambda g, i: (i, g)), pl.BlockSpec((1, tq, ns), lambda g, i: (g, i, 0))),
        compiler_params=pltpu.CompilerParams(
            dimension_semantics=("parallel", "parallel"), vmem_limit_bytes=VMEM_LIMIT),
        name="nsa_cmp",
    )(qa, kvc, kvc, overlap)


def _flash_init(m_ref, l_ref, acc_ref):
    m_ref[...] = jnp.full(m_ref.shape, M_INIT, F32)
    l_ref[...] = jnp.zeros(l_ref.shape, F32)
    acc_ref[...] = jnp.zeros(acc_ref.shape, F32)


def _flash_tile(q_ref, k_tile, v_tile, mask, m_ref, l_ref, acc_ref):
    dh = NSA_HEAD_DIM
    for h in range(NSA_GH):
        qh = q_ref[:, h * dh:(h + 1) * dh]
        s = lax.dot_general(qh, k_tile, (((1,), (1,)), ((), ())), preferred_element_type=F32)
        s = jnp.where(mask, s, NEG)
        m_old = m_ref[h]
        m_new = jnp.maximum(m_old, jnp.max(s, axis=-1, keepdims=True))
        alpha = jnp.exp(m_old - m_new)
        p = jnp.exp(s - m_new)
        l_ref[h] = alpha * l_ref[h] + jnp.sum(p, axis=-1, keepdims=True)
        acc_ref[h] = alpha * acc_ref[h] + jnp.dot(p.astype(BF16), v_tile, preferred_element_type=F32)
        m_ref[h] = m_new


def _nsa_slc_kernel(q_ref, k_ref, v_ref, sel_ref, o_ref, m_ref, l_ref, acc_ref):
    tq = q_ref.shape[0]
    ns = sel_ref.shape[2]
    kt = min(NSA_KT, k_ref.shape[0])
    bpt = kt // NSA_SLC_BLOCK
    dh = NSA_HEAD_DIM
    t0 = pl.program_id(1) * tq
    t = t0 + lax.broadcasted_iota(jnp.int32, (tq, 1), 0)
    sel = sel_ref[0]
    b_idx = lax.broadcasted_iota(jnp.int32, (ns, 1), 0)
    c_idx = lax.broadcasted_iota(jnp.int32, (1, kt), 1)
    c_blk = lax.shift_right_logical(c_idx, int(math.log2(NSA_SLC_BLOCK)))
    _flash_init(m_ref, l_ref, acc_ref)

    def body(jt, _):
        k0 = pl.multiple_of(jt * kt, kt)
        expand = jnp.where(b_idx == c_blk + jt * bpt, 1.0, 0.0).astype(BF16)
        picked = jnp.dot(sel, expand, preferred_element_type=F32)
        mask = (picked > 0.5) & (c_idx + k0 <= t)
        _flash_tile(q_ref, k_ref[pl.ds(k0, kt), :], v_ref[pl.ds(k0, kt), :], mask, m_ref, l_ref, acc_ref)
        return 0

    lax.fori_loop(0, (t0 + tq - 1) // kt + 1, body, 0)
    for h in range(NSA_GH):
        o_ref[:, h * dh:(h + 1) * dh] = acc_ref[h] / l_ref[h]


def _nsa_win_kernel(q_ref, k_ref, v_ref, ocmp_ref, oslc_ref, gates_ref, gp_ref, o_ref, m_ref, l_ref, acc_ref):
    tq = q_ref.shape[0]
    kt = min(NSA_KT, k_ref.shape[0])
    dh = NSA_HEAD_DIM
    t0 = pl.program_id(1) * tq
    t = t0 + lax.broadcasted_iota(jnp.int32, (tq, 1), 0)
    c_idx = lax.broadcasted_iota(jnp.int32, (1, kt), 1)
    _flash_init(m_ref, l_ref, acc_ref)

    def body(jt, _):
        k0 = pl.multiple_of(jt * kt, kt)
        key = c_idx + k0
        mask = (key <= t) & (key > t - NSA_WINDOW)
        _flash_tile(q_ref, k_ref[pl.ds(k0, kt), :], v_ref[pl.ds(k0, kt), :], mask, m_ref, l_ref, acc_ref)
        return 0

    lo = jnp.maximum(t0 - (NSA_WINDOW - 1), 0) // kt
    lax.fori_loop(lo, (t0 + tq - 1) // kt + 1, body, 0)
    gates = jax.nn.sigmoid(gates_ref[...])
    for h in range(NSA_GH):
        cols = slice(h * dh, (h + 1) * dh)
        o = (gates[:, h:h + 1] * ocmp_ref[:, cols]
             + gates[:, NSA_GH + h:NSA_GH + h + 1] * oslc_ref[:, cols]
             + gates[:, 2 * NSA_GH + h:2 * NSA_GH + h + 1] * (acc_ref[h] / l_ref[h]))
        gp = gp_ref[:, cols]
        o_ref[:, cols] = (o * (gp * jax.nn.sigmoid(gp))).astype(o_ref.dtype)


def _nsa_scratch(tq):
    return [pltpu.VMEM((NSA_GH, tq, 1), F32), pltpu.VMEM((NSA_GH, tq, 1), F32),
            pltpu.VMEM((NSA_GH, tq, NSA_HEAD_DIM), F32)]


def _nsa_slc(qa, sel):
    t_len = qa.shape[0]
    tq = min(NSA_TQ, t_len)
    ns = sel.shape[2]
    g_n, dh = NSA_KV_GROUPS, NSA_HEAD_DIM
    gw = NSA_GH * dh
    return pl.pallas_call(
        _nsa_slc_kernel,
        out_shape=jax.ShapeDtypeStruct((t_len, MIX_W), F32),
        grid=(g_n, t_len // tq),
        in_specs=[
            pl.BlockSpec((tq, gw), lambda g, i: (i, g)),
            pl.BlockSpec((t_len, dh), lambda g, i: (0, OD_A_KSLC + g)),
            pl.BlockSpec((t_len, dh), lambda g, i: (0, OD_A_VSLC + g)),
            pl.BlockSpec((1, tq, ns), lambda g, i: (g, i, 0)),
        ],
        out_specs=pl.BlockSpec((tq, gw), lambda g, i: (i, g)),
        scratch_shapes=_nsa_scratch(tq),
        compiler_params=pltpu.CompilerParams(
            dimension_semantics=("parallel", "parallel"), vmem_limit_bytes=VMEM_LIMIT),
        name="nsa_slc",
    )(qa, qa, qa, sel)


def _nsa_win(qa, ub, ocmp, oslc):
    t_len = qa.shape[0]
    tq = min(NSA_TQ, t_len)
    g_n, dh = NSA_KV_GROUPS, NSA_HEAD_DIM
    gw = NSA_GH * dh
    row = lambda g, i: (i, g)
    return pl.pallas_call(
        _nsa_win_kernel,
        out_shape=jax.ShapeDtypeStruct((t_len, MIX_W), BF16),
        grid=(g_n, t_len // tq),
        in_specs=[
            pl.BlockSpec((tq, gw), row),
            pl.BlockSpec((t_len, dh), lambda g, i: (0, OD_A_KWIN + g)),
            pl.BlockSpec((t_len, dh), lambda g, i: (0, OD_A_VWIN + g)),
            pl.BlockSpec((tq, gw), row),
            pl.BlockSpec((tq, gw), row),
            pl.BlockSpec((tq, LANES), lambda g, i: (i, OD_B_GATES + g)),
            pl.BlockSpec((tq, gw), lambda g, i: (i, OD_B_GP // gw + g)),
        ],
        out_specs=pl.BlockSpec((tq, gw), row),
        scratch_shapes=_nsa_scratch(tq),
        compiler_params=pltpu.CompilerParams(
            dimension_semantics=("parallel", "parallel"), vmem_limit_bytes=VMEM_LIMIT),
        name="nsa_win",
    )(qa, qa, qa, ocmp, oslc, ub, ub)


def _nsa_overlap(t_len):
    nc = t_len // NSA_CMP_STRIDE
    n_cmp = (t_len - NSA_CMP_BLOCK) // NSA_CMP_STRIDE + 1
    cs = np.arange(n_cmp) * NSA_CMP_STRIDE
    ss = np.arange(t_len // NSA_SLC_BLOCK) * NSA_SLC_BLOCK
    ov = np.clip(np.minimum(cs[:, None] + NSA_CMP_BLOCK, ss[None] + NSA_SLC_BLOCK)
                 - np.maximum(cs[:, None], ss[None]), 0, None) / NSA_CMP_BLOCK
    ov = np.concatenate([ov, np.zeros((nc - n_cmp, ov.shape[1]))], axis=0)
    return jnp.asarray(ov, dtype=BF16)


def _nsa_mixer(qa, ub, cmp_pe, cmp_w1, cmp_w2):
    t_len = qa.shape[0]
    kvc = _nsa_compress(ub, cmp_pe, cmp_w1, cmp_w2)
    ocmp, sel = _nsa_cmp(qa, kvc, _nsa_overlap(t_len))
    oslc = _nsa_slc(qa, sel)
    return _nsa_win(qa, ub, ocmp, oslc)


def _rmsnorm(x, w):
    return x * lax.rsqrt(jnp.mean(x * x, axis=-1, keepdims=True) + EPS) * w


def _l2norm(x):
    return x * lax.rsqrt(jnp.sum(x * x, axis=-1, keepdims=True) + EPS)


def _causal_conv(x, w, b):
    k_w, ch = w.shape
    y = lax.conv_general_dilated(x, w[:, None, :], window_strides=(1,), padding=[(k_w - 1, 0)],
                                 dimension_numbers=('NWC', 'WIO', 'NWC'), feature_group_count=ch)
    return y + b


def _chunk_scan(states, decays):
    def step(h, inp):
        s, d = inp
        return h * d + s, h
    h0 = jnp.zeros_like(states[:, 0])
    _, prev = lax.scan(step, h0, (jnp.moveaxis(states, 1, 0), jnp.moveaxis(decays, 1, 0)))
    return jnp.moveaxis(prev, 0, 1)


def _ssd_mixer(z, xbc_raw, dt_raw, conv_w, conv_b, dt_bias, a_log, d_skip, norm_w):
    bsz, t_len, _ = z.shape
    G, R, P, N, L = SSD_GROUPS, SSD_HEADS // SSD_GROUPS, SSD_HEAD_DIM, SSD_STATE, SSD_CHUNK
    nc = t_len // L
    xbc = jax.nn.silu(_causal_conv(xbc_raw, conv_w, conv_b))
    dt = jax.nn.softplus(dt_raw + dt_bias)
    a = -jnp.exp(a_log)
    x = xbc[..., :MIX_W].reshape(bsz, nc, L, G, R, P)
    bm = xbc[..., MIX_W:MIX_W + SSD_GN].reshape(bsz, nc, L, G, N)
    cm = xbc[..., MIX_W + SSD_GN:].reshape(bsz, nc, L, G, N)
    dt = dt.reshape(bsz, nc, L, G, R)
    cum = jnp.cumsum(dt * a.reshape(G, R), axis=2)
    xdt = x * dt[..., None]
    causal = np.tril(np.ones((L, L), bool))
    cum_t = jnp.moveaxis(cum, 2, -1)
    decay = jnp.exp(jnp.where(causal, cum_t[..., :, None] - cum_t[..., None, :], -jnp.inf))
    scores = jnp.einsum('bclgn,bcsgn->bcgls', cm, bm)
    y = jnp.einsum('bcgrls,bcsgrp->bclgrp', scores[:, :, :, None] * decay, xdt)
    to_end = jnp.exp(cum[:, :, -1:] - cum)
    states = jnp.einsum('bclgn,bclgrp->bcgrpn', bm, xdt * to_end[..., None])
    prev = _chunk_scan(states, jnp.exp(cum[:, :, -1])[..., None, None])
    y = y + jnp.einsum('bclgn,bcgrpn->bclgrp', cm, prev) * jnp.exp(cum)[..., None]
    y = y + x * d_skip.reshape(G, R)[..., None]
    y = y.reshape(bsz, t_len, MIX_W)
    return _rmsnorm(y * jax.nn.silu(z), norm_w)


def _gla_mixer(q, k, v, glr, r, w_gate, b_gate, norm_w):
    bsz, t_len, _ = q.shape
    H, DK, DV, L = GLA_HEADS, GLA_DK, GLA_DV, GLA_CHUNK
    nc = t_len // L
    q = q.reshape(bsz, nc, L, H, DK) * DK ** -0.5
    k = k.reshape(bsz, nc, L, H, DK)
    v = v.reshape(bsz, nc, L, H, DV)
    log_a = jax.nn.log_sigmoid(glr @ w_gate + b_gate) / GLA_TAU
    cum = jnp.cumsum(log_a.reshape(bsz, nc, L, H, DK), axis=2)
    tot = cum[:, :, -1:]
    half = 0.5 * tot
    att = jnp.einsum('bclhd,bcshd->bchls', q * jnp.exp(cum - half), k * jnp.exp(half - cum))
    att = jnp.where(np.tril(np.ones((L, L), bool)), att, 0.0)
    y = jnp.einsum('bchls,bcshv->bclhv', att, v)
    states = jnp.einsum('bclhd,bclhv->bchdv', k * jnp.exp(tot - cum), v)
    prev = _chunk_scan(states, jnp.exp(tot[:, :, 0])[..., None])
    y = y + jnp.einsum('bclhd,bchdv->bclhv', q * jnp.exp(cum), prev)
    y = _rmsnorm(y.reshape(bsz, t_len, H, DV), norm_w).reshape(bsz, t_len, MIX_W)
    return y * jax.nn.silu(r)


def _gdn_mixer(conv_in, a, b, z, conv_w, conv_b, a_log, dt_bias, norm_w):
    bsz, t_len, _ = z.shape
    H, DK, DV, L = GDN_V_HEADS, GDN_DK, GDN_DV, GDN_CHUNK
    rep = GDN_V_HEADS // GDN_QK_HEADS
    nc = t_len // L
    qkv = jax.nn.silu(_causal_conv(conv_in, conv_w, conv_b))
    q = _l2norm(qkv[..., :GDN_QKW].reshape(bsz, t_len, GDN_QK_HEADS, DK))
    k = _l2norm(qkv[..., GDN_QKW:2 * GDN_QKW].reshape(bsz, t_len, GDN_QK_HEADS, DK))
    q = jnp.repeat(q, rep, axis=2).reshape(bsz, nc, L, H, DK) * DK ** -0.5
    k = jnp.repeat(k, rep, axis=2).reshape(bsz, nc, L, H, DK)
    v = qkv[..., 2 * GDN_QKW:].reshape(bsz, nc, L, H, DV)
    beta = jax.nn.sigmoid(b).reshape(bsz, nc, L, H)
    g = (-jnp.exp(a_log) * jax.nn.softplus(a + dt_bias)).reshape(bsz, nc, L, H)
    gc = jnp.cumsum(g, axis=2)
    gct = jnp.moveaxis(gc, 2, -1)
    incl = np.tril(np.ones((L, L), bool))
    strict = np.tril(np.ones((L, L), bool), -1)
    decay = jnp.exp(jnp.where(incl, gct[..., :, None] - gct[..., None, :], -jnp.inf))
    kb = k * beta[..., None]
    vb = v * beta[..., None]
    lower = jnp.where(strict, jnp.einsum('bclhd,bcshd->bchls', kb, k) * decay, 0.0)
    eye = jnp.eye(L, dtype=lower.dtype)
    tmat = lax.linalg.triangular_solve(lower + eye, jnp.broadcast_to(eye, lower.shape),
                                       left_side=True, lower=True, unit_diagonal=True)
    u_val = jnp.einsum('bchls,bcshv->bclhv', tmat, vb)
    w_val = jnp.einsum('bchls,bcshd->bclhd', tmat, kb * jnp.exp(gc)[..., None])
    qk = jnp.einsum('bclhd,bcshd->bchls', q, k) * decay
    q_dec = q * jnp.exp(gc)[..., None]
    k_dec = k * jnp.exp(gc[:, :, -1:] - gc)[..., None]
    c_dec = jnp.exp(gc[:, :, -1])

    def step(state, xs):
        u_c, w_c, qd_c, qk_c, kd_c, cd_c = xs
        v_new = u_c - jnp.einsum('blhd,bhdv->blhv', w_c, state)
        o_c = jnp.einsum('blhd,bhdv->blhv', qd_c, state) + jnp.einsum('bhls,bshv->blhv', qk_c, v_new)
        state = state * cd_c[..., None, None] + jnp.einsum('blhd,blhv->bhdv', kd_c, v_new)
        return state, o_c

    s0 = jnp.zeros((bsz, H, DK, DV), u_val.dtype)
    xs = tuple(jnp.moveaxis(t_, 1, 0) for t_ in (u_val, w_val, q_dec, qk, k_dec, c_dec))
    _, o = lax.scan(step, s0, xs)
    o = jnp.moveaxis(o, 0, 1).reshape(bsz, t_len, H, DV)
    o = _rmsnorm(o, norm_w) * jax.nn.silu(z.reshape(bsz, t_len, H, DV))
    return o.reshape(bsz, t_len, MIX_W)


def _even_layer(xs, ln_w, w_in, w_out, conv_w, conv_b, dt_bias, a_log, d_skip, ssd_norm_w, w_gate, b_gate,
                gla_norm_w):
    u = _norm_proj(xs, ln_w, _gather_cols(w_in, _EV), jnp.ones((1, EV_W), F32), F32)
    s = lambda start, width: u[None, :, start:start + width]
    ya = _ssd_mixer(s(EV_Z, MIX_W), s(EV_XBC, SSD_XBC), s(EV_DT, SSD_HEADS), conv_w, conv_b, dt_bias, a_log,
                    d_skip, ssd_norm_w)
    yb = _gla_mixer(s(EV_Q, GLA_DKT), s(EV_K, GLA_DKT), s(EV_V, MIX_W), s(EV_GLR, GLA_RANK), s(EV_R, MIX_W),
                    w_gate, b_gate, gla_norm_w)
    return _out_proj(xs, ya[0], yb[0], w_out)


def _odd_layer(xs, ln_w, w_in, w_out, cmp_pe, cmp_w1, cmp_w2, conv_w, conv_b, a_log, dt_bias, gdn_norm_w):
    q_scale = np.ones((1, OD_A_W), np.float32)
    q_scale[:, :MIX_W] = NSA_HEAD_DIM ** -0.5
    qa = _norm_proj(xs, ln_w, _gather_cols(w_in, _OD_A), jnp.asarray(q_scale), BF16)
    ub = _norm_proj(xs, ln_w, _gather_cols(w_in, _OD_B), jnp.ones((1, OD_B_W), F32), F32)
    yc = _nsa_mixer(qa, ub, cmp_pe, cmp_w1, cmp_w2)
    s = lambda start, width: ub[None, :, start:start + width]
    yd = _gdn_mixer(s(OD_B_CONV, GDN_CONV_CH), s(OD_B_AB, GDN_V_HEADS), s(OD_B_AB + GDN_V_HEADS, GDN_V_HEADS),
                    s(OD_B_Z, MIX_W), conv_w, conv_b, a_log, dt_bias, gdn_norm_w)
    return _out_proj(xs, yc, yd[0], w_out)


def kernel(x, ln_w, final_ln_w, ev_w_in, ev_w_out, ssd_conv_w, ssd_conv_b, ssd_dt_bias, ssd_a_log, ssd_d,
           ssd_norm_w, gla_w_gate, gla_b_gate, gla_norm_w, od_w_in, od_w_out, nsa_cmp_pe, nsa_cmp_w1,
           nsa_cmp_w2, gdn_conv_w, gdn_conv_b, gdn_a_log, gdn_dt_bias, gdn_norm_w):
    bsz, t_len, d = x.shape
    assert bsz == 1 and d == D_MODEL
    xs = x.reshape(t_len, d)
    for layer in range(DEPTH):
        i = layer // 2
        if layer % 2 == 0:
            xs = _even_layer(xs, ln_w[layer], ev_w_in[i], ev_w_out[i], ssd_conv_w[i], ssd_conv_b[i], ssd_dt_bias[i],
                             ssd_a_log[i], ssd_d[i], ssd_norm_w[i], gla_w_gate[i], gla_b_gate[i], gla_norm_w[i])
        else:
            xs = _odd_layer(xs, ln_w[layer], od_w_in[i], od_w_out[i], nsa_cmp_pe[i], nsa_cmp_w1[i], nsa_cmp_w2[i],
                            gdn_conv_w[i], gdn_conv_b[i], gdn_a_log[i], gdn_dt_bias[i], gdn_norm_w[i])
    return _final_norm(xs, final_ln_w).reshape(bsz, t_len, d)
```

```python
import functools
import math

import numpy as np
import jax
import jax.numpy as jnp
from jax import lax
from jax.experimental import pallas as pl
from jax.experimental.pallas import tpu as pltpu

D_MODEL = 2048
DEPTH = 4
EPS = 1e-6
NEG = -1e30
MIX_W = D_MODEL

SSD_HEAD_DIM = 64
SSD_HEADS = MIX_W // SSD_HEAD_DIM
SSD_STATE = 128
SSD_GROUPS = 8
SSD_CONV = 4
SSD_CHUNK = 128
SSD_GN = SSD_GROUPS * SSD_STATE
SSD_XBC = MIX_W + 2 * SSD_GN
SSD_IN = MIX_W + SSD_XBC + SSD_HEADS

GLA_HEADS = 4
GLA_DKT = MIX_W // 2
GLA_DK = GLA_DKT // GLA_HEADS
GLA_DV = MIX_W // GLA_HEADS
GLA_RANK = 16
GLA_TAU = 16.0
GLA_CHUNK = 128
GLA_IN = 2 * GLA_DKT + 2 * MIX_W + GLA_RANK

NSA_HEADS = 16
NSA_HEAD_DIM = MIX_W // NSA_HEADS
NSA_KV_GROUPS = 4
NSA_GH = NSA_HEADS // NSA_KV_GROUPS
NSA_KVW = NSA_KV_GROUPS * NSA_HEAD_DIM
NSA_CMP_BLOCK = 32
NSA_CMP_STRIDE = 16
NSA_SLC_BLOCK = 64
NSA_TOPK = 16
NSA_WINDOW = 512
NSA_FORCE_BONUS = 1e3
NSA_IN = 2 * MIX_W + 6 * NSA_KVW + 3 * NSA_HEADS

GDN_QK_HEADS = 8
GDN_V_HEADS = 16
GDN_DK = 128
GDN_DV = MIX_W // GDN_V_HEADS
GDN_CONV = 4
GDN_CHUNK = 64
GDN_QKW = GDN_QK_HEADS * GDN_DK
GDN_CONV_CH = 2 * GDN_QKW + MIX_W
GDN_IN = GDN_CONV_CH + 2 * GDN_V_HEADS + MIX_W

LANES = 128
VMEM_LIMIT = 56 * 1024 * 1024
M_INIT = 0.1 * NEG

BF16 = jnp.bfloat16
F32 = jnp.float32


def _round_up(n, m):
    return (n + m - 1) // m * m


def _width(pieces):
    return sum(w for _, w in pieces)


def _gather_cols(w, pieces):
    cols = [jnp.zeros((w.shape[0], wd), w.dtype) if s is None else w[:, s:s + wd] for s, wd in pieces]
    return jnp.concatenate(cols, axis=1).astype(BF16)


_KV0 = MIX_W
_GATES0 = MIX_W + 6 * NSA_KVW

_OD_AT = [(0, MIX_W), (_KV0 + 3 * NSA_KVW, NSA_KVW), (_KV0 + 5 * NSA_KVW, NSA_KVW)]
OD_AT_W = _width(_OD_AT)
OD_AT_VSLC = MIX_W // NSA_HEAD_DIM
OD_AT_VWIN = OD_AT_VSLC + NSA_KV_GROUPS
_OD_AN = [(_KV0 + 2 * NSA_KVW, NSA_KVW), (_KV0 + 4 * NSA_KVW, NSA_KVW)]
OD_AN_W = _width(_OD_AN)
OD_AN_KSLC, OD_AN_KWIN = 0, NSA_KV_GROUPS

_OD_B = [(NSA_IN, GDN_CONV_CH), (NSA_IN + GDN_CONV_CH + 2 * GDN_V_HEADS, MIX_W), (_GATES0 + 3 * NSA_HEADS, MIX_W),
         (_KV0, 2 * NSA_KVW)]
for _g in range(NSA_KV_GROUPS):
    _OD_B += [(_GATES0 + br * NSA_HEADS + _g * NSA_GH, NSA_GH) for br in range(3)] + [(None, LANES - 3 * NSA_GH)]
_OD_B += [(NSA_IN + GDN_CONV_CH, 2 * GDN_V_HEADS), (None, LANES - 2 * GDN_V_HEADS)]
OD_B_W = _width(_OD_B)
OD_B_CONV = 0
OD_B_Z = OD_B_CONV + GDN_CONV_CH
OD_B_GP = OD_B_Z + MIX_W
OD_B_KCMP = OD_B_GP + MIX_W
OD_B_VCMP = OD_B_KCMP + NSA_KVW
OD_B_GATES = OD_B_VCMP + NSA_KVW
OD_B_AB = OD_B_GATES + NSA_KV_GROUPS * LANES

_GLA_R0 = SSD_IN + 2 * GLA_DKT + MIX_W + GLA_RANK
_EV = [(0, MIX_W + SSD_XBC), (SSD_IN, 2 * GLA_DKT + MIX_W), (_GLA_R0, MIX_W),
       (MIX_W + SSD_XBC, SSD_HEADS), (None, LANES - SSD_HEADS),
       (_GLA_R0 - GLA_RANK, GLA_RANK), (None, LANES - GLA_RANK)]
EV_W = _width(_EV)
EV_Z, EV_X = 0, MIX_W
EV_B = EV_X + MIX_W
EV_C = EV_B + SSD_GN
EV_Q = EV_C + SSD_GN
EV_K = EV_Q + GLA_DKT
EV_V = EV_K + GLA_DKT
EV_R = EV_V + MIX_W
EV_DT = EV_R + MIX_W
EV_GLR = EV_DT + LANES


def _pick_tn(n, cap=2048):
    best = LANES
    for k in range(1, n // LANES + 1):
        if (n // LANES) % k == 0 and k * LANES <= cap:
            best = k * LANES
    return best


def _norm_rows(x_ref, g_ref, h_ref):
    x = x_ref[...]
    ms = jnp.mean(x * x, axis=-1, keepdims=True)
    h_ref[...] = (x * lax.rsqrt(ms + EPS) * g_ref[...]).astype(h_ref.dtype)


def _norm_proj_kernel(x_ref, g_ref, w_ref, o_ref, h_ref):
    pl.when(pl.program_id(1) == 0)(lambda: _norm_rows(x_ref, g_ref, h_ref))
    o_ref[...] = jnp.dot(h_ref[...], w_ref[...], preferred_element_type=F32).astype(o_ref.dtype)


def _norm_proj_t_kernel(x_ref, g_ref, wt_ref, rs_ref, o_ref, h_ref):
    pl.when(pl.program_id(1) == 0)(lambda: _norm_rows(x_ref, g_ref, h_ref))
    acc = lax.dot_general(wt_ref[...], h_ref[...], (((1,), (1,)), ((), ())), preferred_element_type=F32)
    o_ref[...] = (acc * rs_ref[...]).astype(o_ref.dtype)


def _norm_proj(x, g, w, out_dtype, tm=512):
    t_len, d = x.shape
    n = w.shape[1]
    tn = _pick_tn(n)
    return pl.pallas_call(
        _norm_proj_kernel,
        out_shape=jax.ShapeDtypeStruct((t_len, n), out_dtype),
        grid=(t_len // tm, n // tn),
        in_specs=[
            pl.BlockSpec((tm, d), lambda i, j: (i, 0)),
            pl.BlockSpec((1, d), lambda i, j: (0, 0)),
            pl.BlockSpec((d, tn), lambda i, j: (0, j)),
        ],
        out_specs=pl.BlockSpec((tm, tn), lambda i, j: (i, j)),
        scratch_shapes=[pltpu.VMEM((tm, d), BF16)],
        compiler_params=pltpu.CompilerParams(
            dimension_semantics=("parallel", "arbitrary"), vmem_limit_bytes=VMEM_LIMIT),
        name="norm_proj",
    )(x, g.reshape(1, d), w)


def _norm_proj_t(x, g, wt, rowscale, out_dtype, tm=512):
    t_len, d = x.shape
    n = wt.shape[0]
    tn = _pick_tn(n)
    return pl.pallas_call(
        _norm_proj_t_kernel,
        out_shape=jax.ShapeDtypeStruct((n, t_len), out_dtype),
        grid=(t_len // tm, n // tn),
        in_specs=[
            pl.BlockSpec((tm, d), lambda i, j: (i, 0)),
            pl.BlockSpec((1, d), lambda i, j: (0, 0)),
            pl.BlockSpec((tn, d), lambda i, j: (j, 0)),
            pl.BlockSpec((tn, 1), lambda i, j: (j, 0)),
        ],
        out_specs=pl.BlockSpec((tn, tm), lambda i, j: (j, i)),
        scratch_shapes=[pltpu.VMEM((tm, d), BF16)],
        compiler_params=pltpu.CompilerParams(
            dimension_semantics=("parallel", "arbitrary"), vmem_limit_bytes=VMEM_LIMIT),
        name="norm_proj_t",
    )(x, g.reshape(1, d), wt, rowscale)


def _out_proj_kernel(x_ref, ya_ref, yb_ref, wa_ref, wb_ref, o_ref):
    acc = jnp.dot(ya_ref[...].astype(BF16), wa_ref[...], preferred_element_type=F32)
    acc += jnp.dot(yb_ref[...].astype(BF16), wb_ref[...], preferred_element_type=F32)
    o_ref[...] = x_ref[...] + acc


def _out_proj(x, ya, yb, w, tm=256, tn=1024):
    t_len, d = x.shape
    k = ya.shape[1]
    wa = w[:k].astype(BF16)
    wb = w[k:].astype(BF16)
    return pl.pallas_call(
        _out_proj_kernel,
        out_shape=jax.ShapeDtypeStruct((t_len, d), F32),
        grid=(d // tn, t_len // tm),
        in_specs=[
            pl.BlockSpec((tm, tn), lambda j, i: (i, j)),
            pl.BlockSpec((tm, k), lambda j, i: (i, 0)),
            pl.BlockSpec((tm, k), lambda j, i: (i, 0)),
            pl.BlockSpec((k, tn), lambda j, i: (0, j)),
            pl.BlockSpec((k, tn), lambda j, i: (0, j)),
        ],
        out_specs=pl.BlockSpec((tm, tn), lambda j, i: (i, j)),
        compiler_params=pltpu.CompilerParams(
            dimension_semantics=("parallel", "parallel"), vmem_limit_bytes=VMEM_LIMIT),
        name="out_proj",
    )(x, ya, yb, wa, wb)


def _rmsnorm_kernel(x_ref, g_ref, o_ref):
    x = x_ref[...]
    ms = jnp.mean(x * x, axis=-1, keepdims=True)
    o_ref[...] = x * lax.rsqrt(ms + EPS) * g_ref[...]


def _final_norm(x, g, tm=512):
    t_len, d = x.shape
    return pl.pallas_call(
        _rmsnorm_kernel,
        out_shape=jax.ShapeDtypeStruct((t_len, d), F32),
        grid=(t_len // tm,),
        in_specs=[pl.BlockSpec((tm, d), lambda i: (i, 0)), pl.BlockSpec((1, d), lambda i: (0, 0))],
        out_specs=pl.BlockSpec((tm, d), lambda i: (i, 0)),
        compiler_params=pltpu.CompilerParams(dimension_semantics=("parallel",)),
        name="final_norm",
    )(x, g.reshape(1, d))


NSA_TQ = 256
NSA_KT = 256

_NT = (((1,), (1,)), ((), ()))


def _nsa_compress_hidden(u_ref, pe_ref, w1_ref, nc):
    dh = NSA_HEAD_DIM
    acc_a = jnp.zeros((nc, dh), F32)
    acc_b = jnp.zeros((nc, dh), F32)
    for l in range(NSA_CMP_STRIDE):
        x = u_ref[pl.ds(l, nc, stride=NSA_CMP_STRIDE), :]
        xa = (x + pe_ref[l:l + 1, :]).astype(BF16)
        xb = (x + pe_ref[NSA_CMP_STRIDE + l:NSA_CMP_STRIDE + l + 1, :]).astype(BF16)
        acc_a += jnp.dot(xa, w1_ref[l * dh:(l + 1) * dh, :].astype(BF16), preferred_element_type=F32)
        acc_b += jnp.dot(xb, w1_ref[(NSA_CMP_STRIDE + l) * dh:(NSA_CMP_STRIDE + l + 1) * dh, :].astype(BF16),
                         preferred_element_type=F32)
    h = acc_a + pltpu.roll(acc_b, nc - 1, axis=0)
    return (h * jax.nn.sigmoid(h)).astype(BF16)


def _nsa_compress_kernel(uk_ref, uv_ref, pe_ref, w1_ref, w2k_ref, w2vt_ref, kc_ref, vct_ref):
    nc = kc_ref.shape[1]
    hk = _nsa_compress_hidden(uk_ref, pe_ref.at[0], w1_ref.at[0], nc)
    kc_ref[0] = jnp.dot(hk, w2k_ref[...].astype(BF16), preferred_element_type=F32).astype(kc_ref.dtype)
    hv = _nsa_compress_hidden(uv_ref, pe_ref.at[1], w1_ref.at[1], nc)
    vct_ref[0] = lax.dot_general(w2vt_ref[...].astype(BF16), hv, _NT,
                                 preferred_element_type=F32).astype(vct_ref.dtype)


def _nsa_compress(ub, pe, w1, w2):
    t_len = ub.shape[0]
    nc = t_len // NSA_CMP_STRIDE
    g_n, dh = NSA_KV_GROUPS, NSA_HEAD_DIM
    return pl.pallas_call(
        _nsa_compress_kernel,
        out_shape=(jax.ShapeDtypeStruct((g_n, nc, dh), BF16), jax.ShapeDtypeStruct((g_n, dh, nc), BF16)),
        grid=(g_n,),
        in_specs=[
            pl.BlockSpec((t_len, dh), lambda g: (0, OD_B_KCMP // dh + g)),
            pl.BlockSpec((t_len, dh), lambda g: (0, OD_B_VCMP // dh + g)),
            pl.BlockSpec((2, NSA_CMP_BLOCK, dh), lambda g: (0, 0, 0)),
            pl.BlockSpec((2, NSA_CMP_BLOCK * dh, dh), lambda g: (0, 0, 0)),
            pl.BlockSpec((dh, dh), lambda g: (0, 0)),
            pl.BlockSpec((dh, dh), lambda g: (0, 0)),
        ],
        out_specs=(pl.BlockSpec((1, nc, dh), lambda g: (g, 0, 0)), pl.BlockSpec((1, dh, nc), lambda g: (g, 0, 0))),
        compiler_params=pltpu.CompilerParams(dimension_semantics=("parallel",), vmem_limit_bytes=VMEM_LIMIT),
        name="nsa_compress",
    )(ub, ub, pe, w1, w2[0], w2[1].T)


def _nsa_cmp_kernel(qt_ref, kc_ref, vct_ref, ovt_ref, ocmp_ref, sel_ref):
    tq = qt_ref.shape[1]
    nc = kc_ref.shape[1]
    ns = ovt_ref.shape[0]
    dh = NSA_HEAD_DIM
    t = pl.program_id(1) * tq + lax.broadcasted_iota(jnp.int32, (1, tq), 1)
    n_idx = lax.broadcasted_iota(jnp.int32, (nc, 1), 0)
    valid = n_idx * NSA_CMP_STRIDE + (NSA_CMP_BLOCK - 1) <= t
    kc = kc_ref[0]
    vct = vct_ref[0]
    psum = jnp.zeros((nc, tq), F32)
    for h in range(NSA_GH):
        s = jnp.dot(kc, qt_ref[h * dh:(h + 1) * dh, :], preferred_element_type=F32)
        s = jnp.where(valid, s, NEG)
        m = jnp.max(s, axis=0, keepdims=True)
        p = jnp.where(valid, jnp.exp(s - m), 0.0)
        l = jnp.sum(p, axis=0, keepdims=True)
        p = p * (1.0 / jnp.where(l > 0.0, l, 1.0))
        ocmp_ref[h * dh:(h + 1) * dh, :] = jnp.dot(vct, p.astype(BF16), preferred_element_type=F32)
        psum += p
    hi = psum.astype(BF16)
    lo = (psum - hi.astype(F32)).astype(BF16)
    ovt = ovt_ref[...]
    imp = jnp.dot(ovt, hi, preferred_element_type=F32) + jnp.dot(ovt, lo, preferred_element_type=F32)
    j = lax.broadcasted_iota(jnp.int32, (ns, 1), 0)
    cur = lax.shift_right_logical(t, int(math.log2(NSA_SLC_BLOCK)))
    forced = (j == 0) | (j == cur) | (j == cur - 1)
    ok = j * NSA_SLC_BLOCK <= t
    imp = jnp.where(ok, imp + jnp.where(forced, NSA_FORCE_BONUS, 0.0), NEG)
    jf = j.astype(F32)

    def pick(_, carry):
        imp, sel = carry
        mx = jnp.max(imp, axis=0, keepdims=True)
        first = jnp.min(jnp.where(imp == mx, jf, float(ns)), axis=0, keepdims=True)
        hit = jf == first
        sel = jnp.where(hit & (mx > 0.5 * NEG), 1.0, sel)
        imp = jnp.where(hit, -jnp.inf, imp)
        return imp, sel

    _, sel = lax.fori_loop(0, min(NSA_TOPK, ns), pick, (imp, jnp.zeros((ns, tq), F32)))
    sel_ref[0] = sel.astype(sel_ref.dtype)


def _nsa_cmp(at, kc, vct, overlap_t):
    t_len = at.shape[1]
    tq = min(NSA_TQ, t_len)
    ns, nc = overlap_t.shape
    g_n, dh = NSA_KV_GROUPS, NSA_HEAD_DIM
    gw = NSA_GH * dh
    return pl.pallas_call(
        _nsa_cmp_kernel,
        out_shape=(jax.ShapeDtypeStruct((MIX_W, t_len), F32), jax.ShapeDtypeStruct((g_n, ns, t_len), BF16)),
        grid=(g_n, t_len // tq),
        in_specs=[
            pl.BlockSpec((gw, tq), lambda g, i: (g, i)),
            pl.BlockSpec((1, nc, dh), lambda g, i: (g, 0, 0)),
            pl.BlockSpec((1, dh, nc), lambda g, i: (g, 0, 0)),
            pl.BlockSpec((ns, nc), lambda g, i: (0, 0)),
        ],
        out_specs=(pl.BlockSpec((gw, tq), lambda g, i: (g, i)), pl.BlockSpec((1, ns, tq), lambda g, i: (g, 0, i))),
        compiler_params=pltpu.CompilerParams(
            dimension_semantics=("parallel", "parallel"), vmem_limit_bytes=VMEM_LIMIT),
        name="nsa_cmp",
    )(at, kc, vct, overlap_t)


def _flash_init(m_ref, l_ref, acc_ref):
    m_ref[...] = jnp.full(m_ref.shape, M_INIT, F32)
    l_ref[...] = jnp.zeros(l_ref.shape, F32)
    acc_ref[...] = jnp.zeros(acc_ref.shape, F32)


def _flash_tile(qt_ref, k_tile, vt_tile, mask, m_ref, l_ref, acc_ref):
    dh = NSA_HEAD_DIM
    for h in range(NSA_GH):
        s = jnp.dot(k_tile, qt_ref[h * dh:(h + 1) * dh, :], preferred_element_type=F32)
        s = jnp.where(mask, s, NEG)
        m_old = m_ref[h]
        m_new = jnp.maximum(m_old, jnp.max(s, axis=0, keepdims=True))
        alpha = jnp.exp(m_old - m_new)
        p = jnp.exp(s - m_new)
        l_ref[h] = alpha * l_ref[h] + jnp.sum(p, axis=0, keepdims=True)
        acc_ref[h] = alpha * acc_ref[h] + jnp.dot(vt_tile, p.astype(BF16), preferred_element_type=F32)
        m_ref[h] = m_new


def _nsa_slc_kernel(qt_ref, k_ref, vt_ref, sel_ref, o_ref, m_ref, l_ref, acc_ref):
    tq = qt_ref.shape[1]
    ns = sel_ref.shape[1]
    kt = min(NSA_KT, k_ref.shape[0])
    bpt = kt // NSA_SLC_BLOCK
    dh = NSA_HEAD_DIM
    t0 = pl.program_id(1) * tq
    t = t0 + lax.broadcasted_iota(jnp.int32, (1, tq), 1)
    sel = sel_ref[0]
    b_idx = lax.broadcasted_iota(jnp.int32, (1, ns), 1)
    c_idx = lax.broadcasted_iota(jnp.int32, (kt, 1), 0)
    c_blk = lax.shift_right_logical(c_idx, int(math.log2(NSA_SLC_BLOCK)))
    _flash_init(m_ref, l_ref, acc_ref)

    def body(jt, _):
        k0 = pl.multiple_of(jt * kt, kt)
        expand = jnp.where(b_idx == c_blk + jt * bpt, 1.0, 0.0).astype(BF16)
        picked = jnp.dot(expand, sel, preferred_element_type=F32)
        mask = (picked > 0.5) & (c_idx + k0 <= t)
        _flash_tile(qt_ref, k_ref[pl.ds(k0, kt), :], vt_ref[:, pl.ds(k0, kt)], mask, m_ref, l_ref, acc_ref)
        return 0

    lax.fori_loop(0, (t0 + tq - 1) // kt + 1, body, 0)
    for h in range(NSA_GH):
        o_ref[h * dh:(h + 1) * dh, :] = acc_ref[h] * (1.0 / l_ref[h])


def _nsa_win_kernel(qt_ref, k_ref, vt_ref, ocmp_ref, oslc_ref, gates_ref, gp_ref, o_ref, m_ref, l_ref, acc_ref):
    tq = qt_ref.shape[1]
    kt = min(NSA_KT, k_ref.shape[0])
    dh = NSA_HEAD_DIM
    t0 = pl.program_id(1) * tq
    t = t0 + lax.broadcasted_iota(jnp.int32, (1, tq), 1)
    c_idx = lax.broadcasted_iota(jnp.int32, (kt, 1), 0)
    _flash_init(m_ref, l_ref, acc_ref)

    def body(jt, _):
        k0 = pl.multiple_of(jt * kt, kt)
        key = c_idx + k0
        mask = (key <= t) & (key > t - NSA_WINDOW)
        _flash_tile(qt_ref, k_ref[pl.ds(k0, kt), :], vt_ref[:, pl.ds(k0, kt)], mask, m_ref, l_ref, acc_ref)
        return 0

    lo = jnp.maximum(t0 - (NSA_WINDOW - 1), 0) // kt
    lax.fori_loop(lo, (t0 + tq - 1) // kt + 1, body, 0)
    gates_t = jnp.transpose(jax.nn.sigmoid(gates_ref[...]))
    for h in range(NSA_GH):
        rows = slice(h * dh, (h + 1) * dh)
        o_t = (gates_t[h:h + 1, :] * ocmp_ref[rows, :]
               + gates_t[NSA_GH + h:NSA_GH + h + 1, :] * oslc_ref[rows, :]
               + gates_t[2 * NSA_GH + h:2 * NSA_GH + h + 1, :] * (acc_ref[h] * (1.0 / l_ref[h])))
        gp = gp_ref[:, rows]
        o_ref[:, rows] = (jnp.transpose(o_t) * (gp * jax.nn.sigmoid(gp))).astype(o_ref.dtype)


def _nsa_scratch(tq):
    return [pltpu.VMEM((NSA_GH, 1, tq), F32), pltpu.VMEM((NSA_GH, 1, tq), F32),
            pltpu.VMEM((NSA_GH, NSA_HEAD_DIM, tq), F32)]


def _nsa_slc(at, an, sel):
    t_len = at.shape[1]
    tq = min(NSA_TQ, t_len)
    ns = sel.shape[1]
    g_n, dh = NSA_KV_GROUPS, NSA_HEAD_DIM
    gw = NSA_GH * dh
    return pl.pallas_call(
        _nsa_slc_kernel,
        out_shape=jax.ShapeDtypeStruct((MIX_W, t_len), F32),
        grid=(g_n, t_len // tq),
        in_specs=[
            pl.BlockSpec((gw, tq), lambda g, i: (g, i)),
            pl.BlockSpec((t_len, dh), lambda g, i: (0, OD_AN_KSLC + g)),
            pl.BlockSpec((dh, t_len), lambda g, i: (OD_AT_VSLC + g, 0)),
            pl.BlockSpec((1, ns, tq), lambda g, i: (g, 0, i)),
        ],
        out_specs=pl.BlockSpec((gw, tq), lambda g, i: (g, i)),
        scratch_shapes=_nsa_scratch(tq),
        compiler_params=pltpu.CompilerParams(
            dimension_semantics=("parallel", "parallel"), vmem_limit_bytes=VMEM_LIMIT),
        name="nsa_slc",
    )(at, an, at, sel)


def _nsa_win(at, an, ub, ocmp, oslc):
    t_len = at.shape[1]
    tq = min(NSA_TQ, t_len)
    g_n, dh = NSA_KV_GROUPS, NSA_HEAD_DIM
    gw = NSA_GH * dh
    return pl.pallas_call(
        _nsa_win_kernel,
        out_shape=jax.ShapeDtypeStruct((t_len, MIX_W), BF16),
        grid=(g_n, t_len // tq),
        in_specs=[
            pl.BlockSpec((gw, tq), lambda g, i: (g, i)),
            pl.BlockSpec((t_len, dh), lambda g, i: (0, OD_AN_KWIN + g)),
            pl.BlockSpec((dh, t_len), lambda g, i: (OD_AT_VWIN + g, 0)),
            pl.BlockSpec((gw, tq), lambda g, i: (g, i)),
            pl.BlockSpec((gw, tq), lambda g, i: (g, i)),
            pl.BlockSpec((tq, LANES), lambda g, i: (i, OD_B_GATES // LANES + g)),
            pl.BlockSpec((tq, gw), lambda g, i: (i, OD_B_GP // gw + g)),
        ],
        out_specs=pl.BlockSpec((tq, gw), lambda g, i: (i, g)),
        scratch_shapes=_nsa_scratch(tq),
        compiler_params=pltpu.CompilerParams(
            dimension_semantics=("parallel", "parallel"), vmem_limit_bytes=VMEM_LIMIT),
        name="nsa_win",
    )(at, an, at, ocmp, oslc, ub, ub)


def _nsa_overlap_t(t_len):
    nc = t_len // NSA_CMP_STRIDE
    n_cmp = (t_len - NSA_CMP_BLOCK) // NSA_CMP_STRIDE + 1
    cs = np.arange(n_cmp) * NSA_CMP_STRIDE
    ss = np.arange(t_len // NSA_SLC_BLOCK) * NSA_SLC_BLOCK
    ov = np.clip(np.minimum(cs[:, None] + NSA_CMP_BLOCK, ss[None] + NSA_SLC_BLOCK)
                 - np.maximum(cs[:, None], ss[None]), 0, None) / NSA_CMP_BLOCK
    ov = np.concatenate([ov, np.zeros((nc - n_cmp, ov.shape[1]))], axis=0)
    return jnp.asarray(ov.T, dtype=BF16)


def _nsa_mixer(at, an, ub, cmp_pe, cmp_w1, cmp_w2):
    t_len = an.shape[0]
    kc, vct = _nsa_compress(ub, cmp_pe, cmp_w1, cmp_w2)
    ocmp, sel = _nsa_cmp(at, kc, vct, _nsa_overlap_t(t_len))
    oslc = _nsa_slc(at, an, sel)
    return _nsa_win(at, an, ub, ocmp, oslc)


CONV_TAIL = 8


def _cumsum_rows(x):
    n = x.shape[0]
    row = lax.broadcasted_iota(jnp.int32, x.shape, 0)
    s = 1
    while s < n:
        x = x + jnp.where(row >= s, pltpu.roll(x, s, axis=0), 0.0)
        s *= 2
    return x


def _split_bf16(v, parts):
    out = []
    for _ in range(parts - 1):
        p = v.astype(BF16)
        out.append(p)
        v = v - p.astype(F32)
    out.append(v.astype(BF16))
    return out


def _expand_lanes(v, onehot):
    return sum(jnp.dot(p, onehot, preferred_element_type=F32) for p in _split_bf16(v, 3))


def _conv_silu(x, tail, w, b):
    n = x.shape[0]
    ext = jnp.concatenate([tail, x], axis=0)
    y = b
    for k in range(SSD_CONV):
        off = CONV_TAIL - (SSD_CONV - 1) + k
        y = y + w[k:k + 1, :] * ext[off:off + n, :]
    return y * jax.nn.sigmoid(y)


def _softplus(x):
    return jnp.maximum(x, 0.0) + jnp.log1p(jnp.exp(-jnp.abs(x)))


_TN = (((0,), (0,)), ((), ()))


def _ssd_kernel(z_ref, x_ref, b_ref, c_ref, dt_ref, cwx_ref, cwb_ref, cwc_ref, cbx_ref, cbb_ref, cbc_ref,
                dtb_ref, alog_ref, dsk_ref, nw_ref, e_ref, o_ref, tail_ref, state_ref, y_ref):
    L = SSD_CHUNK
    n, gw = SSD_STATE, (SSD_HEADS // SSD_GROUPS) * SSD_HEAD_DIM

    @pl.when(pl.program_id(0) == 0)
    def _():
        tail_ref[...] = jnp.zeros(tail_ref.shape, F32)
        state_ref[...] = jnp.zeros(state_ref.shape, F32)

    xr, br, cr = x_ref[...], b_ref[...], c_ref[...]
    x = _conv_silu(xr, tail_ref[:, :MIX_W], cwx_ref[...], cbx_ref[...])
    bm = _conv_silu(br, tail_ref[:, MIX_W:MIX_W + SSD_GN], cwb_ref[...], cbb_ref[...])
    cm = _conv_silu(cr, tail_ref[:, MIX_W + SSD_GN:], cwc_ref[...], cbc_ref[...])
    tail_ref[:, :MIX_W] = xr[L - CONV_TAIL:, :]
    tail_ref[:, MIX_W:MIX_W + SSD_GN] = br[L - CONV_TAIL:, :]
    tail_ref[:, MIX_W + SSD_GN:] = cr[L - CONV_TAIL:, :]

    dt = _softplus(dt_ref[...] + dtb_ref[...])
    cum = _cumsum_rows(dt * -jnp.exp(alog_ref[...]))
    cum_t = jnp.transpose(cum)
    onehot = e_ref[...]
    cum_e = _expand_lanes(cum, onehot)
    xdt = x * _expand_lanes(dt, onehot)
    last_e = cum_e[L - 1:L, :]
    xdt16 = xdt.astype(BF16)
    xw16 = (xdt * jnp.exp(last_e - cum_e)).astype(BF16)
    ecum_e = jnp.exp(cum_e)
    causal = lax.broadcasted_iota(jnp.int32, (L, L), 0) >= lax.broadcasted_iota(jnp.int32, (L, L), 1)
    lane = lax.broadcasted_iota(jnp.int32, (1, gw), 1)
    for g in range(SSD_GROUPS):
        cols = slice(g * gw, (g + 1) * gw)
        bg = bm[:, g * n:(g + 1) * n]
        cg16 = cm[:, g * n:(g + 1) * n].astype(BF16)
        sc = lax.dot_general(cg16, bg.astype(BF16), _NT, preferred_element_type=F32)
        state = state_ref[g]
        yg = jnp.dot(cg16, state.astype(BF16), preferred_element_type=F32) * ecum_e[:, cols]
        xg = xdt16[:, cols]
        for r in range(SSD_HEADS // SSD_GROUPS):
            h = g * (SSD_HEADS // SSD_GROUPS) + r
            decay = jnp.where(causal, jnp.exp(cum[:, h:h + 1] - cum_t[h:h + 1, :]), 0.0)
            in_head = (lane >= r * SSD_HEAD_DIM) & (lane < (r + 1) * SSD_HEAD_DIM)
            yg += jnp.dot((sc * decay).astype(BF16), jnp.where(in_head, xg, jnp.zeros_like(xg)),
                          preferred_element_type=F32)
        y_ref[:, cols] = yg
        bg_t = jnp.transpose(bg).astype(BF16)
        state_ref[g] = state * jnp.exp(last_e[:, cols]) + jnp.dot(bg_t, xw16[:, cols], preferred_element_type=F32)

    y = y_ref[...] + x * dsk_ref[...]
    zz = z_ref[...]
    y = y * (zz * jax.nn.sigmoid(zz))
    ms = jnp.mean(y * y, axis=-1, keepdims=True)
    o_ref[...] = (y * lax.rsqrt(ms + EPS) * nw_ref[...]).astype(o_ref.dtype)


def _ssd_mixer(u, conv_w, conv_b, dt_bias, a_log, d_skip, norm_w):
    t_len = u.shape[0]
    L = SSD_CHUNK
    pad = lambda v: jnp.pad(v, (0, LANES - SSD_HEADS)).reshape(1, LANES)
    onehot = np.zeros((LANES, MIX_W), np.float32)
    onehot[np.arange(MIX_W) // SSD_HEAD_DIM, np.arange(MIX_W)] = 1.0
    row = lambda width, start: pl.BlockSpec((L, width), lambda c: (c, start // width))
    fixed = lambda rows, width, start: pl.BlockSpec((rows, width), lambda c: (0, start // width))
    return pl.pallas_call(
        _ssd_kernel,
        out_shape=jax.ShapeDtypeStruct((t_len, MIX_W), BF16),
        grid=(t_len // L,),
        in_specs=[
            row(MIX_W, EV_Z), row(MIX_W, EV_X), row(SSD_GN, EV_B), row(SSD_GN, EV_C), row(LANES, EV_DT),
            fixed(SSD_CONV, MIX_W, 0), fixed(SSD_CONV, SSD_GN, MIX_W), fixed(SSD_CONV, SSD_GN, MIX_W + SSD_GN),
            fixed(1, MIX_W, 0), fixed(1, SSD_GN, MIX_W), fixed(1, SSD_GN, MIX_W + SSD_GN),
            fixed(1, LANES, 0), fixed(1, LANES, 0), fixed(1, MIX_W, 0), fixed(1, MIX_W, 0),
            fixed(LANES, MIX_W, 0),
        ],
        out_specs=pl.BlockSpec((L, MIX_W), lambda c: (c, 0)),
        scratch_shapes=[pltpu.VMEM((CONV_TAIL, SSD_XBC), F32),
                        pltpu.VMEM((SSD_GROUPS, SSD_STATE, MIX_W // SSD_GROUPS), F32),
                        pltpu.VMEM((L, MIX_W), F32)],
        compiler_params=pltpu.CompilerParams(dimension_semantics=("arbitrary",), vmem_limit_bytes=VMEM_LIMIT),
        name="ssd_mixer",
    )(u, u, u, u, u, conv_w, conv_w, conv_w, conv_b.reshape(1, -1), conv_b.reshape(1, -1), conv_b.reshape(1, -1),
      pad(dt_bias), pad(a_log), jnp.repeat(d_skip, SSD_HEAD_DIM).reshape(1, MIX_W), norm_w.reshape(1, MIX_W),
      jnp.asarray(onehot, dtype=BF16))


def _gla_kernel(q_ref, k_ref, v_ref, r_ref, glr_ref, wg_ref, bg_ref, nw_ref, o_ref, state_ref):
    L, dk, dv = GLA_CHUNK, GLA_DK, GLA_DV

    @pl.when(pl.program_id(0) == 0)
    def _():
        state_ref[...] = jnp.zeros(state_ref.shape, F32)

    g_hi, g_lo = _split_bf16(glr_ref[...], 2)
    w_hi, w_lo = _split_bf16(wg_ref[...], 2)
    logits = (jnp.dot(g_hi, w_hi, preferred_element_type=F32) + jnp.dot(g_hi, w_lo, preferred_element_type=F32)
              + jnp.dot(g_lo, w_hi, preferred_element_type=F32) + bg_ref[...])
    cum = _cumsum_rows(-_softplus(-logits) * (1.0 / GLA_TAU))
    tot = cum[L - 1:L, :]
    half = 0.5 * tot
    q = q_ref[...] * dk ** -0.5
    k = k_ref[...]
    qe = (q * jnp.exp(cum - half)).astype(BF16)
    ke = (k * jnp.exp(half - cum)).astype(BF16)
    qd = (q * jnp.exp(cum)).astype(BF16)
    kd = k * jnp.exp(tot - cum)
    etot = jnp.exp(tot)
    causal = lax.broadcasted_iota(jnp.int32, (L, L), 0) >= lax.broadcasted_iota(jnp.int32, (L, L), 1)
    for h in range(GLA_HEADS):
        ks = slice(h * dk, (h + 1) * dk)
        vs = slice(h * dv, (h + 1) * dv)
        att = lax.dot_general(qe[:, ks], ke[:, ks], _NT, preferred_element_type=F32)
        att = jnp.where(causal, att, 0.0).astype(BF16)
        vh = v_ref[:, vs].astype(BF16)
        state = state_ref[h]
        y = jnp.dot(att, vh, preferred_element_type=F32) + jnp.dot(qd[:, ks], state.astype(BF16),
                                                                   preferred_element_type=F32)
        kd_t = jnp.transpose(kd[:, ks]).astype(BF16)
        decay_col = jnp.transpose(jnp.broadcast_to(etot[:, ks], (8, dk)))[:, 0:1]
        state_ref[h] = state * decay_col + jnp.dot(kd_t, vh, preferred_element_type=F32)
        ms = jnp.mean(y * y, axis=-1, keepdims=True)
        r = r_ref[:, vs]
        o_ref[:, vs] = (y * lax.rsqrt(ms + EPS) * nw_ref[...] * (r * jax.nn.sigmoid(r))).astype(o_ref.dtype)


def _gla_mixer(u, w_gate, b_gate, norm_w):
    t_len = u.shape[0]
    L = GLA_CHUNK
    row = lambda width, start: pl.BlockSpec((L, width), lambda c: (c, start // width))
    full = lambda a: pl.BlockSpec(a.shape, lambda c: (0, 0))
    wg = jnp.pad(w_gate, ((0, LANES - GLA_RANK), (0, 0)))
    bg = b_gate.reshape(1, GLA_DKT)
    nw = norm_w.reshape(1, GLA_DV)
    return pl.pallas_call(
        _gla_kernel,
        out_shape=jax.ShapeDtypeStruct((t_len, MIX_W), BF16),
        grid=(t_len // L,),
        in_specs=[row(GLA_DKT, EV_Q), row(GLA_DKT, EV_K), row(MIX_W, EV_V), row(MIX_W, EV_R), row(LANES, EV_GLR),
                  full(wg), full(bg), full(nw)],
        out_specs=pl.BlockSpec((L, MIX_W), lambda c: (c, 0)),
        scratch_shapes=[pltpu.VMEM((GLA_HEADS, GLA_DK, GLA_DV), F32)],
        compiler_params=pltpu.CompilerParams(dimension_semantics=("arbitrary",), vmem_limit_bytes=VMEM_LIMIT),
        name="gla_mixer",
    )(u, u, u, u, u, wg, bg, nw)


def _unit_lower_inverse(a):
    n = a.shape[0]
    eye = (lax.broadcasted_iota(jnp.int32, (n, n), 0) == lax.broadcasted_iota(jnp.int32, (n, n), 1)).astype(F32)
    p = eye - a
    x = a
    steps = int(math.log2(n)) - 1
    for i in range(steps):
        x16 = x.astype(BF16)
        x = jnp.dot(x16, x16, preferred_element_type=F32)
        p = p + jnp.dot(p.astype(BF16), x.astype(BF16), preferred_element_type=F32)
    return p


def _gdn_kernel(q_ref, k_ref, v_ref, z_ref, ab_ref, cwq_ref, cwk_ref, cwv_ref, cbq_ref, cbk_ref, cbv_ref,
                alog_ref, dtb_ref, nw_ref, o_ref, tail_ref, state_ref):
    L, dk, dv = GDN_CHUNK, GDN_DK, GDN_DV
    rep = GDN_V_HEADS // GDN_QK_HEADS

    @pl.when(pl.program_id(0) == 0)
    def _():
        tail_ref[...] = jnp.zeros(tail_ref.shape, F32)
        state_ref[...] = jnp.zeros(state_ref.shape, F32)

    qr, kr, vr = q_ref[...], k_ref[...], v_ref[...]
    q = _conv_silu(qr, tail_ref[:, :GDN_QKW], cwq_ref[...], cbq_ref[...])
    k = _conv_silu(kr, tail_ref[:, GDN_QKW:2 * GDN_QKW], cwk_ref[...], cbk_ref[...])
    v = _conv_silu(vr, tail_ref[:, 2 * GDN_QKW:], cwv_ref[...], cbv_ref[...])
    tail_ref[:, :GDN_QKW] = qr[L - CONV_TAIL:, :]
    tail_ref[:, GDN_QKW:2 * GDN_QKW] = kr[L - CONV_TAIL:, :]
    tail_ref[:, 2 * GDN_QKW:] = vr[L - CONV_TAIL:, :]

    ab = ab_ref[...]
    beta = jax.nn.sigmoid(ab)
    gc = _cumsum_rows(-jnp.exp(alog_ref[...]) * _softplus(ab + dtb_ref[...]))
    gc_t = jnp.transpose(gc)
    egc = jnp.exp(gc)
    last = gc[L - 1:L, :]
    e_end = jnp.exp(last - gc)
    e_last = jnp.exp(last)
    rows = lax.broadcasted_iota(jnp.int32, (L, L), 0)
    cols = lax.broadcasted_iota(jnp.int32, (L, L), 1)
    for j in range(GDN_QK_HEADS):
        qj = q[:, j * dk:(j + 1) * dk]
        kj = k[:, j * dk:(j + 1) * dk]
        qn = qj * (lax.rsqrt(jnp.sum(qj * qj, axis=-1, keepdims=True) + EPS) * dk ** -0.5)
        kn = kj * lax.rsqrt(jnp.sum(kj * kj, axis=-1, keepdims=True) + EPS)
        kn16 = kn.astype(BF16)
        kk = lax.dot_general(kn16, kn16, _NT, preferred_element_type=F32)
        qk = lax.dot_general(qn.astype(BF16), kn16, _NT, preferred_element_type=F32)
        for r in range(rep):
            h = j * rep + r
            hs = slice(h * dv, (h + 1) * dv)
            decay = jnp.where(rows >= cols, jnp.exp(gc[:, h:h + 1] - gc_t[h:h + 1, :]), 0.0)
            b_col = beta[:, GDN_V_HEADS + h:GDN_V_HEADS + h + 1]
            t_mat = _unit_lower_inverse(jnp.where(rows > cols, b_col * kk * decay, 0.0)).astype(BF16)
            rhs = jnp.concatenate([v[:, hs] * b_col, kn * (b_col * egc[:, h:h + 1])], axis=1).astype(BF16)
            uw = jnp.dot(t_mat, rhs, preferred_element_type=F32)
            state = state_ref[h]
            s16 = state.astype(BF16)
            v_new = uw[:, :dv] - jnp.dot(uw[:, dv:].astype(BF16), s16, preferred_element_type=F32)
            vn16 = v_new.astype(BF16)
            o = (jnp.dot((qn * egc[:, h:h + 1]).astype(BF16), s16, preferred_element_type=F32)
                 + jnp.dot((qk * decay).astype(BF16), vn16, preferred_element_type=F32))
            kd_t = jnp.transpose(kn * e_end[:, h:h + 1]).astype(BF16)
            state_ref[h] = state * e_last[:, h:h + 1] + jnp.dot(kd_t, vn16, preferred_element_type=F32)
            ms = jnp.mean(o * o, axis=-1, keepdims=True)
            zz = z_ref[:, hs]
            o_ref[:, hs] = (o * lax.rsqrt(ms + EPS) * nw_ref[...] * (zz * jax.nn.sigmoid(zz))).astype(o_ref.dtype)


def _gdn_mixer(ub, conv_w, conv_b, a_log, dt_bias, norm_w):
    t_len = ub.shape[0]
    L = GDN_CHUNK
    pad = lambda v: jnp.pad(v, (0, LANES - GDN_V_HEADS)).reshape(1, LANES)
    row = lambda width, start: pl.BlockSpec((L, width), lambda c: (c, start // width))
    fixed = lambda rows, width, start: pl.BlockSpec((rows, width), lambda c: (0, start // width))
    cb = conv_b.reshape(1, -1)
    return pl.pallas_call(
        _gdn_kernel,
        out_shape=jax.ShapeDtypeStruct((t_len, MIX_W), BF16),
        grid=(t_len // L,),
        in_specs=[
            row(GDN_QKW, OD_B_CONV), row(GDN_QKW, OD_B_CONV + GDN_QKW), row(MIX_W, OD_B_CONV + 2 * GDN_QKW),
            row(MIX_W, OD_B_Z), row(LANES, OD_B_AB),
            fixed(GDN_CONV, GDN_QKW, 0), fixed(GDN_CONV, GDN_QKW, GDN_QKW), fixed(GDN_CONV, MIX_W, 2 * GDN_QKW),
            fixed(1, GDN_QKW, 0), fixed(1, GDN_QKW, GDN_QKW), fixed(1, MIX_W, 2 * GDN_QKW),
            fixed(1, LANES, 0), fixed(1, LANES, 0), fixed(1, GDN_DV, 0),
        ],
        out_specs=pl.BlockSpec((L, MIX_W), lambda c: (c, 0)),
        scratch_shapes=[pltpu.VMEM((CONV_TAIL, GDN_CONV_CH), F32),
                        pltpu.VMEM((GDN_V_HEADS, GDN_DK, GDN_DV), F32)],
        compiler_params=pltpu.CompilerParams(dimension_semantics=("arbitrary",), vmem_limit_bytes=VMEM_LIMIT),
        name="gdn_mixer",
    )(ub, ub, ub, ub, ub, conv_w, conv_w, conv_w, cb, cb, cb, pad(a_log), pad(dt_bias),
      norm_w.reshape(1, GDN_DV))


def _even_layer(xs, ln_w, w_in, w_out, conv_w, conv_b, dt_bias, a_log, d_skip, ssd_norm_w, w_gate, b_gate,
                gla_norm_w):
    u = _norm_proj(xs, ln_w, _gather_cols(w_in, _EV), F32)
    ya = _ssd_mixer(u, conv_w, conv_b, dt_bias, a_log, d_skip, ssd_norm_w)
    yb = _gla_mixer(u, w_gate, b_gate, gla_norm_w)
    return _out_proj(xs, ya, yb, w_out)


def _odd_layer(xs, ln_w, w_in, w_out, cmp_pe, cmp_w1, cmp_w2, conv_w, conv_b, a_log, dt_bias, gdn_norm_w):
    q_scale = np.ones((OD_AT_W, 1), np.float32)
    q_scale[:MIX_W] = NSA_HEAD_DIM ** -0.5
    at = _norm_proj_t(xs, ln_w, _gather_cols(w_in, _OD_AT).T, jnp.asarray(q_scale), BF16)
    an = _norm_proj(xs, ln_w, _gather_cols(w_in, _OD_AN), BF16)
    ub = _norm_proj(xs, ln_w, _gather_cols(w_in, _OD_B), F32)
    yc = _nsa_mixer(at, an, ub, cmp_pe, cmp_w1, cmp_w2)
    yd = _gdn_mixer(ub, conv_w, conv_b, a_log, dt_bias, gdn_norm_w)
    return _out_proj(xs, yc, yd, w_out)


def kernel(x, ln_w, final_ln_w, ev_w_in, ev_w_out, ssd_conv_w, ssd_conv_b, ssd_dt_bias, ssd_a_log, ssd_d,
           ssd_norm_w, gla_w_gate, gla_b_gate, gla_norm_w, od_w_in, od_w_out, nsa_cmp_pe, nsa_cmp_w1,
           nsa_cmp_w2, gdn_conv_w, gdn_conv_b, gdn_a_log, gdn_dt_bias, gdn_norm_w):
    bsz, t_len, d = x.shape
    assert bsz == 1 and d == D_MODEL
    xs = x.reshape(t_len, d)
    for layer in range(DEPTH):
        i = layer // 2
        if layer % 2 == 0:
            xs = _even_layer(xs, ln_w[layer], ev_w_in[i], ev_w_out[i], ssd_conv_w[i], ssd_conv_b[i], ssd_dt_bias[i],
                             ssd_a_log[i], ssd_d[i], ssd_norm_w[i], gla_w_gate[i], gla_b_gate[i], gla_norm_w[i])
        else:
            xs = _odd_layer(xs, ln_w[layer], od_w_in[i], od_w_out[i], nsa_cmp_pe[i], nsa_cmp_w1[i], nsa_cmp_w2[i],
                            gdn_conv_w[i], gdn_conv_b[i], gdn_a_log[i], gdn_dt_bias[i], gdn_norm_w[i])
    return _final_norm(xs, final_ln_w).reshape(bsz, t_len, d)
```

```python
import functools
import math

import numpy as np
import jax
import jax.numpy as jnp
from jax import lax
from jax.experimental import pallas as pl
from jax.experimental.pallas import tpu as pltpu

D_MODEL = 2048
DEPTH = 4
EPS = 1e-6
NEG = -1e30
MIX_W = D_MODEL

SSD_HEAD_DIM = 64
SSD_HEADS = MIX_W // SSD_HEAD_DIM
SSD_STATE = 128
SSD_GROUPS = 8
SSD_CONV = 4
SSD_CHUNK = 128
SSD_GN = SSD_GROUPS * SSD_STATE
SSD_XBC = MIX_W + 2 * SSD_GN
SSD_IN = MIX_W + SSD_XBC + SSD_HEADS

GLA_HEADS = 4
GLA_DKT = MIX_W // 2
GLA_DK = GLA_DKT // GLA_HEADS
GLA_DV = MIX_W // GLA_HEADS
GLA_RANK = 16
GLA_TAU = 16.0
GLA_CHUNK = 128
GLA_IN = 2 * GLA_DKT + 2 * MIX_W + GLA_RANK

NSA_HEADS = 16
NSA_HEAD_DIM = MIX_W // NSA_HEADS
NSA_KV_GROUPS = 4
NSA_GH = NSA_HEADS // NSA_KV_GROUPS
NSA_KVW = NSA_KV_GROUPS * NSA_HEAD_DIM
NSA_CMP_BLOCK = 32
NSA_CMP_STRIDE = 16
NSA_SLC_BLOCK = 64
NSA_TOPK = 16
NSA_WINDOW = 512
NSA_FORCE_BONUS = 1e3
NSA_IN = 2 * MIX_W + 6 * NSA_KVW + 3 * NSA_HEADS

GDN_QK_HEADS = 8
GDN_V_HEADS = 16
GDN_DK = 128
GDN_DV = MIX_W // GDN_V_HEADS
GDN_CONV = 4
GDN_CHUNK = 64
GDN_QKW = GDN_QK_HEADS * GDN_DK
GDN_CONV_CH = 2 * GDN_QKW + MIX_W
GDN_IN = GDN_CONV_CH + 2 * GDN_V_HEADS + MIX_W

LANES = 128
VMEM_LIMIT = 56 * 1024 * 1024
M_INIT = 0.1 * NEG

BF16 = jnp.bfloat16
F32 = jnp.float32


def _round_up(n, m):
    return (n + m - 1) // m * m


def _width(pieces):
    return sum(w for _, w in pieces)


def _gather_cols(w, pieces):
    cols = [jnp.zeros((w.shape[0], wd), w.dtype) if s is None else w[:, s:s + wd] for s, wd in pieces]
    return jnp.concatenate(cols, axis=1).astype(BF16)


_KV0 = MIX_W
_GATES0 = MIX_W + 6 * NSA_KVW

_OD_AT = [(0, MIX_W), (_KV0 + 3 * NSA_KVW, NSA_KVW), (_KV0 + 5 * NSA_KVW, NSA_KVW)]
OD_AT_W = _width(_OD_AT)
OD_AT_VSLC = MIX_W // NSA_HEAD_DIM
OD_AT_VWIN = OD_AT_VSLC + NSA_KV_GROUPS
_OD_AN = [(_KV0 + 2 * NSA_KVW, NSA_KVW), (_KV0 + 4 * NSA_KVW, NSA_KVW)]
OD_AN_W = _width(_OD_AN)
OD_AN_KSLC, OD_AN_KWIN = 0, NSA_KV_GROUPS

_OD_B = [(NSA_IN, GDN_CONV_CH), (NSA_IN + GDN_CONV_CH + 2 * GDN_V_HEADS, MIX_W), (_GATES0 + 3 * NSA_HEADS, MIX_W),
         (_KV0, 2 * NSA_KVW)]
for _g in range(NSA_KV_GROUPS):
    _OD_B += [(_GATES0 + br * NSA_HEADS + _g * NSA_GH, NSA_GH) for br in range(3)] + [(None, LANES - 3 * NSA_GH)]
_OD_B += [(NSA_IN + GDN_CONV_CH, 2 * GDN_V_HEADS), (None, LANES - 2 * GDN_V_HEADS)]
OD_B_W = _width(_OD_B)
OD_B_CONV = 0
OD_B_Z = OD_B_CONV + GDN_CONV_CH
OD_B_GP = OD_B_Z + MIX_W
OD_B_KCMP = OD_B_GP + MIX_W
OD_B_VCMP = OD_B_KCMP + NSA_KVW
OD_B_GATES = OD_B_VCMP + NSA_KVW
OD_B_AB = OD_B_GATES + NSA_KV_GROUPS * LANES

_GLA_R0 = SSD_IN + 2 * GLA_DKT + MIX_W + GLA_RANK
_EV = [(0, MIX_W + SSD_XBC), (SSD_IN, 2 * GLA_DKT + MIX_W), (_GLA_R0, MIX_W),
       (MIX_W + SSD_XBC, SSD_HEADS), (None, LANES - SSD_HEADS),
       (_GLA_R0 - GLA_RANK, GLA_RANK), (None, LANES - GLA_RANK)]
EV_W = _width(_EV)
EV_Z, EV_X = 0, MIX_W
EV_B = EV_X + MIX_W
EV_C = EV_B + SSD_GN
EV_Q = EV_C + SSD_GN
EV_K = EV_Q + GLA_DKT
EV_V = EV_K + GLA_DKT
EV_R = EV_V + MIX_W
EV_DT = EV_R + MIX_W
EV_GLR = EV_DT + LANES


def _pick_tn(n, cap=2048):
    best = LANES
    for k in range(1, n // LANES + 1):
        if (n // LANES) % k == 0 and k * LANES <= cap:
            best = k * LANES
    return best


def _norm_rows(x_ref, g_ref, h_ref):
    x = x_ref[...]
    ms = jnp.mean(x * x, axis=-1, keepdims=True)
    h_ref[...] = (x * lax.rsqrt(ms + EPS) * g_ref[...]).astype(h_ref.dtype)


def _norm_proj_kernel(x_ref, g_ref, w_ref, o_ref, h_ref):
    pl.when(pl.program_id(1) == 0)(lambda: _norm_rows(x_ref, g_ref, h_ref))
    o_ref[...] = jnp.dot(h_ref[...], w_ref[...], preferred_element_type=F32).astype(o_ref.dtype)


def _norm_proj_t_kernel(x_ref, g_ref, wt_ref, rs_ref, o_ref, h_ref):
    pl.when(pl.program_id(1) == 0)(lambda: _norm_rows(x_ref, g_ref, h_ref))
    acc = lax.dot_general(wt_ref[...], h_ref[...], (((1,), (1,)), ((), ())), preferred_element_type=F32)
    o_ref[...] = (acc * rs_ref[...]).astype(o_ref.dtype)


def _norm_proj(x, g, w, out_dtype, tm=512):
    t_len, d = x.shape
    n = w.shape[1]
    tn = _pick_tn(n)
    return pl.pallas_call(
        _norm_proj_kernel,
        out_shape=jax.ShapeDtypeStruct((t_len, n), out_dtype),
        grid=(t_len // tm, n // tn),
        in_specs=[
            pl.BlockSpec((tm, d), lambda i, j: (i, 0)),
            pl.BlockSpec((1, d), lambda i, j: (0, 0)),
            pl.BlockSpec((d, tn), lambda i, j: (0, j)),
        ],
        out_specs=pl.BlockSpec((tm, tn), lambda i, j: (i, j)),
        scratch_shapes=[pltpu.VMEM((tm, d), BF16)],
        compiler_params=pltpu.CompilerParams(
            dimension_semantics=("parallel", "arbitrary"), vmem_limit_bytes=VMEM_LIMIT),
        name="norm_proj",
    )(x, g.reshape(1, d), w)


def _norm_proj_t(x, g, wt, rowscale, out_dtype, tm=512):
    t_len, d = x.shape
    n = wt.shape[0]
    tn = _pick_tn(n)
    return pl.pallas_call(
        _norm_proj_t_kernel,
        out_shape=jax.ShapeDtypeStruct((n, t_len), out_dtype),
        grid=(t_len // tm, n // tn),
        in_specs=[
            pl.BlockSpec((tm, d), lambda i, j: (i, 0)),
            pl.BlockSpec((1, d), lambda i, j: (0, 0)),
            pl.BlockSpec((tn, d), lambda i, j: (j, 0)),
            pl.BlockSpec((tn, 1), lambda i, j: (j, 0)),
        ],
        out_specs=pl.BlockSpec((tn, tm), lambda i, j: (j, i)),
        scratch_shapes=[pltpu.VMEM((tm, d), BF16)],
        compiler_params=pltpu.CompilerParams(
            dimension_semantics=("parallel", "arbitrary"), vmem_limit_bytes=VMEM_LIMIT),
        name="norm_proj_t",
    )(x, g.reshape(1, d), wt, rowscale)


def _out_proj_kernel(x_ref, ya_ref, yb_ref, wa_ref, wb_ref, o_ref):
    acc = jnp.dot(ya_ref[...].astype(BF16), wa_ref[...], preferred_element_type=F32)
    acc += jnp.dot(yb_ref[...].astype(BF16), wb_ref[...], preferred_element_type=F32)
    o_ref[...] = x_ref[...] + acc


def _out_proj(x, ya, yb, w, tm=256, tn=1024):
    t_len, d = x.shape
    k = ya.shape[1]
    wa = w[:k].astype(BF16)
    wb = w[k:].astype(BF16)
    return pl.pallas_call(
        _out_proj_kernel,
        out_shape=jax.ShapeDtypeStruct((t_len, d), F32),
        grid=(d // tn, t_len // tm),
        in_specs=[
            pl.BlockSpec((tm, tn), lambda j, i: (i, j)),
            pl.BlockSpec((tm, k), lambda j, i: (i, 0)),
            pl.BlockSpec((tm, k), lambda j, i: (i, 0)),
            pl.BlockSpec((k, tn), lambda j, i: (0, j)),
            pl.BlockSpec((k, tn), lambda j, i: (0, j)),
        ],
        out_specs=pl.BlockSpec((tm, tn), lambda j, i: (i, j)),
        compiler_params=pltpu.CompilerParams(
            dimension_semantics=("parallel", "parallel"), vmem_limit_bytes=VMEM_LIMIT),
        name="out_proj",
    )(x, ya, yb, wa, wb)


def _rmsnorm_kernel(x_ref, g_ref, o_ref):
    x = x_ref[...]
    ms = jnp.mean(x * x, axis=-1, keepdims=True)
    o_ref[...] = x * lax.rsqrt(ms + EPS) * g_ref[...]


def _final_norm(x, g, tm=512):
    t_len, d = x.shape
    return pl.pallas_call(
        _rmsnorm_kernel,
        out_shape=jax.ShapeDtypeStruct((t_len, d), F32),
        grid=(t_len // tm,),
        in_specs=[pl.BlockSpec((tm, d), lambda i: (i, 0)), pl.BlockSpec((1, d), lambda i: (0, 0))],
        out_specs=pl.BlockSpec((tm, d), lambda i: (i, 0)),
        compiler_params=pltpu.CompilerParams(dimension_semantics=("parallel",)),
        name="final_norm",
    )(x, g.reshape(1, d))


NSA_TQ = 256
NSA_SLC_UNROLL = 2

_NT = (((1,), (1,)), ((), ()))


def _nsa_compress_hidden(u_ref, pe_ref, w1_ref, nc):
    dh = NSA_HEAD_DIM
    acc_a = jnp.zeros((nc, dh), F32)
    acc_b = jnp.zeros((nc, dh), F32)
    for l in range(NSA_CMP_STRIDE):
        x = u_ref[pl.ds(l, nc, stride=NSA_CMP_STRIDE), :]
        xa = (x + pe_ref[l:l + 1, :]).astype(BF16)
        xb = (x + pe_ref[NSA_CMP_STRIDE + l:NSA_CMP_STRIDE + l + 1, :]).astype(BF16)
        acc_a += jnp.dot(xa, w1_ref[l * dh:(l + 1) * dh, :].astype(BF16), preferred_element_type=F32)
        acc_b += jnp.dot(xb, w1_ref[(NSA_CMP_STRIDE + l) * dh:(NSA_CMP_STRIDE + l + 1) * dh, :].astype(BF16),
                         preferred_element_type=F32)
    h = acc_a + pltpu.roll(acc_b, nc - 1, axis=0)
    return (h * jax.nn.sigmoid(h)).astype(BF16)


def _nsa_compress_kernel(uk_ref, uv_ref, pe_ref, w1_ref, w2k_ref, w2vt_ref, kc_ref, vct_ref):
    nc = kc_ref.shape[1]
    hk = _nsa_compress_hidden(uk_ref, pe_ref.at[0], w1_ref.at[0], nc)
    kc_ref[0] = jnp.dot(hk, w2k_ref[...].astype(BF16), preferred_element_type=F32).astype(kc_ref.dtype)
    hv = _nsa_compress_hidden(uv_ref, pe_ref.at[1], w1_ref.at[1], nc)
    vct_ref[0] = lax.dot_general(w2vt_ref[...].astype(BF16), hv, _NT,
                                 preferred_element_type=F32).astype(vct_ref.dtype)


def _nsa_compress(ub, pe, w1, w2):
    t_len = ub.shape[0]
    nc = t_len // NSA_CMP_STRIDE
    g_n, dh = NSA_KV_GROUPS, NSA_HEAD_DIM
    return pl.pallas_call(
        _nsa_compress_kernel,
        out_shape=(jax.ShapeDtypeStruct((g_n, nc, dh), BF16), jax.ShapeDtypeStruct((g_n, dh, nc), BF16)),
        grid=(g_n,),
        in_specs=[
            pl.BlockSpec((t_len, dh), lambda g: (0, OD_B_KCMP // dh + g)),
            pl.BlockSpec((t_len, dh), lambda g: (0, OD_B_VCMP // dh + g)),
            pl.BlockSpec((2, NSA_CMP_BLOCK, dh), lambda g: (0, 0, 0)),
            pl.BlockSpec((2, NSA_CMP_BLOCK * dh, dh), lambda g: (0, 0, 0)),
            pl.BlockSpec((dh, dh), lambda g: (0, 0)),
            pl.BlockSpec((dh, dh), lambda g: (0, 0)),
        ],
        out_specs=(pl.BlockSpec((1, nc, dh), lambda g: (g, 0, 0)), pl.BlockSpec((1, dh, nc), lambda g: (g, 0, 0))),
        compiler_params=pltpu.CompilerParams(dimension_semantics=("parallel",), vmem_limit_bytes=VMEM_LIMIT),
        name="nsa_compress",
    )(ub, ub, pe, w1, w2[0], w2[1].T)


def _nsa_cmp_kernel(qt_ref, kc_ref, vct_ref, ovt_ref, ocmp_ref, sel_ref):
    tq = qt_ref.shape[1]
    nc = kc_ref.shape[1]
    ns = ovt_ref.shape[0]
    dh = NSA_HEAD_DIM
    t = pl.program_id(1) * tq + lax.broadcasted_iota(jnp.int32, (1, tq), 1)
    n_idx = lax.broadcasted_iota(jnp.int32, (nc, 1), 0)
    valid = n_idx * NSA_CMP_STRIDE + (NSA_CMP_BLOCK - 1) <= t
    kc = kc_ref[0]
    vct = vct_ref[0]
    heads = range(NSA_GH)
    s = [jnp.where(valid, jnp.dot(kc, qt_ref[h * dh:(h + 1) * dh, :], preferred_element_type=F32), NEG)
         for h in heads]
    m = [jnp.max(s[h], axis=0, keepdims=True) for h in heads]
    p = [jnp.where(valid, jnp.exp2(s[h] - m[h]), 0.0) for h in heads]
    l = [jnp.sum(p[h], axis=0, keepdims=True) for h in heads]
    p = [p[h] * (1.0 / jnp.where(l[h] > 0.0, l[h], 1.0)) for h in heads]
    for h in heads:
        ocmp_ref[h * dh:(h + 1) * dh, :] = jnp.dot(vct, p[h].astype(BF16), preferred_element_type=F32)
    psum = sum(p[1:], p[0])
    hi = psum.astype(BF16)
    lo = (psum - hi.astype(F32)).astype(BF16)
    ovt = ovt_ref[...]
    imp = jnp.dot(ovt, hi, preferred_element_type=F32) + jnp.dot(ovt, lo, preferred_element_type=F32)
    j = lax.broadcasted_iota(jnp.int32, (ns, 1), 0)
    cur = lax.shift_right_logical(t, int(math.log2(NSA_SLC_BLOCK)))
    forced = (j == 0) | (j == cur) | (j == cur - 1)
    ok = j * NSA_SLC_BLOCK <= t
    imp = jnp.where(ok, imp + jnp.where(forced, NSA_FORCE_BONUS, 0.0), NEG)
    jf = j.astype(F32)

    def pick(_, carry):
        imp, sel = carry
        mx = jnp.max(imp, axis=0, keepdims=True)
        first = jnp.min(jnp.where(imp == mx, jf, float(ns)), axis=0, keepdims=True)
        hit = jf == first
        sel = jnp.where(hit & (mx > 0.5 * NEG), 1.0, sel)
        imp = jnp.where(hit, -jnp.inf, imp)
        return imp, sel

    _, sel = lax.fori_loop(0, min(NSA_TOPK, ns), pick, (imp, jnp.zeros((ns, tq), F32)))
    sel_ref[0] = sel.astype(sel_ref.dtype)


def _nsa_cmp(at, kc, vct, overlap_t):
    t_len = at.shape[1]
    tq = min(NSA_TQ, t_len)
    ns, nc = overlap_t.shape
    g_n, dh = NSA_KV_GROUPS, NSA_HEAD_DIM
    gw = NSA_GH * dh
    return pl.pallas_call(
        _nsa_cmp_kernel,
        out_shape=(jax.ShapeDtypeStruct((MIX_W, t_len), F32), jax.ShapeDtypeStruct((g_n, ns, t_len), BF16)),
        grid=(g_n, t_len // tq),
        in_specs=[
            pl.BlockSpec((gw, tq), lambda g, i: (g, i)),
            pl.BlockSpec((1, nc, dh), lambda g, i: (g, 0, 0)),
            pl.BlockSpec((1, dh, nc), lambda g, i: (g, 0, 0)),
            pl.BlockSpec((ns, nc), lambda g, i: (0, 0)),
        ],
        out_specs=(pl.BlockSpec((gw, tq), lambda g, i: (g, i)), pl.BlockSpec((1, ns, tq), lambda g, i: (g, 0, i))),
        compiler_params=pltpu.CompilerParams(
            dimension_semantics=("parallel", "parallel"), vmem_limit_bytes=VMEM_LIMIT),
        name="nsa_cmp",
    )(at, kc, vct, overlap_t)


def _flash_init(m_ref, l_ref, acc_ref):
    m_ref[...] = jnp.full(m_ref.shape, M_INIT, F32)
    l_ref[...] = jnp.zeros(l_ref.shape, F32)
    acc_ref[...] = jnp.zeros(acc_ref.shape, F32)


def _flash_tile(q_of, k_tile, vt_tile, mask, m_ref, l_ref, acc_ref):
    heads = range(NSA_GH)
    s = [jnp.dot(k_tile, q_of(h), preferred_element_type=F32) for h in heads]
    if mask is not None:
        s = [jnp.where(mask, s[h], NEG) for h in heads]
    m_old = [m_ref[h] for h in heads]
    m_new = [jnp.maximum(m_old[h], jnp.max(s[h], axis=0, keepdims=True)) for h in heads]
    alpha = [jnp.exp2(m_old[h] - m_new[h]) for h in heads]
    p = [jnp.exp2(s[h] - m_new[h]) for h in heads]
    for h in heads:
        l_ref[h] = alpha[h] * l_ref[h] + jnp.sum(p[h], axis=0, keepdims=True)
        m_ref[h] = m_new[h]
    pv = [jnp.dot(vt_tile, p[h].astype(BF16), preferred_element_type=F32) for h in heads]
    for h in heads:
        acc_ref[h] = alpha[h] * acc_ref[h] + pv[h]


def _nsa_slc_kernel(qt_ref, k_ref, vt_ref, sel_ref, eb_ref, o_ref, qa_ref, m_ref, l_ref, acc_ref):
    tq = qt_ref.shape[1]
    dh = NSA_HEAD_DIM
    i = pl.program_id(1)
    t = i * tq + lax.broadcasted_iota(jnp.int32, (1, tq), 1)
    c_idx = lax.broadcasted_iota(jnp.int32, (tq, 1), 0)
    bias = ((sel_ref[0].astype(F32) - 1.0) * -NEG).astype(BF16)
    for h in range(NSA_GH):
        qa_ref[h, :dh, :] = qt_ref[h * dh:(h + 1) * dh, :]
        qa_ref[h, dh:, :] = bias
    _flash_init(m_ref, l_ref, acc_ref)

    def tile(k0, nk, causal):
        k0 = pl.multiple_of(k0, tq)
        k_aug = jnp.concatenate([k_ref[pl.ds(k0, nk), :], eb_ref[pl.ds(k0, nk), :]], axis=1)
        mask = (c_idx + k0 <= t) if causal else None
        _flash_tile(lambda h: qa_ref[h], k_aug, vt_ref[:, pl.ds(k0, nk)], mask, m_ref, l_ref, acc_ref)

    def body(jt, carry):
        tile(jt * (NSA_SLC_UNROLL * tq), NSA_SLC_UNROLL * tq, False)
        return carry

    lax.fori_loop(0, i // NSA_SLC_UNROLL, body, 0)
    for r in range(NSA_SLC_UNROLL - 1):
        pl.when(i % NSA_SLC_UNROLL > r)(
            functools.partial(tile, (i // NSA_SLC_UNROLL * NSA_SLC_UNROLL + r) * tq, tq, False))
    tile(i * tq, tq, True)
    for h in range(NSA_GH):
        o_ref[h * dh:(h + 1) * dh, :] = acc_ref[h] * (1.0 / l_ref[h])


def _nsa_win_kernel(qt_ref, k_ref, vt_ref, ocmp_ref, oslc_ref, gates_ref, gp_ref, o_ref, m_ref, l_ref, acc_ref):
    tq = qt_ref.shape[1]
    dh = NSA_HEAD_DIM
    i = pl.program_id(1)
    t = i * tq + lax.broadcasted_iota(jnp.int32, (1, tq), 1)
    c_idx = lax.broadcasted_iota(jnp.int32, (tq, 1), 0)
    _flash_init(m_ref, l_ref, acc_ref)

    def tile(jt, mask_of):
        k0 = pl.multiple_of(jt * tq, tq)
        mask = None if mask_of is None else mask_of(c_idx + k0)
        _flash_tile(lambda h: qt_ref[h * dh:(h + 1) * dh, :], k_ref[pl.ds(k0, tq), :], vt_ref[:, pl.ds(k0, tq)],
                    mask, m_ref, l_ref, acc_ref)

    pl.when(i >= 2)(lambda: tile(i - 2, lambda key: key > t - NSA_WINDOW))
    pl.when(i >= 1)(lambda: tile(i - 1, None))
    tile(i, lambda key: key <= t)
    gates_t = jnp.transpose(jax.nn.sigmoid(gates_ref[...]))
    for h in range(NSA_GH):
        rows = slice(h * dh, (h + 1) * dh)
        o_t = (gates_t[h:h + 1, :] * ocmp_ref[rows, :]
               + gates_t[NSA_GH + h:NSA_GH + h + 1, :] * oslc_ref[rows, :]
               + gates_t[2 * NSA_GH + h:2 * NSA_GH + h + 1, :] * (acc_ref[h] * (1.0 / l_ref[h])))
        gp = gp_ref[:, rows]
        o_ref[:, rows] = (jnp.transpose(o_t) * (gp * jax.nn.sigmoid(gp))).astype(o_ref.dtype)


def _nsa_scratch(tq):
    return [pltpu.VMEM((NSA_GH, 1, tq), F32), pltpu.VMEM((NSA_GH, 1, tq), F32),
            pltpu.VMEM((NSA_GH, NSA_HEAD_DIM, tq), F32)]


def _nsa_slc(at, an, sel):
    t_len = at.shape[1]
    tq = NSA_TQ
    ns = sel.shape[1]
    g_n, dh = NSA_KV_GROUPS, NSA_HEAD_DIM
    gw = NSA_GH * dh
    block_of_key = jnp.asarray(np.arange(t_len)[:, None] // NSA_SLC_BLOCK == np.arange(ns)[None], dtype=BF16)
    return pl.pallas_call(
        _nsa_slc_kernel,
        out_shape=jax.ShapeDtypeStruct((MIX_W, t_len), F32),
        grid=(g_n, t_len // tq),
        in_specs=[
            pl.BlockSpec((gw, tq), lambda g, i: (g, i)),
            pl.BlockSpec((t_len, dh), lambda g, i: (0, OD_AN_KSLC + g)),
            pl.BlockSpec((dh, t_len), lambda g, i: (OD_AT_VSLC + g, 0)),
            pl.BlockSpec((1, ns, tq), lambda g, i: (g, 0, i)),
            pl.BlockSpec((t_len, ns), lambda g, i: (0, 0)),
        ],
        out_specs=pl.BlockSpec((gw, tq), lambda g, i: (g, i)),
        scratch_shapes=[pltpu.VMEM((NSA_GH, dh + ns, tq), BF16)] + _nsa_scratch(tq),
        compiler_params=pltpu.CompilerParams(
            dimension_semantics=("parallel", "parallel"), vmem_limit_bytes=VMEM_LIMIT),
        name="nsa_slc",
    )(at, an, at, sel, block_of_key)


def _nsa_win(at, an, ub, ocmp, oslc):
    t_len = at.shape[1]
    tq = NSA_TQ
    assert NSA_WINDOW == 2 * tq
    g_n, dh = NSA_KV_GROUPS, NSA_HEAD_DIM
    gw = NSA_GH * dh
    return pl.pallas_call(
        _nsa_win_kernel,
        out_shape=jax.ShapeDtypeStruct((t_len, MIX_W), BF16),
        grid=(g_n, t_len // tq),
        in_specs=[
            pl.BlockSpec((gw, tq), lambda g, i: (g, i)),
            pl.BlockSpec((t_len, dh), lambda g, i: (0, OD_AN_KWIN + g)),
            pl.BlockSpec((dh, t_len), lambda g, i: (OD_AT_VWIN + g, 0)),
            pl.BlockSpec((gw, tq), lambda g, i: (g, i)),
            pl.BlockSpec((gw, tq), lambda g, i: (g, i)),
            pl.BlockSpec((tq, LANES), lambda g, i: (i, OD_B_GATES // LANES + g)),
            pl.BlockSpec((tq, gw), lambda g, i: (i, OD_B_GP // gw + g)),
        ],
        out_specs=pl.BlockSpec((tq, gw), lambda g, i: (i, g)),
        scratch_shapes=_nsa_scratch(tq),
        compiler_params=pltpu.CompilerParams(
            dimension_semantics=("parallel", "parallel"), vmem_limit_bytes=VMEM_LIMIT),
        name="nsa_win",
    )(at, an, at, ocmp, oslc, ub, ub)


def _nsa_overlap_t(t_len):
    nc = t_len // NSA_CMP_STRIDE
    n_cmp = (t_len - NSA_CMP_BLOCK) // NSA_CMP_STRIDE + 1
    cs = np.arange(n_cmp) * NSA_CMP_STRIDE
    ss = np.arange(t_len // NSA_SLC_BLOCK) * NSA_SLC_BLOCK
    ov = np.clip(np.minimum(cs[:, None] + NSA_CMP_BLOCK, ss[None] + NSA_SLC_BLOCK)
                 - np.maximum(cs[:, None], ss[None]), 0, None) / NSA_CMP_BLOCK
    ov = np.concatenate([ov, np.zeros((nc - n_cmp, ov.shape[1]))], axis=0)
    return jnp.asarray(ov.T, dtype=BF16)


def _nsa_mixer(at, an, ub, cmp_pe, cmp_w1, cmp_w2):
    t_len = an.shape[0]
    kc, vct = _nsa_compress(ub, cmp_pe, cmp_w1, cmp_w2)
    ocmp, sel = _nsa_cmp(at, kc, vct, _nsa_overlap_t(t_len))
    oslc = _nsa_slc(at, an, sel)
    return _nsa_win(at, an, ub, ocmp, oslc)


CONV_TAIL = 8


def _cumsum_rows(x):
    n = x.shape[0]
    row = lax.broadcasted_iota(jnp.int32, x.shape, 0)
    s = 1
    while s < n:
        x = x + jnp.where(row >= s, pltpu.roll(x, s, axis=0), 0.0)
        s *= 2
    return x


def _split_bf16(v, parts):
    out = []
    for _ in range(parts - 1):
        p = v.astype(BF16)
        out.append(p)
        v = v - p.astype(F32)
    out.append(v.astype(BF16))
    return out


def _expand_lanes(v, onehot):
    return sum(jnp.dot(p, onehot, preferred_element_type=F32) for p in _split_bf16(v, 3))


def _conv_silu(x, tail, w, b):
    n = x.shape[0]
    ext = jnp.concatenate([tail, x], axis=0)
    y = b
    for k in range(SSD_CONV):
        off = CONV_TAIL - (SSD_CONV - 1) + k
        y = y + w[k:k + 1, :] * ext[off:off + n, :]
    return y * jax.nn.sigmoid(y)


def _softplus(x):
    return jnp.maximum(x, 0.0) + jnp.log1p(jnp.exp(-jnp.abs(x)))


_TN = (((0,), (0,)), ((), ()))


def _ssd_kernel(z_ref, x_ref, b_ref, c_ref, dt_ref, cwx_ref, cwb_ref, cwc_ref, cbx_ref, cbb_ref, cbc_ref,
                dtb_ref, alog_ref, dsk_ref, nw_ref, e_ref, o_ref, tail_ref, state_ref, y_ref):
    L = SSD_CHUNK
    n, gw = SSD_STATE, (SSD_HEADS // SSD_GROUPS) * SSD_HEAD_DIM

    @pl.when(pl.program_id(0) == 0)
    def _():
        tail_ref[...] = jnp.zeros(tail_ref.shape, F32)
        state_ref[...] = jnp.zeros(state_ref.shape, F32)

    xr, br, cr = x_ref[...], b_ref[...], c_ref[...]
    x = _conv_silu(xr, tail_ref[:, :MIX_W], cwx_ref[...], cbx_ref[...])
    bm = _conv_silu(br, tail_ref[:, MIX_W:MIX_W + SSD_GN], cwb_ref[...], cbb_ref[...])
    cm = _conv_silu(cr, tail_ref[:, MIX_W + SSD_GN:], cwc_ref[...], cbc_ref[...])
    tail_ref[:, :MIX_W] = xr[L - CONV_TAIL:, :]
    tail_ref[:, MIX_W:MIX_W + SSD_GN] = br[L - CONV_TAIL:, :]
    tail_ref[:, MIX_W + SSD_GN:] = cr[L - CONV_TAIL:, :]

    dt = _softplus(dt_ref[...] + dtb_ref[...])
    cum = _cumsum_rows(dt * -jnp.exp(alog_ref[...]))
    cum_t = jnp.transpose(cum)
    onehot = e_ref[...]
    cum_e = _expand_lanes(cum, onehot)
    xdt = x * _expand_lanes(dt, onehot)
    last_e = cum_e[L - 1:L, :]
    xdt16 = xdt.astype(BF16)
    xw16 = (xdt * jnp.exp(last_e - cum_e)).astype(BF16)
    ecum_e = jnp.exp(cum_e)
    causal = lax.broadcasted_iota(jnp.int32, (L, L), 0) >= lax.broadcasted_iota(jnp.int32, (L, L), 1)
    lane = lax.broadcasted_iota(jnp.int32, (1, gw), 1)
    for g in range(SSD_GROUPS):
        cols = slice(g * gw, (g + 1) * gw)
        bg = bm[:, g * n:(g + 1) * n]
        cg16 = cm[:, g * n:(g + 1) * n].astype(BF16)
        sc = lax.dot_general(cg16, bg.astype(BF16), _NT, preferred_element_type=F32)
        state = state_ref[g]
        yg = jnp.dot(cg16, state.astype(BF16), preferred_element_type=F32) * ecum_e[:, cols]
        xg = xdt16[:, cols]
        for r in range(SSD_HEADS // SSD_GROUPS):
            h = g * (SSD_HEADS // SSD_GROUPS) + r
            decay = jnp.where(causal, jnp.exp(cum[:, h:h + 1] - cum_t[h:h + 1, :]), 0.0)
            in_head = (lane >= r * SSD_HEAD_DIM) & (lane < (r + 1) * SSD_HEAD_DIM)
            yg += jnp.dot((sc * decay).astype(BF16), jnp.where(in_head, xg, jnp.zeros_like(xg)),
                          preferred_element_type=F32)
        y_ref[:, cols] = yg
        bg_t = jnp.transpose(bg).astype(BF16)
        state_ref[g] = state * jnp.exp(last_e[:, cols]) + jnp.dot(bg_t, xw16[:, cols], preferred_element_type=F32)

    y = y_ref[...] + x * dsk_ref[...]
    zz = z_ref[...]
    y = y * (zz * jax.nn.sigmoid(zz))
    ms = jnp.mean(y * y, axis=-1, keepdims=True)
    o_ref[...] = (y * lax.rsqrt(ms + EPS) * nw_ref[...]).astype(o_ref.dtype)


def _ssd_mixer(u, conv_w, conv_b, dt_bias, a_log, d_skip, norm_w):
    t_len = u.shape[0]
    L = SSD_CHUNK
    pad = lambda v: jnp.pad(v, (0, LANES - SSD_HEADS)).reshape(1, LANES)
    onehot = np.zeros((LANES, MIX_W), np.float32)
    onehot[np.arange(MIX_W) // SSD_HEAD_DIM, np.arange(MIX_W)] = 1.0
    row = lambda width, start: pl.BlockSpec((L, width), lambda c: (c, start // width))
    fixed = lambda rows, width, start: pl.BlockSpec((rows, width), lambda c: (0, start // width))
    return pl.pallas_call(
        _ssd_kernel,
        out_shape=jax.ShapeDtypeStruct((t_len, MIX_W), BF16),
        grid=(t_len // L,),
        in_specs=[
            row(MIX_W, EV_Z), row(MIX_W, EV_X), row(SSD_GN, EV_B), row(SSD_GN, EV_C), row(LANES, EV_DT),
            fixed(SSD_CONV, MIX_W, 0), fixed(SSD_CONV, SSD_GN, MIX_W), fixed(SSD_CONV, SSD_GN, MIX_W + SSD_GN),
            fixed(1, MIX_W, 0), fixed(1, SSD_GN, MIX_W), fixed(1, SSD_GN, MIX_W + SSD_GN),
            fixed(1, LANES, 0), fixed(1, LANES, 0), fixed(1, MIX_W, 0), fixed(1, MIX_W, 0),
            fixed(LANES, MIX_W, 0),
        ],
        out_specs=pl.BlockSpec((L, MIX_W), lambda c: (c, 0)),
        scratch_shapes=[pltpu.VMEM((CONV_TAIL, SSD_XBC), F32),
                        pltpu.VMEM((SSD_GROUPS, SSD_STATE, MIX_W // SSD_GROUPS), F32),
                        pltpu.VMEM((L, MIX_W), F32)],
        compiler_params=pltpu.CompilerParams(dimension_semantics=("arbitrary",), vmem_limit_bytes=VMEM_LIMIT),
        name="ssd_mixer",
    )(u, u, u, u, u, conv_w, conv_w, conv_w, conv_b.reshape(1, -1), conv_b.reshape(1, -1), conv_b.reshape(1, -1),
      pad(dt_bias), pad(a_log), jnp.repeat(d_skip, SSD_HEAD_DIM).reshape(1, MIX_W), norm_w.reshape(1, MIX_W),
      jnp.asarray(onehot, dtype=BF16))


def _gla_kernel(q_ref, k_ref, v_ref, r_ref, glr_ref, wg_ref, bg_ref, nw_ref, o_ref, state_ref):
    L, dk, dv = GLA_CHUNK, GLA_DK, GLA_DV

    @pl.when(pl.program_id(0) == 0)
    def _():
        state_ref[...] = jnp.zeros(state_ref.shape, F32)

    g_hi, g_lo = _split_bf16(glr_ref[...], 2)
    w_hi, w_lo = _split_bf16(wg_ref[...], 2)
    logits = (jnp.dot(g_hi, w_hi, preferred_element_type=F32) + jnp.dot(g_hi, w_lo, preferred_element_type=F32)
              + jnp.dot(g_lo, w_hi, preferred_element_type=F32) + bg_ref[...])
    cum = _cumsum_rows(-_softplus(-logits) * (1.0 / GLA_TAU))
    tot = cum[L - 1:L, :]
    half = 0.5 * tot
    q = q_ref[...] * dk ** -0.5
    k = k_ref[...]
    qe = (q * jnp.exp(cum - half)).astype(BF16)
    ke = (k * jnp.exp(half - cum)).astype(BF16)
    qd = (q * jnp.exp(cum)).astype(BF16)
    kd = k * jnp.exp(tot - cum)
    etot = jnp.exp(tot)
    causal = lax.broadcasted_iota(jnp.int32, (L, L), 0) >= lax.broadcasted_iota(jnp.int32, (L, L), 1)
    for h in range(GLA_HEADS):
        ks = slice(h * dk, (h + 1) * dk)
        vs = slice(h * dv, (h + 1) * dv)
        att = lax.dot_general(qe[:, ks], ke[:, ks], _NT, preferred_element_type=F32)
        att = jnp.where(causal, att, 0.0).astype(BF16)
        vh = v_ref[:, vs].astype(BF16)
        state = state_ref[h]
        y = jnp.dot(att, vh, preferred_element_type=F32) + jnp.dot(qd[:, ks], state.astype(BF16),
                                                                   preferred_element_type=F32)
        kd_t = jnp.transpose(kd[:, ks]).astype(BF16)
        decay_col = jnp.transpose(jnp.broadcast_to(etot[:, ks], (8, dk)))[:, 0:1]
        state_ref[h] = state * decay_col + jnp.dot(kd_t, vh, preferred_element_type=F32)
        ms = jnp.mean(y * y, axis=-1, keepdims=True)
        r = r_ref[:, vs]
        o_ref[:, vs] = (y * lax.rsqrt(ms + EPS) * nw_ref[...] * (r * jax.nn.sigmoid(r))).astype(o_ref.dtype)


def _gla_mixer(u, w_gate, b_gate, norm_w):
    t_len = u.shape[0]
    L = GLA_CHUNK
    row = lambda width, start: pl.BlockSpec((L, width), lambda c: (c, start // width))
    full = lambda a: pl.BlockSpec(a.shape, lambda c: (0, 0))
    wg = jnp.pad(w_gate, ((0, LANES - GLA_RANK), (0, 0)))
    bg = b_gate.reshape(1, GLA_DKT)
    nw = norm_w.reshape(1, GLA_DV)
    return pl.pallas_call(
        _gla_kernel,
        out_shape=jax.ShapeDtypeStruct((t_len, MIX_W), BF16),
        grid=(t_len // L,),
        in_specs=[row(GLA_DKT, EV_Q), row(GLA_DKT, EV_K), row(MIX_W, EV_V), row(MIX_W, EV_R), row(LANES, EV_GLR),
                  full(wg), full(bg), full(nw)],
        out_specs=pl.BlockSpec((L, MIX_W), lambda c: (c, 0)),
        scratch_shapes=[pltpu.VMEM((GLA_HEADS, GLA_DK, GLA_DV), F32)],
        compiler_params=pltpu.CompilerParams(dimension_semantics=("arbitrary",), vmem_limit_bytes=VMEM_LIMIT),
        name="gla_mixer",
    )(u, u, u, u, u, wg, bg, nw)


def _unit_lower_inverses(mats):
    n = mats[0].shape[0]
    eye = (lax.broadcasted_iota(jnp.int32, (n, n), 0) == lax.broadcasted_iota(jnp.int32, (n, n), 1)).astype(F32)
    ps = [eye - a for a in mats]
    xs = list(mats)
    for _ in range(int(math.log2(n)) - 1):
        x16 = [x.astype(BF16) for x in xs]
        xs = [jnp.dot(x, x, preferred_element_type=F32) for x in x16]
        ps = [p + jnp.dot(p.astype(BF16), x.astype(BF16), preferred_element_type=F32) for p, x in zip(ps, xs)]
    return ps


def _gdn_kernel(q_ref, k_ref, v_ref, z_ref, ab_ref, cwq_ref, cwk_ref, cwv_ref, cbq_ref, cbk_ref, cbv_ref,
                alog_ref, dtb_ref, nw_ref, o_ref, tail_ref, state_ref):
    L, dk, dv = GDN_CHUNK, GDN_DK, GDN_DV
    rep = GDN_V_HEADS // GDN_QK_HEADS

    @pl.when(pl.program_id(0) == 0)
    def _():
        tail_ref[...] = jnp.zeros(tail_ref.shape, F32)
        state_ref[...] = jnp.zeros(state_ref.shape, F32)

    qr, kr, vr = q_ref[...], k_ref[...], v_ref[...]
    q = _conv_silu(qr, tail_ref[:, :GDN_QKW], cwq_ref[...], cbq_ref[...])
    k = _conv_silu(kr, tail_ref[:, GDN_QKW:2 * GDN_QKW], cwk_ref[...], cbk_ref[...])
    v = _conv_silu(vr, tail_ref[:, 2 * GDN_QKW:], cwv_ref[...], cbv_ref[...])
    tail_ref[:, :GDN_QKW] = qr[L - CONV_TAIL:, :]
    tail_ref[:, GDN_QKW:2 * GDN_QKW] = kr[L - CONV_TAIL:, :]
    tail_ref[:, 2 * GDN_QKW:] = vr[L - CONV_TAIL:, :]

    ab = ab_ref[...]
    beta = jax.nn.sigmoid(ab)
    gc = _cumsum_rows(-jnp.exp(alog_ref[...]) * _softplus(ab + dtb_ref[...]))
    gc_t = jnp.transpose(gc)
    egc = jnp.exp(gc)
    last = gc[L - 1:L, :]
    e_end = jnp.exp(last - gc)
    e_last = jnp.exp(last)
    rows = lax.broadcasted_iota(jnp.int32, (L, L), 0)
    cols = lax.broadcasted_iota(jnp.int32, (L, L), 1)
    heads = range(GDN_V_HEADS)
    qn, kn, kk, qk = [], [], [], []
    for j in range(GDN_QK_HEADS):
        qj = q[:, j * dk:(j + 1) * dk]
        kj = k[:, j * dk:(j + 1) * dk]
        qn.append(qj * (lax.rsqrt(jnp.sum(qj * qj, axis=-1, keepdims=True) + EPS) * dk ** -0.5))
        kn.append(kj * lax.rsqrt(jnp.sum(kj * kj, axis=-1, keepdims=True) + EPS))
        kn16 = kn[j].astype(BF16)
        kk.append(lax.dot_general(kn16, kn16, _NT, preferred_element_type=F32))
        qk.append(lax.dot_general(qn[j].astype(BF16), kn16, _NT, preferred_element_type=F32))
    decay = [jnp.where(rows >= cols, jnp.exp(gc[:, h:h + 1] - gc_t[h:h + 1, :]), 0.0) for h in heads]
    b_col = [beta[:, GDN_V_HEADS + h:GDN_V_HEADS + h + 1] for h in heads]
    t_mat = _unit_lower_inverses([jnp.where(rows > cols, b_col[h] * kk[h // rep] * decay[h], 0.0) for h in heads])
    rhs = [jnp.concatenate([v[:, h * dv:(h + 1) * dv] * b_col[h], kn[h // rep] * (b_col[h] * egc[:, h:h + 1])],
                           axis=1).astype(BF16) for h in heads]
    uw = [jnp.dot(t_mat[h].astype(BF16), rhs[h], preferred_element_type=F32) for h in heads]
    state = [state_ref[h] for h in heads]
    s16 = [s.astype(BF16) for s in state]
    vn16 = [(uw[h][:, :dv] - jnp.dot(uw[h][:, dv:].astype(BF16), s16[h], preferred_element_type=F32)).astype(BF16)
            for h in heads]
    out = [jnp.dot((qn[h // rep] * egc[:, h:h + 1]).astype(BF16), s16[h], preferred_element_type=F32)
           + jnp.dot((qk[h // rep] * decay[h]).astype(BF16), vn16[h], preferred_element_type=F32) for h in heads]
    for h in heads:
        kd_t = jnp.transpose(kn[h // rep] * e_end[:, h:h + 1]).astype(BF16)
        state_ref[h] = state[h] * e_last[:, h:h + 1] + jnp.dot(kd_t, vn16[h], preferred_element_type=F32)
    for h in heads:
        o = out[h]
        ms = jnp.mean(o * o, axis=-1, keepdims=True)
        zz = z_ref[:, h * dv:(h + 1) * dv]
        o_ref[:, h * dv:(h + 1) * dv] = (o * lax.rsqrt(ms + EPS) * nw_ref[...]
                                         * (zz * jax.nn.sigmoid(zz))).astype(o_ref.dtype)


def _gdn_mixer(ub, conv_w, conv_b, a_log, dt_bias, norm_w):
    t_len = ub.shape[0]
    L = GDN_CHUNK
    pad = lambda v: jnp.pad(v, (0, LANES - GDN_V_HEADS)).reshape(1, LANES)
    row = lambda width, start: pl.BlockSpec((L, width), lambda c: (c, start // width))
    fixed = lambda rows, width, start: pl.BlockSpec((rows, width), lambda c: (0, start // width))
    cb = conv_b.reshape(1, -1)
    return pl.pallas_call(
        _gdn_kernel,
        out_shape=jax.ShapeDtypeStruct((t_len, MIX_W), BF16),
        grid=(t_len // L,),
        in_specs=[
            row(GDN_QKW, OD_B_CONV), row(GDN_QKW, OD_B_CONV + GDN_QKW), row(MIX_W, OD_B_CONV + 2 * GDN_QKW),
            row(MIX_W, OD_B_Z), row(LANES, OD_B_AB),
            fixed(GDN_CONV, GDN_QKW, 0), fixed(GDN_CONV, GDN_QKW, GDN_QKW), fixed(GDN_CONV, MIX_W, 2 * GDN_QKW),
            fixed(1, GDN_QKW, 0), fixed(1, GDN_QKW, GDN_QKW), fixed(1, MIX_W, 2 * GDN_QKW),
            fixed(1, LANES, 0), fixed(1, LANES, 0), fixed(1, GDN_DV, 0),
        ],
        out_specs=pl.BlockSpec((L, MIX_W), lambda c: (c, 0)),
        scratch_shapes=[pltpu.VMEM((CONV_TAIL, GDN_CONV_CH), F32),
                        pltpu.VMEM((GDN_V_HEADS, GDN_DK, GDN_DV), F32)],
        compiler_params=pltpu.CompilerParams(dimension_semantics=("arbitrary",), vmem_limit_bytes=VMEM_LIMIT),
        name="gdn_mixer",
    )(ub, ub, ub, ub, ub, conv_w, conv_w, conv_w, cb, cb, cb, pad(a_log), pad(dt_bias),
      norm_w.reshape(1, GDN_DV))


def _even_layer(xs, ln_w, w_in, w_out, conv_w, conv_b, dt_bias, a_log, d_skip, ssd_norm_w, w_gate, b_gate,
                gla_norm_w):
    u = _norm_proj(xs, ln_w, _gather_cols(w_in, _EV), F32)
    ya = _ssd_mixer(u, conv_w, conv_b, dt_bias, a_log, d_skip, ssd_norm_w)
    yb = _gla_mixer(u, w_gate, b_gate, gla_norm_w)
    return _out_proj(xs, ya, yb, w_out)


def _odd_layer(xs, ln_w, w_in, w_out, cmp_pe, cmp_w1, cmp_w2, conv_w, conv_b, a_log, dt_bias, gdn_norm_w):
    q_scale = np.ones((OD_AT_W, 1), np.float32)
    q_scale[:MIX_W] = NSA_HEAD_DIM ** -0.5 * math.log2(math.e)
    at = _norm_proj_t(xs, ln_w, _gather_cols(w_in, _OD_AT).T, jnp.asarray(q_scale), BF16)
    an = _norm_proj(xs, ln_w, _gather_cols(w_in, _OD_AN), BF16)
    ub = _norm_proj(xs, ln_w, _gather_cols(w_in, _OD_B), F32)
    yc = _nsa_mixer(at, an, ub, cmp_pe, cmp_w1, cmp_w2)
    yd = _gdn_mixer(ub, conv_w, conv_b, a_log, dt_bias, gdn_norm_w)
    return _out_proj(xs, yc, yd, w_out)


def kernel(x, ln_w, final_ln_w, ev_w_in, ev_w_out, ssd_conv_w, ssd_conv_b, ssd_dt_bias, ssd_a_log, ssd_d,
           ssd_norm_w, gla_w_gate, gla_b_gate, gla_norm_w, od_w_in, od_w_out, nsa_cmp_pe, nsa_cmp_w1,
           nsa_cmp_w2, gdn_conv_w, gdn_conv_b, gdn_a_log, gdn_dt_bias, gdn_norm_w):
    bsz, t_len, d = x.shape
    assert bsz == 1 and d == D_MODEL
    xs = x.reshape(t_len, d)
    for layer in range(DEPTH):
        i = layer // 2
        if layer % 2 == 0:
            xs = _even_layer(xs, ln_w[layer], ev_w_in[i], ev_w_out[i], ssd_conv_w[i], ssd_conv_b[i], ssd_dt_bias[i],
                             ssd_a_log[i], ssd_d[i], ssd_norm_w[i], gla_w_gate[i], gla_b_gate[i], gla_norm_w[i])
        else:
            xs = _odd_layer(xs, ln_w[layer], od_w_in[i], od_w_out[i], nsa_cmp_pe[i], nsa_cmp_w1[i], nsa_cmp_w2[i],
                            gdn_conv_w[i], gdn_conv_b[i], gdn_a_log[i], gdn_dt_bias[i], gdn_norm_w[i])
    return _final_norm(xs, final_ln_w).reshape(bsz, t_len, d)
```

```python
import functools
import math

import numpy as np
import jax
import jax.numpy as jnp
from jax import lax
from jax.experimental import pallas as pl
from jax.experimental.pallas import tpu as pltpu

D_MODEL = 2048
DEPTH = 4
EPS = 1e-6
NEG = -1e30
MIX_W = D_MODEL

SSD_HEAD_DIM = 64
SSD_HEADS = MIX_W // SSD_HEAD_DIM
SSD_STATE = 128
SSD_GROUPS = 8
SSD_CONV = 4
SSD_CHUNK = 128
SSD_GN = SSD_GROUPS * SSD_STATE
SSD_XBC = MIX_W + 2 * SSD_GN
SSD_IN = MIX_W + SSD_XBC + SSD_HEADS

GLA_HEADS = 4
GLA_DKT = MIX_W // 2
GLA_DK = GLA_DKT // GLA_HEADS
GLA_DV = MIX_W // GLA_HEADS
GLA_RANK = 16
GLA_TAU = 16.0
GLA_CHUNK = 128
GLA_IN = 2 * GLA_DKT + 2 * MIX_W + GLA_RANK

NSA_HEADS = 16
NSA_HEAD_DIM = MIX_W // NSA_HEADS
NSA_KV_GROUPS = 4
NSA_GH = NSA_HEADS // NSA_KV_GROUPS
NSA_KVW = NSA_KV_GROUPS * NSA_HEAD_DIM
NSA_CMP_BLOCK = 32
NSA_CMP_STRIDE = 16
NSA_SLC_BLOCK = 64
NSA_TOPK = 16
NSA_WINDOW = 512
NSA_FORCE_BONUS = 1e3
NSA_IN = 2 * MIX_W + 6 * NSA_KVW + 3 * NSA_HEADS

GDN_QK_HEADS = 8
GDN_V_HEADS = 16
GDN_DK = 128
GDN_DV = MIX_W // GDN_V_HEADS
GDN_CONV = 4
GDN_CHUNK = 64
GDN_QKW = GDN_QK_HEADS * GDN_DK
GDN_CONV_CH = 2 * GDN_QKW + MIX_W
GDN_IN = GDN_CONV_CH + 2 * GDN_V_HEADS + MIX_W

LANES = 128
VMEM_LIMIT = 56 * 1024 * 1024
M_INIT = 0.1 * NEG

BF16 = jnp.bfloat16
F32 = jnp.float32


def _round_up(n, m):
    return (n + m - 1) // m * m


def _width(pieces):
    return sum(w for _, w in pieces)


def _gather_cols(w, pieces):
    cols = [jnp.zeros((w.shape[0], wd), BF16) if s is None else w[:, s:s + wd].astype(BF16) for s, wd in pieces]
    return jnp.concatenate(cols, axis=1)


_KV0 = MIX_W
_GATES0 = MIX_W + 6 * NSA_KVW

_OD_AT = [(0, MIX_W), (_KV0 + 3 * NSA_KVW, NSA_KVW), (_KV0 + 5 * NSA_KVW, NSA_KVW)]
OD_AT_W = _width(_OD_AT)
OD_AT_VSLC = MIX_W // NSA_HEAD_DIM
OD_AT_VWIN = OD_AT_VSLC + NSA_KV_GROUPS
_OD_AN = [(_KV0 + 2 * NSA_KVW, NSA_KVW), (_KV0 + 4 * NSA_KVW, NSA_KVW)]
OD_AN_W = _width(_OD_AN)
OD_AN_KSLC, OD_AN_KWIN = 0, NSA_KV_GROUPS

_OD_B = [(NSA_IN, GDN_CONV_CH), (NSA_IN + GDN_CONV_CH + 2 * GDN_V_HEADS, MIX_W), (_GATES0 + 3 * NSA_HEADS, MIX_W),
         (_KV0, 2 * NSA_KVW)]
for _g in range(NSA_KV_GROUPS):
    _OD_B += [(_GATES0 + br * NSA_HEADS + _g * NSA_GH, NSA_GH) for br in range(3)] + [(None, LANES - 3 * NSA_GH)]
_OD_B += [(NSA_IN + GDN_CONV_CH, 2 * GDN_V_HEADS), (None, LANES - 2 * GDN_V_HEADS)]
OD_B_W = _width(_OD_B)
OD_B_CONV = 0
OD_B_Z = OD_B_CONV + GDN_CONV_CH
OD_B_GP = OD_B_Z + MIX_W
OD_B_KCMP = OD_B_GP + MIX_W
OD_B_VCMP = OD_B_KCMP + NSA_KVW
OD_B_GATES = OD_B_VCMP + NSA_KVW
OD_B_AB = OD_B_GATES + NSA_KV_GROUPS * LANES

_GLA_R0 = SSD_IN + 2 * GLA_DKT + MIX_W + GLA_RANK
_EV = [(0, MIX_W + SSD_XBC), (SSD_IN, 2 * GLA_DKT + MIX_W), (_GLA_R0, MIX_W),
       (MIX_W + SSD_XBC, SSD_HEADS), (None, LANES - SSD_HEADS),
       (_GLA_R0 - GLA_RANK, GLA_RANK), (None, LANES - GLA_RANK)]
EV_W = _width(_EV)
EV_Z, EV_X = 0, MIX_W
EV_B = EV_X + MIX_W
EV_C = EV_B + SSD_GN
EV_Q = EV_C + SSD_GN
EV_K = EV_Q + GLA_DKT
EV_V = EV_K + GLA_DKT
EV_R = EV_V + MIX_W
EV_DT = EV_R + MIX_W
EV_GLR = EV_DT + LANES


def _pick_tn(n, cap=2048):
    best = LANES
    for k in range(1, n // LANES + 1):
        if (n // LANES) % k == 0 and k * LANES <= cap:
            best = k * LANES
    return best


def _rmsnorm_kernel(x_ref, g_ref, o_ref):
    x = x_ref[...]
    ms = jnp.mean(x * x, axis=-1, keepdims=True)
    o_ref[...] = (x * lax.rsqrt(ms + EPS) * g_ref[...]).astype(o_ref.dtype)


def _rmsnorm(x, g, out_dtype, tm=512):
    t_len, d = x.shape
    return pl.pallas_call(
        _rmsnorm_kernel,
        out_shape=jax.ShapeDtypeStruct((t_len, d), out_dtype),
        grid=(t_len // tm,),
        in_specs=[pl.BlockSpec((tm, d), lambda i: (i, 0)), pl.BlockSpec((1, d), lambda i: (0, 0))],
        out_specs=pl.BlockSpec((tm, d), lambda i: (i, 0)),
        compiler_params=pltpu.CompilerParams(dimension_semantics=("parallel",), vmem_limit_bytes=VMEM_LIMIT),
        name="rmsnorm",
    )(x, g.reshape(1, d))


def _proj_kernel(h_ref, w_ref, o_ref):
    o_ref[...] = jnp.dot(h_ref[...], w_ref[...], preferred_element_type=F32).astype(o_ref.dtype)


def _proj_t_kernel(h_ref, wt_ref, rs_ref, o_ref):
    acc = lax.dot_general(wt_ref[...], h_ref[...], (((1,), (1,)), ((), ())), preferred_element_type=F32)
    o_ref[...] = (acc * rs_ref[...]).astype(o_ref.dtype)


PROJ_TM = 1024


def _proj(h, w, out_dtype):
    t_len, d = h.shape
    n = w.shape[1]
    tm, tn = min(PROJ_TM, t_len), _pick_tn(n)
    return pl.pallas_call(
        _proj_kernel,
        out_shape=jax.ShapeDtypeStruct((t_len, n), out_dtype),
        grid=(n // tn, t_len // tm),
        in_specs=[pl.BlockSpec((tm, d), lambda j, i: (i, 0)), pl.BlockSpec((d, tn), lambda j, i: (0, j))],
        out_specs=pl.BlockSpec((tm, tn), lambda j, i: (i, j)),
        compiler_params=pltpu.CompilerParams(
            dimension_semantics=("parallel", "parallel"), vmem_limit_bytes=VMEM_LIMIT),
        name="proj",
    )(h, w)


def _proj_t(h, wt, rowscale, out_dtype):
    t_len, d = h.shape
    n = wt.shape[0]
    tm, tn = min(PROJ_TM, t_len), _pick_tn(n)
    return pl.pallas_call(
        _proj_t_kernel,
        out_shape=jax.ShapeDtypeStruct((n, t_len), out_dtype),
        grid=(n // tn, t_len // tm),
        in_specs=[pl.BlockSpec((tm, d), lambda j, i: (i, 0)), pl.BlockSpec((tn, d), lambda j, i: (j, 0)),
                  pl.BlockSpec((tn, 1), lambda j, i: (j, 0))],
        out_specs=pl.BlockSpec((tn, tm), lambda j, i: (j, i)),
        compiler_params=pltpu.CompilerParams(
            dimension_semantics=("parallel", "parallel"), vmem_limit_bytes=VMEM_LIMIT),
        name="proj_t",
    )(h, wt, rowscale)


def _out_proj_kernel(x_ref, ya_ref, yb_ref, wa_ref, wb_ref, o_ref, wa16_ref, wb16_ref):
    @pl.when(pl.program_id(1) == 0)
    def _():
        wa16_ref[...] = wa_ref[...].astype(BF16)
        wb16_ref[...] = wb_ref[...].astype(BF16)

    acc = jnp.dot(ya_ref[...], wa16_ref[...], preferred_element_type=F32)
    acc += jnp.dot(yb_ref[...], wb16_ref[...], preferred_element_type=F32)
    o_ref[...] = x_ref[...] + acc


def _out_proj(x, ya, yb, w, tm=512, tn=512):
    t_len, d = x.shape
    k = ya.shape[1]
    return pl.pallas_call(
        _out_proj_kernel,
        out_shape=jax.ShapeDtypeStruct((t_len, d), F32),
        grid=(d // tn, t_len // tm),
        in_specs=[
            pl.BlockSpec((tm, tn), lambda j, i: (i, j)),
            pl.BlockSpec((tm, k), lambda j, i: (i, 0)),
            pl.BlockSpec((tm, k), lambda j, i: (i, 0)),
            pl.BlockSpec((k, tn), lambda j, i: (0, j)),
            pl.BlockSpec((k, tn), lambda j, i: (1, j)),
        ],
        out_specs=pl.BlockSpec((tm, tn), lambda j, i: (i, j)),
        scratch_shapes=[pltpu.VMEM((k, tn), BF16), pltpu.VMEM((k, tn), BF16)],
        compiler_params=pltpu.CompilerParams(
            dimension_semantics=("parallel", "arbitrary"), vmem_limit_bytes=VMEM_LIMIT),
        name="out_proj",
    )(x, ya, yb, w, w)


NSA_TQ = 256
_NT = (((1,), (1,)), ((), ()))


def _nsa_compress_hidden(u_ref, pe_ref, w1_ref, nc):
    dh = NSA_HEAD_DIM
    acc_a = jnp.zeros((nc, dh), F32)
    acc_b = jnp.zeros((nc, dh), F32)
    for l in range(NSA_CMP_STRIDE):
        x = u_ref[pl.ds(l, nc, stride=NSA_CMP_STRIDE), :]
        xa = (x + pe_ref[l:l + 1, :]).astype(BF16)
        xb = (x + pe_ref[NSA_CMP_STRIDE + l:NSA_CMP_STRIDE + l + 1, :]).astype(BF16)
        acc_a += jnp.dot(xa, w1_ref[l * dh:(l + 1) * dh, :].astype(BF16), preferred_element_type=F32)
        acc_b += jnp.dot(xb, w1_ref[(NSA_CMP_STRIDE + l) * dh:(NSA_CMP_STRIDE + l + 1) * dh, :].astype(BF16),
                         preferred_element_type=F32)
    h = acc_a + pltpu.roll(acc_b, nc - 1, axis=0)
    return (h * jax.nn.sigmoid(h)).astype(BF16)


def _nsa_compress_kernel(uk_ref, uv_ref, pe_ref, w1_ref, w2k_ref, w2vt_ref, kc_ref, vct_ref):
    nc = kc_ref.shape[1]
    hk = _nsa_compress_hidden(uk_ref, pe_ref.at[0], w1_ref.at[0], nc)
    kc_ref[0] = jnp.dot(hk, w2k_ref[...].astype(BF16), preferred_element_type=F32).astype(kc_ref.dtype)
    hv = _nsa_compress_hidden(uv_ref, pe_ref.at[1], w1_ref.at[1], nc)
    vct_ref[0] = lax.dot_general(w2vt_ref[...].astype(BF16), hv, _NT,
                                 preferred_element_type=F32).astype(vct_ref.dtype)


def _nsa_compress(ub, pe, w1, w2):
    t_len = ub.shape[0]
    nc = t_len // NSA_CMP_STRIDE
    g_n, dh = NSA_KV_GROUPS, NSA_HEAD_DIM
    return pl.pallas_call(
        _nsa_compress_kernel,
        out_shape=(jax.ShapeDtypeStruct((g_n, nc, dh), BF16), jax.ShapeDtypeStruct((g_n, dh, nc), BF16)),
        grid=(g_n,),
        in_specs=[
            pl.BlockSpec((t_len, dh), lambda g: (0, OD_B_KCMP // dh + g)),
            pl.BlockSpec((t_len, dh), lambda g: (0, OD_B_VCMP // dh + g)),
            pl.BlockSpec((2, NSA_CMP_BLOCK, dh), lambda g: (0, 0, 0)),
            pl.BlockSpec((2, NSA_CMP_BLOCK * dh, dh), lambda g: (0, 0, 0)),
            pl.BlockSpec((dh, dh), lambda g: (0, 0)),
            pl.BlockSpec((dh, dh), lambda g: (0, 0)),
        ],
        out_specs=(pl.BlockSpec((1, nc, dh), lambda g: (g, 0, 0)), pl.BlockSpec((1, dh, nc), lambda g: (g, 0, 0))),
        compiler_params=pltpu.CompilerParams(dimension_semantics=("parallel",), vmem_limit_bytes=VMEM_LIMIT),
        name="nsa_compress",
    )(ub, ub, pe, w1, w2[0], w2[1].T)


def _nsa_cmp_kernel(qt_ref, kc_ref, vct_ref, ovt_ref, ocmp_ref, sel_ref):
    tq = qt_ref.shape[1]
    nc = kc_ref.shape[1]
    ns = ovt_ref.shape[0]
    dh = NSA_HEAD_DIM
    t = pl.program_id(1) * tq + lax.broadcasted_iota(jnp.int32, (1, tq), 1)
    n_idx = lax.broadcasted_iota(jnp.int32, (nc, 1), 0)
    valid = n_idx * NSA_CMP_STRIDE + (NSA_CMP_BLOCK - 1) <= t
    kc = kc_ref[0]
    vct = vct_ref[0]
    heads = range(NSA_GH)
    s = [jnp.where(valid, jnp.dot(kc, qt_ref[h * dh:(h + 1) * dh, :], preferred_element_type=F32), NEG)
         for h in heads]
    m = [jnp.max(s[h], axis=0, keepdims=True) for h in heads]
    p = [jnp.where(valid, jnp.exp2(s[h] - m[h]), 0.0) for h in heads]
    l = [jnp.sum(p[h], axis=0, keepdims=True) for h in heads]
    p = [p[h] * (1.0 / jnp.where(l[h] > 0.0, l[h], 1.0)) for h in heads]
    for h in heads:
        ocmp_ref[h * dh:(h + 1) * dh, :] = jnp.dot(vct, p[h].astype(BF16), preferred_element_type=F32)
    psum = sum(p[1:], p[0])
    hi = psum.astype(BF16)
    lo = (psum - hi.astype(F32)).astype(BF16)
    ovt = ovt_ref[...]
    imp = jnp.dot(ovt, hi, preferred_element_type=F32) + jnp.dot(ovt, lo, preferred_element_type=F32)
    j = lax.broadcasted_iota(jnp.int32, (ns, 1), 0)
    cur = lax.shift_right_logical(t, int(math.log2(NSA_SLC_BLOCK)))
    forced = (j == 0) | (j == cur) | (j == cur - 1)
    ok = j * NSA_SLC_BLOCK <= t
    imp = jnp.where(ok, imp + jnp.where(forced, NSA_FORCE_BONUS, 0.0), NEG)
    jf = j.astype(F32)

    def pick(_, carry):
        imp, sel = carry
        mx = jnp.max(imp, axis=0, keepdims=True)
        first = jnp.min(jnp.where(imp == mx, jf, float(ns)), axis=0, keepdims=True)
        hit = jf == first
        sel = jnp.where(hit & (mx > 0.5 * NEG), 1.0, sel)
        imp = jnp.where(hit, -jnp.inf, imp)
        return imp, sel

    _, sel = lax.fori_loop(0, min(NSA_TOPK, ns), pick, (imp, jnp.zeros((ns, tq), F32)))
    sel_ref[0] = sel.astype(sel_ref.dtype)


def _nsa_cmp(at, kc, vct, overlap_t):
    t_len = at.shape[1]
    tq = min(NSA_TQ, t_len)
    ns, nc = overlap_t.shape
    g_n, dh = NSA_KV_GROUPS, NSA_HEAD_DIM
    gw = NSA_GH * dh
    return pl.pallas_call(
        _nsa_cmp_kernel,
        out_shape=(jax.ShapeDtypeStruct((MIX_W, t_len), F32), jax.ShapeDtypeStruct((g_n, ns, t_len), BF16)),
        grid=(g_n, t_len // tq),
        in_specs=[
            pl.BlockSpec((gw, tq), lambda g, i: (g, i)),
            pl.BlockSpec((1, nc, dh), lambda g, i: (g, 0, 0)),
            pl.BlockSpec((1, dh, nc), lambda g, i: (g, 0, 0)),
            pl.BlockSpec((ns, nc), lambda g, i: (0, 0)),
        ],
        out_specs=(pl.BlockSpec((gw, tq), lambda g, i: (g, i)), pl.BlockSpec((1, ns, tq), lambda g, i: (g, 0, i))),
        compiler_params=pltpu.CompilerParams(
            dimension_semantics=("parallel", "parallel"), vmem_limit_bytes=VMEM_LIMIT),
        name="nsa_cmp",
    )(at, kc, vct, overlap_t)


def _flash_init(m_ref, l_ref, acc_ref):
    m_ref[...] = jnp.full(m_ref.shape, M_INIT, F32)
    l_ref[...] = jnp.zeros(l_ref.shape, F32)
    acc_ref[...] = jnp.zeros(acc_ref.shape, F32)


def _flash_tile(q_of, k_tile, vt_tile, mask, m_ref, l_ref, acc_ref):
    _flash_consume([jnp.dot(k_tile, q_of(h), preferred_element_type=F32) for h in range(NSA_GH)], vt_tile, mask,
                   m_ref, l_ref, acc_ref)


def _flash_consume(s, vt_tile, mask, m_ref, l_ref, acc_ref):
    heads = range(NSA_GH)
    if mask is not None:
        s = [jnp.where(mask, s[h], NEG) for h in heads]
    m_old = [m_ref[h] for h in heads]
    m_new = [jnp.maximum(m_old[h], jnp.max(s[h], axis=0, keepdims=True)) for h in heads]
    alpha = [jnp.exp2(m_old[h] - m_new[h]) for h in heads]
    p = [jnp.exp2(s[h] - m_new[h]) for h in heads]
    for h in heads:
        l_ref[h] = alpha[h] * l_ref[h] + jnp.sum(p[h], axis=0, keepdims=True)
        m_ref[h] = m_new[h]
    pv = [jnp.dot(vt_tile, p[h].astype(BF16), preferred_element_type=F32) for h in heads]
    for h in heads:
        acc_ref[h] = alpha[h] * acc_ref[h] + pv[h]


def _nsa_slc_kernel(qt_ref, k_ref, vt_ref, sel_ref, eb_ref, o_ref, qa_ref, s0_ref, s1_ref, m_ref, l_ref, acc_ref):
    tq = qt_ref.shape[1]
    dh = NSA_HEAD_DIM
    i = pl.program_id(1)
    t = i * tq + lax.broadcasted_iota(jnp.int32, (1, tq), 1)
    c_idx = lax.broadcasted_iota(jnp.int32, (tq, 1), 0)
    bias = ((sel_ref[0].astype(F32) - 1.0) * -NEG).astype(BF16)
    for h in range(NSA_GH):
        qa_ref[h, :dh, :] = qt_ref[h * dh:(h + 1) * dh, :]
        qa_ref[h, dh:, :] = bias
    _flash_init(m_ref, l_ref, acc_ref)

    def scores(jt, s_ref):
        k0 = pl.multiple_of(jt * tq, tq)
        k_aug = jnp.concatenate([k_ref[pl.ds(k0, tq), :], eb_ref[pl.ds(k0, tq), :]], axis=1)
        for h in range(NSA_GH):
            s_ref[h] = jnp.dot(k_aug, qa_ref[h], preferred_element_type=F32)

    def consume(jt, s_ref, causal):
        k0 = pl.multiple_of(jt * tq, tq)
        mask = (c_idx + k0 <= t) if causal else None
        _flash_consume([s_ref[h] for h in range(NSA_GH)], vt_ref[:, pl.ds(k0, tq)], mask, m_ref, l_ref, acc_ref)

    def pair(p, carry):
        scores(2 * p + 1, s1_ref)
        consume(2 * p, s0_ref, False)
        scores(2 * p + 2, s0_ref)
        consume(2 * p + 1, s1_ref, False)
        return carry

    scores(0, s0_ref)
    lax.fori_loop(0, i // 2, pair, 0)

    @pl.when(i % 2 == 0)
    def _():
        consume(i, s0_ref, True)

    @pl.when(i % 2 == 1)
    def _():
        scores(i, s1_ref)
        consume(i - 1, s0_ref, False)
        consume(i, s1_ref, True)

    for h in range(NSA_GH):
        o_ref[h * dh:(h + 1) * dh, :] = acc_ref[h] * (1.0 / l_ref[h])


def _nsa_win_kernel(qt_ref, k_ref, vt_ref, ocmp_ref, oslc_ref, gates_ref, gp_ref, o_ref, m_ref, l_ref, acc_ref):
    tq = qt_ref.shape[1]
    dh = NSA_HEAD_DIM
    i = pl.program_id(1)
    t = i * tq + lax.broadcasted_iota(jnp.int32, (1, tq), 1)
    c_idx = lax.broadcasted_iota(jnp.int32, (tq, 1), 0)
    _flash_init(m_ref, l_ref, acc_ref)

    def tile(jt, mask_of):
        k0 = pl.multiple_of(jt * tq, tq)
        mask = None if mask_of is None else mask_of(c_idx + k0)
        _flash_tile(lambda h: qt_ref[h * dh:(h + 1) * dh, :], k_ref[pl.ds(k0, tq), :], vt_ref[:, pl.ds(k0, tq)],
                    mask, m_ref, l_ref, acc_ref)

    pl.when(i >= 2)(lambda: tile(i - 2, lambda key: key > t - NSA_WINDOW))
    pl.when(i >= 1)(lambda: tile(i - 1, None))
    tile(i, lambda key: key <= t)
    gates_t = jnp.transpose(jax.nn.sigmoid(gates_ref[...]))
    for h in range(NSA_GH):
        rows = slice(h * dh, (h + 1) * dh)
        o_t = (gates_t[h:h + 1, :] * ocmp_ref[rows, :]
               + gates_t[NSA_GH + h:NSA_GH + h + 1, :] * oslc_ref[rows, :]
               + gates_t[2 * NSA_GH + h:2 * NSA_GH + h + 1, :] * (acc_ref[h] * (1.0 / l_ref[h])))
        gp = gp_ref[:, rows]
        o_ref[:, rows] = (jnp.transpose(o_t) * (gp * jax.nn.sigmoid(gp))).astype(o_ref.dtype)


def _nsa_scratch(tq):
    return [pltpu.VMEM((NSA_GH, 1, tq), F32), pltpu.VMEM((NSA_GH, 1, tq), F32),
            pltpu.VMEM((NSA_GH, NSA_HEAD_DIM, tq), F32)]


def _nsa_slc(at, an, sel):
    t_len = at.shape[1]
    tq = NSA_TQ
    ns = sel.shape[1]
    g_n, dh = NSA_KV_GROUPS, NSA_HEAD_DIM
    gw = NSA_GH * dh
    block_of_key = jnp.asarray(np.arange(t_len)[:, None] // NSA_SLC_BLOCK == np.arange(ns)[None], dtype=BF16)
    return pl.pallas_call(
        _nsa_slc_kernel,
        out_shape=jax.ShapeDtypeStruct((MIX_W, t_len), F32),
        grid=(g_n, t_len // tq),
        in_specs=[
            pl.BlockSpec((gw, tq), lambda g, i: (g, i)),
            pl.BlockSpec((t_len, dh), lambda g, i: (0, OD_AN_KSLC + g)),
            pl.BlockSpec((dh, t_len), lambda g, i: (OD_AT_VSLC + g, 0)),
            pl.BlockSpec((1, ns, tq), lambda g, i: (g, 0, i)),
            pl.BlockSpec((t_len, ns), lambda g, i: (0, 0)),
        ],
        out_specs=pl.BlockSpec((gw, tq), lambda g, i: (g, i)),
        scratch_shapes=[pltpu.VMEM((NSA_GH, dh + ns, tq), BF16), pltpu.VMEM((NSA_GH, tq, tq), F32),
                        pltpu.VMEM((NSA_GH, tq, tq), F32)] + _nsa_scratch(tq),
        compiler_params=pltpu.CompilerParams(
            dimension_semantics=("parallel", "parallel"), vmem_limit_bytes=VMEM_LIMIT),
        name="nsa_slc",
    )(at, an, at, sel, block_of_key)


def _nsa_win(at, an, ub, ocmp, oslc):
    t_len = at.shape[1]
    tq = NSA_TQ
    assert NSA_WINDOW == 2 * tq
    g_n, dh = NSA_KV_GROUPS, NSA_HEAD_DIM
    gw = NSA_GH * dh
    return pl.pallas_call(
        _nsa_win_kernel,
        out_shape=jax.ShapeDtypeStruct((t_len, MIX_W), BF16),
        grid=(g_n, t_len // tq),
        in_specs=[
            pl.BlockSpec((gw, tq), lambda g, i: (g, i)),
            pl.BlockSpec((t_len, dh), lambda g, i: (0, OD_AN_KWIN + g)),
            pl.BlockSpec((dh, t_len), lambda g, i: (OD_AT_VWIN + g, 0)),
            pl.BlockSpec((gw, tq), lambda g, i: (g, i)),
            pl.BlockSpec((gw, tq), lambda g, i: (g, i)),
            pl.BlockSpec((tq, LANES), lambda g, i: (i, OD_B_GATES // LANES + g)),
            pl.BlockSpec((tq, gw), lambda g, i: (i, OD_B_GP // gw + g)),
        ],
        out_specs=pl.BlockSpec((tq, gw), lambda g, i: (i, g)),
        scratch_shapes=_nsa_scratch(tq),
        compiler_params=pltpu.CompilerParams(
            dimension_semantics=("parallel", "parallel"), vmem_limit_bytes=VMEM_LIMIT),
        name="nsa_win",
    )(at, an, at, ocmp, oslc, ub, ub)


def _nsa_overlap_t(t_len):
    nc = t_len // NSA_CMP_STRIDE
    n_cmp = (t_len - NSA_CMP_BLOCK) // NSA_CMP_STRIDE + 1
    cs = np.arange(n_cmp) * NSA_CMP_STRIDE
    ss = np.arange(t_len // NSA_SLC_BLOCK) * NSA_SLC_BLOCK
    ov = np.clip(np.minimum(cs[:, None] + NSA_CMP_BLOCK, ss[None] + NSA_SLC_BLOCK)
                 - np.maximum(cs[:, None], ss[None]), 0, None) / NSA_CMP_BLOCK
    ov = np.concatenate([ov, np.zeros((nc - n_cmp, ov.shape[1]))], axis=0)
    return jnp.asarray(ov.T, dtype=BF16)


def _nsa_mixer(at, an, ub, cmp_pe, cmp_w1, cmp_w2):
    t_len = an.shape[0]
    kc, vct = _nsa_compress(ub, cmp_pe, cmp_w1, cmp_w2)
    ocmp, sel = _nsa_cmp(at, kc, vct, _nsa_overlap_t(t_len))
    oslc = _nsa_slc(at, an, sel)
    return _nsa_win(at, an, ub, ocmp, oslc)


CONV_TAIL = 8


def _cumsum_rows(x):
    n = x.shape[0]
    row = lax.broadcasted_iota(jnp.int32, x.shape, 0)
    s = 1
    while s < n:
        x = x + jnp.where(row >= s, pltpu.roll(x, s, axis=0), 0.0)
        s *= 2
    return x


def _split_bf16(v, parts):
    out = []
    for _ in range(parts - 1):
        p = v.astype(BF16)
        out.append(p)
        v = v - p.astype(F32)
    out.append(v.astype(BF16))
    return out


def _expand_lanes(v, onehot):
    return sum(jnp.dot(p, onehot, preferred_element_type=F32) for p in _split_bf16(v, 3))


def _conv_silu(x, tail, w, b):
    n = x.shape[0]
    ext = jnp.concatenate([tail, x], axis=0)
    y = b
    for k in range(SSD_CONV):
        off = CONV_TAIL - (SSD_CONV - 1) + k
        y = y + w[k:k + 1, :] * ext[off:off + n, :]
    return y * jax.nn.sigmoid(y)


def _softplus(x):
    return jnp.maximum(x, 0.0) + jnp.log1p(jnp.exp(-jnp.abs(x)))


_TN = (((0,), (0,)), ((), ()))


def _ssd_kernel(z_ref, x_ref, b_ref, c_ref, dt_ref, cwx_ref, cwb_ref, cwc_ref, cbx_ref, cbb_ref, cbc_ref,
                dtb_ref, alog_ref, dsk_ref, nw_ref, e_ref, o_ref, tail_ref, state_ref, y_ref):
    L = SSD_CHUNK
    n, gw = SSD_STATE, (SSD_HEADS // SSD_GROUPS) * SSD_HEAD_DIM

    @pl.when(pl.program_id(0) == 0)
    def _():
        tail_ref[...] = jnp.zeros(tail_ref.shape, F32)
        state_ref[...] = jnp.zeros(state_ref.shape, F32)

    xr, br, cr = x_ref[...], b_ref[...], c_ref[...]
    x = _conv_silu(xr, tail_ref[:, :MIX_W], cwx_ref[...], cbx_ref[...])
    bm = _conv_silu(br, tail_ref[:, MIX_W:MIX_W + SSD_GN], cwb_ref[...], cbb_ref[...])
    cm = _conv_silu(cr, tail_ref[:, MIX_W + SSD_GN:], cwc_ref[...], cbc_ref[...])
    tail_ref[:, :MIX_W] = xr[L - CONV_TAIL:, :]
    tail_ref[:, MIX_W:MIX_W + SSD_GN] = br[L - CONV_TAIL:, :]
    tail_ref[:, MIX_W + SSD_GN:] = cr[L - CONV_TAIL:, :]

    dt = _softplus(dt_ref[...] + dtb_ref[...])
    cum = _cumsum_rows(dt * -jnp.exp(alog_ref[...]))
    cum_t = jnp.transpose(cum)
    onehot = e_ref[...]
    cum_e = _expand_lanes(cum, onehot)
    xdt = x * _expand_lanes(dt, onehot)
    last_e = cum_e[L - 1:L, :]
    xdt16 = xdt.astype(BF16)
    xw16 = (xdt * jnp.exp(last_e - cum_e)).astype(BF16)
    ecum_e = jnp.exp(cum_e)
    causal = lax.broadcasted_iota(jnp.int32, (L, L), 0) >= lax.broadcasted_iota(jnp.int32, (L, L), 1)
    lane = lax.broadcasted_iota(jnp.int32, (1, gw), 1)
    for g in range(SSD_GROUPS):
        cols = slice(g * gw, (g + 1) * gw)
        bg = bm[:, g * n:(g + 1) * n]
        cg16 = cm[:, g * n:(g + 1) * n].astype(BF16)
        sc = lax.dot_general(cg16, bg.astype(BF16), _NT, preferred_element_type=F32)
        state = state_ref[g]
        yg = jnp.dot(cg16, state.astype(BF16), preferred_element_type=F32) * ecum_e[:, cols]
        xg = xdt16[:, cols]
        for r in range(SSD_HEADS // SSD_GROUPS):
            h = g * (SSD_HEADS // SSD_GROUPS) + r
            decay = jnp.where(causal, jnp.exp(cum[:, h:h + 1] - cum_t[h:h + 1, :]), 0.0)
            in_head = (lane >= r * SSD_HEAD_DIM) & (lane < (r + 1) * SSD_HEAD_DIM)
            yg += jnp.dot((sc * decay).astype(BF16), jnp.where(in_head, xg, jnp.zeros_like(xg)),
                          preferred_element_type=F32)
        y_ref[:, cols] = yg
        bg_t = jnp.transpose(bg).astype(BF16)
        state_ref[g] = state * jnp.exp(last_e[:, cols]) + jnp.dot(bg_t, xw16[:, cols], preferred_element_type=F32)

    y = y_ref[...] + x * dsk_ref[...]
    zz = z_ref[...]
    y = y * (zz * jax.nn.sigmoid(zz))
    ms = jnp.mean(y * y, axis=-1, keepdims=True)
    o_ref[...] = (y * lax.rsqrt(ms + EPS) * nw_ref[...]).astype(o_ref.dtype)


def _ssd_mixer(u, conv_w, conv_b, dt_bias, a_log, d_skip, norm_w):
    t_len = u.shape[0]
    L = SSD_CHUNK
    pad = lambda v: jnp.pad(v, (0, LANES - SSD_HEADS)).reshape(1, LANES)
    onehot = np.zeros((LANES, MIX_W), np.float32)
    onehot[np.arange(MIX_W) // SSD_HEAD_DIM, np.arange(MIX_W)] = 1.0
    row = lambda width, start: pl.BlockSpec((L, width), lambda c: (c, start // width))
    fixed = lambda rows, width, start: pl.BlockSpec((rows, width), lambda c: (0, start // width))
    return pl.pallas_call(
        _ssd_kernel,
        out_shape=jax.ShapeDtypeStruct((t_len, MIX_W), BF16),
        grid=(t_len // L,),
        in_specs=[
            row(MIX_W, EV_Z), row(MIX_W, EV_X), row(SSD_GN, EV_B), row(SSD_GN, EV_C), row(LANES, EV_DT),
            fixed(SSD_CONV, MIX_W, 0), fixed(SSD_CONV, SSD_GN, MIX_W), fixed(SSD_CONV, SSD_GN, MIX_W + SSD_GN),
            fixed(1, MIX_W, 0), fixed(1, SSD_GN, MIX_W), fixed(1, SSD_GN, MIX_W + SSD_GN),
            fixed(1, LANES, 0), fixed(1, LANES, 0), fixed(1, MIX_W, 0), fixed(1, MIX_W, 0),
            fixed(LANES, MIX_W, 0),
        ],
        out_specs=pl.BlockSpec((L, MIX_W), lambda c: (c, 0)),
        scratch_shapes=[pltpu.VMEM((CONV_TAIL, SSD_XBC), F32),
                        pltpu.VMEM((SSD_GROUPS, SSD_STATE, MIX_W // SSD_GROUPS), F32),
                        pltpu.VMEM((L, MIX_W), F32)],
        compiler_params=pltpu.CompilerParams(dimension_semantics=("arbitrary",), vmem_limit_bytes=VMEM_LIMIT),
        name="ssd_mixer",
    )(u, u, u, u, u, conv_w, conv_w, conv_w, conv_b.reshape(1, -1), conv_b.reshape(1, -1), conv_b.reshape(1, -1),
      pad(dt_bias), pad(a_log), jnp.repeat(d_skip, SSD_HEAD_DIM).reshape(1, MIX_W), norm_w.reshape(1, MIX_W),
      jnp.asarray(onehot, dtype=BF16))


def _gla_kernel(q_ref, k_ref, v_ref, r_ref, glr_ref, wg_ref, bg_ref, nw_ref, o_ref, state_ref):
    L, dk, dv = GLA_CHUNK, GLA_DK, GLA_DV

    @pl.when(pl.program_id(0) == 0)
    def _():
        state_ref[...] = jnp.zeros(state_ref.shape, F32)

    g_hi, g_lo = _split_bf16(glr_ref[...], 2)
    w_hi, w_lo = _split_bf16(wg_ref[...], 2)
    logits = (jnp.dot(g_hi, w_hi, preferred_element_type=F32) + jnp.dot(g_hi, w_lo, preferred_element_type=F32)
              + jnp.dot(g_lo, w_hi, preferred_element_type=F32) + bg_ref[...])
    cum = _cumsum_rows(-_softplus(-logits) * (1.0 / GLA_TAU))
    tot = cum[L - 1:L, :]
    half = 0.5 * tot
    q = q_ref[...] * dk ** -0.5
    k = k_ref[...]
    qe = (q * jnp.exp(cum - half)).astype(BF16)
    ke = (k * jnp.exp(half - cum)).astype(BF16)
    qd = (q * jnp.exp(cum)).astype(BF16)
    kd = k * jnp.exp(tot - cum)
    etot = jnp.exp(tot)
    causal = lax.broadcasted_iota(jnp.int32, (L, L), 0) >= lax.broadcasted_iota(jnp.int32, (L, L), 1)
    for h in range(GLA_HEADS):
        ks = slice(h * dk, (h + 1) * dk)
        vs = slice(h * dv, (h + 1) * dv)
        att = lax.dot_general(qe[:, ks], ke[:, ks], _NT, preferred_element_type=F32)
        att = jnp.where(causal, att, 0.0).astype(BF16)
        vh = v_ref[:, vs].astype(BF16)
        state = state_ref[h]
        y = jnp.dot(att, vh, preferred_element_type=F32) + jnp.dot(qd[:, ks], state.astype(BF16),
                                                                   preferred_element_type=F32)
        kd_t = jnp.transpose(kd[:, ks]).astype(BF16)
        decay_col = jnp.transpose(jnp.broadcast_to(etot[:, ks], (8, dk)))[:, 0:1]
        state_ref[h] = state * decay_col + jnp.dot(kd_t, vh, preferred_element_type=F32)
        ms = jnp.mean(y * y, axis=-1, keepdims=True)
        r = r_ref[:, vs]
        o_ref[:, vs] = (y * lax.rsqrt(ms + EPS) * nw_ref[...] * (r * jax.nn.sigmoid(r))).astype(o_ref.dtype)


def _gla_mixer(u, w_gate, b_gate, norm_w):
    t_len = u.shape[0]
    L = GLA_CHUNK
    row = lambda width, start: pl.BlockSpec((L, width), lambda c: (c, start // width))
    full = lambda a: pl.BlockSpec(a.shape, lambda c: (0, 0))
    wg = jnp.pad(w_gate, ((0, LANES - GLA_RANK), (0, 0)))
    bg = b_gate.reshape(1, GLA_DKT)
    nw = norm_w.reshape(1, GLA_DV)
    return pl.pallas_call(
        _gla_kernel,
        out_shape=jax.ShapeDtypeStruct((t_len, MIX_W), BF16),
        grid=(t_len // L,),
        in_specs=[row(GLA_DKT, EV_Q), row(GLA_DKT, EV_K), row(MIX_W, EV_V), row(MIX_W, EV_R), row(LANES, EV_GLR),
                  full(wg), full(bg), full(nw)],
        out_specs=pl.BlockSpec((L, MIX_W), lambda c: (c, 0)),
        scratch_shapes=[pltpu.VMEM((GLA_HEADS, GLA_DK, GLA_DV), F32)],
        compiler_params=pltpu.CompilerParams(dimension_semantics=("arbitrary",), vmem_limit_bytes=VMEM_LIMIT),
        name="gla_mixer",
    )(u, u, u, u, u, wg, bg, nw)


def _unit_lower_inverses(mats):
    n = mats[0].shape[0]
    eye = (lax.broadcasted_iota(jnp.int32, (n, n), 0) == lax.broadcasted_iota(jnp.int32, (n, n), 1)).astype(F32)
    ps = [eye - a for a in mats]
    xs = list(mats)
    for _ in range(int(math.log2(n)) - 1):
        x16 = [x.astype(BF16) for x in xs]
        xs = [jnp.dot(x, x, preferred_element_type=F32) for x in x16]
        ps = [p + jnp.dot(p.astype(BF16), x.astype(BF16), preferred_element_type=F32) for p, x in zip(ps, xs)]
    return ps


def _gdn_kernel(q_ref, k_ref, v_ref, z_ref, ab_ref, cwq_ref, cwk_ref, cwv_ref, cbq_ref, cbk_ref, cbv_ref,
                alog_ref, dtb_ref, nw_ref, o_ref, tail_ref, state_ref):
    L, dk, dv = GDN_CHUNK, GDN_DK, GDN_DV
    rep = GDN_V_HEADS // GDN_QK_HEADS

    @pl.when(pl.program_id(0) == 0)
    def _():
        tail_ref[...] = jnp.zeros(tail_ref.shape, F32)
        state_ref[...] = jnp.zeros(state_ref.shape, F32)

    qr, kr, vr = q_ref[...], k_ref[...], v_ref[...]
    q = _conv_silu(qr, tail_ref[:, :GDN_QKW], cwq_ref[...], cbq_ref[...])
    k = _conv_silu(kr, tail_ref[:, GDN_QKW:2 * GDN_QKW], cwk_ref[...], cbk_ref[...])
    v = _conv_silu(vr, tail_ref[:, 2 * GDN_QKW:], cwv_ref[...], cbv_ref[...])
    tail_ref[:, :GDN_QKW] = qr[L - CONV_TAIL:, :]
    tail_ref[:, GDN_QKW:2 * GDN_QKW] = kr[L - CONV_TAIL:, :]
    tail_ref[:, 2 * GDN_QKW:] = vr[L - CONV_TAIL:, :]

    ab = ab_ref[...]
    beta = jax.nn.sigmoid(ab)
    gc = _cumsum_rows(-jnp.exp(alog_ref[...]) * _softplus(ab + dtb_ref[...]))
    gc_t = jnp.transpose(gc)
    egc = jnp.exp(gc)
    last = gc[L - 1:L, :]
    e_end = jnp.exp(last - gc)
    e_last = jnp.exp(last)
    rows = lax.broadcasted_iota(jnp.int32, (L, L), 0)
    cols = lax.broadcasted_iota(jnp.int32, (L, L), 1)
    heads = range(GDN_V_HEADS)
    qn, kn, kk, qk = [], [], [], []
    for j in range(GDN_QK_HEADS):
        qj = q[:, j * dk:(j + 1) * dk]
        kj = k[:, j * dk:(j + 1) * dk]
        qn.append(qj * (lax.rsqrt(jnp.sum(qj * qj, axis=-1, keepdims=True) + EPS) * dk ** -0.5))
        kn.append(kj * lax.rsqrt(jnp.sum(kj * kj, axis=-1, keepdims=True) + EPS))
        kn16 = kn[j].astype(BF16)
        kk.append(lax.dot_general(kn16, kn16, _NT, preferred_element_type=F32))
        qk.append(lax.dot_general(qn[j].astype(BF16), kn16, _NT, preferred_element_type=F32))
    decay = [jnp.where(rows >= cols, jnp.exp(gc[:, h:h + 1] - gc_t[h:h + 1, :]), 0.0) for h in heads]
    b_col = [beta[:, GDN_V_HEADS + h:GDN_V_HEADS + h + 1] for h in heads]
    t_mat = _unit_lower_inverses([jnp.where(rows > cols, b_col[h] * kk[h // rep] * decay[h], 0.0) for h in heads])
    rhs = [jnp.concatenate([v[:, h * dv:(h + 1) * dv] * b_col[h], kn[h // rep] * (b_col[h] * egc[:, h:h + 1])],
                           axis=1).astype(BF16) for h in heads]
    uw = [jnp.dot(t_mat[h].astype(BF16), rhs[h], preferred_element_type=F32) for h in heads]
    state = [state_ref[h] for h in heads]
    s16 = [s.astype(BF16) for s in state]
    vn16 = [(uw[h][:, :dv] - jnp.dot(uw[h][:, dv:].astype(BF16), s16[h], preferred_element_type=F32)).astype(BF16)
            for h in heads]
    out = [jnp.dot((qn[h // rep] * egc[:, h:h + 1]).astype(BF16), s16[h], preferred_element_type=F32)
           + jnp.dot((qk[h // rep] * decay[h]).astype(BF16), vn16[h], preferred_element_type=F32) for h in heads]
    for h in heads:
        kd_t = jnp.transpose(kn[h // rep] * e_end[:, h:h + 1]).astype(BF16)
        state_ref[h] = state[h] * e_last[:, h:h + 1] + jnp.dot(kd_t, vn16[h], preferred_element_type=F32)
    for h in heads:
        o = out[h]
        ms = jnp.mean(o * o, axis=-1, keepdims=True)
        zz = z_ref[:, h * dv:(h + 1) * dv]
        o_ref[:, h * dv:(h + 1) * dv] = (o * lax.rsqrt(ms + EPS) * nw_ref[...]
                                         * (zz * jax.nn.sigmoid(zz))).astype(o_ref.dtype)


def _gdn_mixer(ub, conv_w, conv_b, a_log, dt_bias, norm_w):
    t_len = ub.shape[0]
    L = GDN_CHUNK
    pad = lambda v: jnp.pad(v, (0, LANES - GDN_V_HEADS)).reshape(1, LANES)
    row = lambda width, start: pl.BlockSpec((L, width), lambda c: (c, start // width))
    fixed = lambda rows, width, start: pl.BlockSpec((rows, width), lambda c: (0, start // width))
    cb = conv_b.reshape(1, -1)
    return pl.pallas_call(
        _gdn_kernel,
        out_shape=jax.ShapeDtypeStruct((t_len, MIX_W), BF16),
        grid=(t_len // L,),
        in_specs=[
            row(GDN_QKW, OD_B_CONV), row(GDN_QKW, OD_B_CONV + GDN_QKW), row(MIX_W, OD_B_CONV + 2 * GDN_QKW),
            row(MIX_W, OD_B_Z), row(LANES, OD_B_AB),
            fixed(GDN_CONV, GDN_QKW, 0), fixed(GDN_CONV, GDN_QKW, GDN_QKW), fixed(GDN_CONV, MIX_W, 2 * GDN_QKW),
            fixed(1, GDN_QKW, 0), fixed(1, GDN_QKW, GDN_QKW), fixed(1, MIX_W, 2 * GDN_QKW),
            fixed(1, LANES, 0), fixed(1, LANES, 0), fixed(1, GDN_DV, 0),
        ],
        out_specs=pl.BlockSpec((L, MIX_W), lambda c: (c, 0)),
        scratch_shapes=[pltpu.VMEM((CONV_TAIL, GDN_CONV_CH), F32),
                        pltpu.VMEM((GDN_V_HEADS, GDN_DK, GDN_DV), F32)],
        compiler_params=pltpu.CompilerParams(dimension_semantics=("arbitrary",), vmem_limit_bytes=VMEM_LIMIT),
        name="gdn_mixer",
    )(ub, ub, ub, ub, ub, conv_w, conv_w, conv_w, cb, cb, cb, pad(a_log), pad(dt_bias),
      norm_w.reshape(1, GDN_DV))


def _even_layer(xs, ln_w, w_in, w_out, conv_w, conv_b, dt_bias, a_log, d_skip, ssd_norm_w, w_gate, b_gate,
                gla_norm_w):
    u = _proj(_rmsnorm(xs, ln_w, BF16), _gather_cols(w_in, _EV), F32)
    ya = _ssd_mixer(u, conv_w, conv_b, dt_bias, a_log, d_skip, ssd_norm_w)
    yb = _gla_mixer(u, w_gate, b_gate, gla_norm_w)
    return _out_proj(xs, ya, yb, w_out)


def _odd_layer(xs, ln_w, w_in, w_out, cmp_pe, cmp_w1, cmp_w2, conv_w, conv_b, a_log, dt_bias, gdn_norm_w):
    q_scale = np.ones((OD_AT_W, 1), np.float32)
    q_scale[:MIX_W] = NSA_HEAD_DIM ** -0.5 * math.log2(math.e)
    h = _rmsnorm(xs, ln_w, BF16)
    at = _proj_t(h, _gather_cols(w_in, _OD_AT).T, jnp.asarray(q_scale), BF16)
    an = _proj(h, _gather_cols(w_in, _OD_AN), BF16)
    ub = _proj(h, _gather_cols(w_in, _OD_B), F32)
    yc = _nsa_mixer(at, an, ub, cmp_pe, cmp_w1, cmp_w2)
    yd = _gdn_mixer(ub, conv_w, conv_b, a_log, dt_bias, gdn_norm_w)
    return _out_proj(xs, yc, yd, w_out)


def kernel(x, ln_w, final_ln_w, ev_w_in, ev_w_out, ssd_conv_w, ssd_conv_b, ssd_dt_bias, ssd_a_log, ssd_d,
           ssd_norm_w, gla_w_gate, gla_b_gate, gla_norm_w, od_w_in, od_w_out, nsa_cmp_pe, nsa_cmp_w1,
           nsa_cmp_w2, gdn_conv_w, gdn_conv_b, gdn_a_log, gdn_dt_bias, gdn_norm_w):
    bsz, t_len, d = x.shape
    assert bsz == 1 and d == D_MODEL
    xs = x.reshape(t_len, d)
    for layer in range(DEPTH):
        i = layer // 2
        if layer % 2 == 0:
            xs = _even_layer(xs, ln_w[layer], ev_w_in[i], ev_w_out[i], ssd_conv_w[i], ssd_conv_b[i], ssd_dt_bias[i],
                             ssd_a_log[i], ssd_d[i], ssd_norm_w[i], gla_w_gate[i], gla_b_gate[i], gla_norm_w[i])
        else:
            xs = _odd_layer(xs, ln_w[layer], od_w_in[i], od_w_out[i], nsa_cmp_pe[i], nsa_cmp_w1[i], nsa_cmp_w2[i],
                            gdn_conv_w[i], gdn_conv_b[i], gdn_a_log[i], gdn_dt_bias[i], gdn_norm_w[i])
    return _rmsnorm(xs, final_ln_w, F32).reshape(bsz, t_len, d)
```

```python
import math

import numpy as np
import jax
import jax.numpy as jnp
from jax import lax
from jax.experimental import pallas as pl
from jax.experimental.pallas import tpu as pltpu

D_MODEL = 2048
DEPTH = 4
EPS = 1e-6
NEG = -1e30
MIX_W = D_MODEL

SSD_HEAD_DIM = 64
SSD_HEADS = MIX_W // SSD_HEAD_DIM
SSD_STATE = 128
SSD_GROUPS = 8
SSD_CONV = 4
SSD_CHUNK = 128
SSD_GN = SSD_GROUPS * SSD_STATE
SSD_XBC = MIX_W + 2 * SSD_GN
SSD_IN = MIX_W + SSD_XBC + SSD_HEADS

GLA_HEADS = 4
GLA_DKT = MIX_W // 2
GLA_DK = GLA_DKT // GLA_HEADS
GLA_DV = MIX_W // GLA_HEADS
GLA_RANK = 16
GLA_TAU = 16.0
GLA_CHUNK = 128
GLA_IN = 2 * GLA_DKT + 2 * MIX_W + GLA_RANK

NSA_HEADS = 16
NSA_HEAD_DIM = MIX_W // NSA_HEADS
NSA_KV_GROUPS = 4
NSA_GH = NSA_HEADS // NSA_KV_GROUPS
NSA_KVW = NSA_KV_GROUPS * NSA_HEAD_DIM
NSA_CMP_BLOCK = 32
NSA_CMP_STRIDE = 16
NSA_SLC_BLOCK = 64
NSA_TOPK = 16
NSA_WINDOW = 512
NSA_FORCE_BONUS = 1e3
NSA_IN = 2 * MIX_W + 6 * NSA_KVW + 3 * NSA_HEADS

GDN_QK_HEADS = 8
GDN_V_HEADS = 16
GDN_DK = 128
GDN_DV = MIX_W // GDN_V_HEADS
GDN_CONV = 4
GDN_CHUNK = 64
GDN_QKW = GDN_QK_HEADS * GDN_DK
GDN_CONV_CH = 2 * GDN_QKW + MIX_W
GDN_IN = GDN_CONV_CH + 2 * GDN_V_HEADS + MIX_W

LANES = 128
VMEM_LIMIT = 56 * 1024 * 1024
M_INIT = 0.1 * NEG

BF16 = jnp.bfloat16
F32 = jnp.float32


def _width(pieces):
    return sum(w for _, w in pieces)


def _gather_cols(w, pieces):
    cols = [jnp.zeros((w.shape[0], wd), BF16) if s is None else w[:, s:s + wd].astype(BF16) for s, wd in pieces]
    return jnp.concatenate(cols, axis=1)


_KV0 = MIX_W
_GATES0 = MIX_W + 6 * NSA_KVW

_OD_AT = [(0, MIX_W), (_KV0 + 3 * NSA_KVW, NSA_KVW), (_KV0 + 5 * NSA_KVW, NSA_KVW)]
OD_AT_W = _width(_OD_AT)
OD_AT_VSLC = MIX_W // NSA_HEAD_DIM
OD_AT_VWIN = OD_AT_VSLC + NSA_KV_GROUPS
_OD_AN = [(_KV0 + 2 * NSA_KVW, NSA_KVW), (_KV0 + 4 * NSA_KVW, NSA_KVW)]
OD_AN_W = _width(_OD_AN)
OD_AN_KSLC, OD_AN_KWIN = 0, NSA_KV_GROUPS

_OD_B = [(NSA_IN, GDN_CONV_CH), (NSA_IN + GDN_CONV_CH + 2 * GDN_V_HEADS, MIX_W), (_GATES0 + 3 * NSA_HEADS, MIX_W),
         (_KV0, 2 * NSA_KVW)]
for _g in range(NSA_KV_GROUPS):
    _OD_B += [(_GATES0 + br * NSA_HEADS + _g * NSA_GH, NSA_GH) for br in range(3)] + [(None, LANES - 3 * NSA_GH)]
_OD_B += [(NSA_IN + GDN_CONV_CH, 2 * GDN_V_HEADS), (None, LANES - 2 * GDN_V_HEADS)]
OD_B_W = _width(_OD_B)
OD_B_CONV = 0
OD_B_Z = OD_B_CONV + GDN_CONV_CH
OD_B_GP = OD_B_Z + MIX_W
OD_B_KCMP = OD_B_GP + MIX_W
OD_B_VCMP = OD_B_KCMP + NSA_KVW
OD_B_GATES = OD_B_VCMP + NSA_KVW
OD_B_AB = OD_B_GATES + NSA_KV_GROUPS * LANES

_GLA_R0 = SSD_IN + 2 * GLA_DKT + MIX_W + GLA_RANK
_EV = [(0, MIX_W + SSD_XBC), (SSD_IN, 2 * GLA_DKT + MIX_W), (_GLA_R0, MIX_W),
       (MIX_W + SSD_XBC, SSD_HEADS), (None, LANES - SSD_HEADS),
       (_GLA_R0 - GLA_RANK, GLA_RANK), (None, LANES - GLA_RANK)]
EV_W = _width(_EV)
EV_Z, EV_X = 0, MIX_W
EV_B = EV_X + MIX_W
EV_C = EV_B + SSD_GN
EV_Q = EV_C + SSD_GN
EV_K = EV_Q + GLA_DKT
EV_V = EV_K + GLA_DKT
EV_R = EV_V + MIX_W
EV_DT = EV_R + MIX_W
EV_GLR = EV_DT + LANES


def _pick_tn(n, cap=2048):
    best = LANES
    for k in range(1, n // LANES + 1):
        if (n // LANES) % k == 0 and k * LANES <= cap:
            best = k * LANES
    return best


def _rmsnorm_kernel(x_ref, g_ref, o_ref):
    x = x_ref[...]
    ms = jnp.mean(x * x, axis=-1, keepdims=True)
    o_ref[...] = (x * lax.rsqrt(ms + EPS) * g_ref[...]).astype(o_ref.dtype)


def _rmsnorm(x, g, out_dtype, tm=512):
    t_len, d = x.shape
    return pl.pallas_call(
        _rmsnorm_kernel,
        out_shape=jax.ShapeDtypeStruct((t_len, d), out_dtype),
        grid=(t_len // tm,),
        in_specs=[pl.BlockSpec((tm, d), lambda i: (i, 0)), pl.BlockSpec((1, d), lambda i: (0, 0))],
        out_specs=pl.BlockSpec((tm, d), lambda i: (i, 0)),
        compiler_params=pltpu.CompilerParams(dimension_semantics=("parallel",), vmem_limit_bytes=VMEM_LIMIT),
        name="rmsnorm",
    )(x, g.reshape(1, d))


def _proj_kernel(h_ref, w_ref, o_ref):
    o_ref[...] = jnp.dot(h_ref[...], w_ref[...], preferred_element_type=F32).astype(o_ref.dtype)


def _proj_t_kernel(h_ref, wt_ref, rs_ref, o_ref):
    acc = lax.dot_general(wt_ref[...], h_ref[...], (((1,), (1,)), ((), ())), preferred_element_type=F32)
    o_ref[...] = (acc * rs_ref[...]).astype(o_ref.dtype)


PROJ_TM = 1024


def _proj(h, w, out_dtype):
    t_len, d = h.shape
    n = w.shape[1]
    tm, tn = min(PROJ_TM, t_len), _pick_tn(n)
    return pl.pallas_call(
        _proj_kernel,
        out_shape=jax.ShapeDtypeStruct((t_len, n), out_dtype),
        grid=(n // tn, t_len // tm),
        in_specs=[pl.BlockSpec((tm, d), lambda j, i: (i, 0)), pl.BlockSpec((d, tn), lambda j, i: (0, j))],
        out_specs=pl.BlockSpec((tm, tn), lambda j, i: (i, j)),
        compiler_params=pltpu.CompilerParams(
            dimension_semantics=("parallel", "parallel"), vmem_limit_bytes=VMEM_LIMIT),
        name="proj",
    )(h, w)


def _proj_t(h, wt, rowscale, out_dtype):
    t_len, d = h.shape
    n = wt.shape[0]
    tm, tn = min(PROJ_TM, t_len), _pick_tn(n)
    return pl.pallas_call(
        _proj_t_kernel,
        out_shape=jax.ShapeDtypeStruct((n, t_len), out_dtype),
        grid=(n // tn, t_len // tm),
        in_specs=[pl.BlockSpec((tm, d), lambda j, i: (i, 0)), pl.BlockSpec((tn, d), lambda j, i: (j, 0)),
                  pl.BlockSpec((tn, 1), lambda j, i: (j, 0))],
        out_specs=pl.BlockSpec((tn, tm), lambda j, i: (j, i)),
        compiler_params=pltpu.CompilerParams(
            dimension_semantics=("parallel", "parallel"), vmem_limit_bytes=VMEM_LIMIT),
        name="proj_t",
    )(h, wt, rowscale)


def _out_proj_kernel(x_ref, ya_ref, yb_ref, wa_ref, wb_ref, o_ref, wa16_ref, wb16_ref):
    @pl.when(pl.program_id(1) == 0)
    def _():
        wa16_ref[...] = wa_ref[...].astype(BF16)
        wb16_ref[...] = wb_ref[...].astype(BF16)

    acc = jnp.dot(ya_ref[...], wa16_ref[...], preferred_element_type=F32)
    acc += jnp.dot(yb_ref[...], wb16_ref[...], preferred_element_type=F32)
    o_ref[...] = x_ref[...] + acc


def _out_proj(x, ya, yb, w, tm=1024, tn=512):
    t_len, d = x.shape
    k = ya.shape[1]
    return pl.pallas_call(
        _out_proj_kernel,
        out_shape=jax.ShapeDtypeStruct((t_len, d), F32),
        grid=(d // tn, t_len // tm),
        in_specs=[
            pl.BlockSpec((tm, tn), lambda j, i: (i, j)),
            pl.BlockSpec((tm, k), lambda j, i: (i, 0)),
            pl.BlockSpec((tm, k), lambda j, i: (i, 0)),
            pl.BlockSpec((k, tn), lambda j, i: (0, j)),
            pl.BlockSpec((k, tn), lambda j, i: (1, j)),
        ],
        out_specs=pl.BlockSpec((tm, tn), lambda j, i: (i, j)),
        scratch_shapes=[pltpu.VMEM((k, tn), BF16), pltpu.VMEM((k, tn), BF16)],
        compiler_params=pltpu.CompilerParams(
            dimension_semantics=("parallel", "arbitrary"), vmem_limit_bytes=VMEM_LIMIT),
        name="out_proj",
    )(x, ya, yb, w, w)


NSA_TQ = 256
_NT = (((1,), (1,)), ((), ()))


def _nsa_compress_hidden(u_ref, pe_ref, w1_ref, nc):
    dh = NSA_HEAD_DIM
    acc_a = jnp.zeros((nc, dh), F32)
    acc_b = jnp.zeros((nc, dh), F32)
    for l in range(NSA_CMP_STRIDE):
        x = u_ref[pl.ds(l, nc, stride=NSA_CMP_STRIDE), :]
        xa = (x + pe_ref[l:l + 1, :]).astype(BF16)
        xb = (x + pe_ref[NSA_CMP_STRIDE + l:NSA_CMP_STRIDE + l + 1, :]).astype(BF16)
        acc_a += jnp.dot(xa, w1_ref[l * dh:(l + 1) * dh, :].astype(BF16), preferred_element_type=F32)
        acc_b += jnp.dot(xb, w1_ref[(NSA_CMP_STRIDE + l) * dh:(NSA_CMP_STRIDE + l + 1) * dh, :].astype(BF16),
                         preferred_element_type=F32)
    h = acc_a + pltpu.roll(acc_b, nc - 1, axis=0)
    return (h * jax.nn.sigmoid(h)).astype(BF16)


def _nsa_compress_kernel(uk_ref, uv_ref, pe_ref, w1_ref, w2k_ref, w2vt_ref, kc_ref, vct_ref):
    nc = kc_ref.shape[1]
    hk = _nsa_compress_hidden(uk_ref, pe_ref.at[0], w1_ref.at[0], nc)
    kc_ref[0] = jnp.dot(hk, w2k_ref[...].astype(BF16), preferred_element_type=F32).astype(kc_ref.dtype)
    hv = _nsa_compress_hidden(uv_ref, pe_ref.at[1], w1_ref.at[1], nc)
    vct_ref[0] = lax.dot_general(w2vt_ref[...].astype(BF16), hv, _NT,
                                 preferred_element_type=F32).astype(vct_ref.dtype)


def _nsa_compress(ub, pe, w1, w2):
    t_len = ub.shape[0]
    nc = t_len // NSA_CMP_STRIDE
    g_n, dh = NSA_KV_GROUPS, NSA_HEAD_DIM
    return pl.pallas_call(
        _nsa_compress_kernel,
        out_shape=(jax.ShapeDtypeStruct((g_n, nc, dh), BF16), jax.ShapeDtypeStruct((g_n, dh, nc), BF16)),
        grid=(g_n,),
        in_specs=[
            pl.BlockSpec((t_len, dh), lambda g: (0, OD_B_KCMP // dh + g)),
            pl.BlockSpec((t_len, dh), lambda g: (0, OD_B_VCMP // dh + g)),
            pl.BlockSpec((2, NSA_CMP_BLOCK, dh), lambda g: (0, 0, 0)),
            pl.BlockSpec((2, NSA_CMP_BLOCK * dh, dh), lambda g: (0, 0, 0)),
            pl.BlockSpec((dh, dh), lambda g: (0, 0)),
            pl.BlockSpec((dh, dh), lambda g: (0, 0)),
        ],
        out_specs=(pl.BlockSpec((1, nc, dh), lambda g: (g, 0, 0)), pl.BlockSpec((1, dh, nc), lambda g: (g, 0, 0))),
        compiler_params=pltpu.CompilerParams(dimension_semantics=("parallel",), vmem_limit_bytes=VMEM_LIMIT),
        name="nsa_compress",
    )(ub, ub, pe, w1, w2[0], w2[1].T)


def _nsa_cmp_kernel(qt_ref, kc_ref, vct_ref, ovt_ref, ocmp_ref, sel_ref):
    tq = qt_ref.shape[1]
    nc = kc_ref.shape[1]
    ns = ovt_ref.shape[0]
    dh = NSA_HEAD_DIM
    t = pl.program_id(1) * tq + lax.broadcasted_iota(jnp.int32, (1, tq), 1)
    n_idx = lax.broadcasted_iota(jnp.int32, (nc, 1), 0)
    valid = n_idx * NSA_CMP_STRIDE + (NSA_CMP_BLOCK - 1) <= t
    kc = kc_ref[0]
    vct = vct_ref[0]
    heads = range(NSA_GH)
    s = [jnp.where(valid, jnp.dot(kc, qt_ref[h * dh:(h + 1) * dh, :], preferred_element_type=F32), NEG)
         for h in heads]
    m = [jnp.max(s[h], axis=0, keepdims=True) for h in heads]
    p = [jnp.where(valid, jnp.exp2(s[h] - m[h]), 0.0) for h in heads]
    l = [jnp.sum(p[h], axis=0, keepdims=True) for h in heads]
    p = [p[h] * (1.0 / jnp.where(l[h] > 0.0, l[h], 1.0)) for h in heads]
    for h in heads:
        ocmp_ref[h * dh:(h + 1) * dh, :] = jnp.dot(vct, p[h].astype(BF16), preferred_element_type=F32)
    psum = sum(p[1:], p[0])
    hi = psum.astype(BF16)
    lo = (psum - hi.astype(F32)).astype(BF16)
    ovt = ovt_ref[...]
    imp = jnp.dot(ovt, hi, preferred_element_type=F32) + jnp.dot(ovt, lo, preferred_element_type=F32)
    j = lax.broadcasted_iota(jnp.int32, (ns, 1), 0)
    cur = lax.shift_right_logical(t, int(math.log2(NSA_SLC_BLOCK)))
    forced = (j == 0) | (j == cur) | (j == cur - 1)
    ok = j * NSA_SLC_BLOCK <= t
    imp = jnp.where(ok, imp + jnp.where(forced, NSA_FORCE_BONUS, 0.0), NEG)
    jf = j.astype(F32)

    def pick(_, carry):
        imp, sel = carry
        mx = jnp.max(imp, axis=0, keepdims=True)
        first = jnp.min(jnp.where(imp == mx, jf, float(ns)), axis=0, keepdims=True)
        hit = jf == first
        sel = jnp.where(hit & (mx > 0.5 * NEG), 1.0, sel)
        imp = jnp.where(hit, -jnp.inf, imp)
        return imp, sel

    _, sel = lax.fori_loop(0, min(NSA_TOPK, ns), pick, (imp, jnp.zeros((ns, tq), F32)))
    sel_ref[0] = sel.astype(sel_ref.dtype)


def _nsa_cmp(at, kc, vct, overlap_t):
    t_len = at.shape[1]
    tq = min(NSA_TQ, t_len)
    ns, nc = overlap_t.shape
    g_n, dh = NSA_KV_GROUPS, NSA_HEAD_DIM
    gw = NSA_GH * dh
    return pl.pallas_call(
        _nsa_cmp_kernel,
        out_shape=(jax.ShapeDtypeStruct((MIX_W, t_len), F32), jax.ShapeDtypeStruct((g_n, ns, t_len), BF16)),
        grid=(g_n, t_len // tq),
        in_specs=[
            pl.BlockSpec((gw, tq), lambda g, i: (g, i)),
            pl.BlockSpec((1, nc, dh), lambda g, i: (g, 0, 0)),
            pl.BlockSpec((1, dh, nc), lambda g, i: (g, 0, 0)),
            pl.BlockSpec((ns, nc), lambda g, i: (0, 0)),
        ],
        out_specs=(pl.BlockSpec((gw, tq), lambda g, i: (g, i)), pl.BlockSpec((1, ns, tq), lambda g, i: (g, 0, i))),
        compiler_params=pltpu.CompilerParams(
            dimension_semantics=("parallel", "parallel"), vmem_limit_bytes=VMEM_LIMIT),
        name="nsa_cmp",
    )(at, kc, vct, overlap_t)


def _flash_init(m_ref, l_ref, acc_ref):
    m_ref[...] = jnp.full(m_ref.shape, M_INIT, F32)
    l_ref[...] = jnp.zeros(l_ref.shape, F32)
    acc_ref[...] = jnp.zeros(acc_ref.shape, F32)


def _flash_tile(q_of, k_tile, vt_tile, mask, m_ref, l_ref, acc_ref):
    _flash_consume([jnp.dot(k_tile, q_of(h), preferred_element_type=F32) for h in range(NSA_GH)], vt_tile, mask,
                   m_ref, l_ref, acc_ref)


def _flash_consume(s, vt_tile, mask, m_ref, l_ref, acc_ref):
    heads = range(NSA_GH)
    if mask is not None:
        s = [jnp.where(mask, s[h], NEG) for h in heads]
    m_old = [m_ref[h] for h in heads]
    m_new = [jnp.maximum(m_old[h], jnp.max(s[h], axis=0, keepdims=True)) for h in heads]
    alpha = [jnp.exp2(m_old[h] - m_new[h]) for h in heads]
    p = [jnp.exp2(s[h] - m_new[h]) for h in heads]
    for h in heads:
        l_ref[h] = alpha[h] * l_ref[h] + jnp.sum(p[h], axis=0, keepdims=True)
        m_ref[h] = m_new[h]
    pv = [jnp.dot(vt_tile, p[h].astype(BF16), preferred_element_type=F32) for h in heads]
    for h in heads:
        acc_ref[h] = alpha[h] * acc_ref[h] + pv[h]


def _nsa_slc_kernel(qt_ref, k_ref, vt_ref, sel_ref, eb_ref, o_ref, qa_ref, s0_ref, s1_ref, m_ref, l_ref, acc_ref):
    tq = qt_ref.shape[1]
    dh = NSA_HEAD_DIM
    i = pl.program_id(1)
    t = i * tq + lax.broadcasted_iota(jnp.int32, (1, tq), 1)
    c_idx = lax.broadcasted_iota(jnp.int32, (tq, 1), 0)
    bias = ((sel_ref[0].astype(F32) - 1.0) * -NEG).astype(BF16)
    for h in range(NSA_GH):
        qa_ref[h, :dh, :] = qt_ref[h * dh:(h + 1) * dh, :]
        qa_ref[h, dh:, :] = bias
    _flash_init(m_ref, l_ref, acc_ref)

    def scores(jt, s_ref):
        k0 = pl.multiple_of(jt * tq, tq)
        k_aug = jnp.concatenate([k_ref[pl.ds(k0, tq), :], eb_ref[pl.ds(k0, tq), :]], axis=1)
        for h in range(NSA_GH):
            s_ref[h] = jnp.dot(k_aug, qa_ref[h], preferred_element_type=F32)

    def consume(jt, s_ref, causal):
        k0 = pl.multiple_of(jt * tq, tq)
        mask = (c_idx + k0 <= t) if causal else None
        _flash_consume([s_ref[h] for h in range(NSA_GH)], vt_ref[:, pl.ds(k0, tq)], mask, m_ref, l_ref, acc_ref)

    def pair(p, carry):
        scores(2 * p + 1, s1_ref)
        consume(2 * p, s0_ref, False)
        scores(2 * p + 2, s0_ref)
        consume(2 * p + 1, s1_ref, False)
        return carry

    scores(0, s0_ref)
    lax.fori_loop(0, i // 2, pair, 0)

    @pl.when(i % 2 == 0)
    def _():
        consume(i, s0_ref, True)

    @pl.when(i % 2 == 1)
    def _():
        scores(i, s1_ref)
        consume(i - 1, s0_ref, False)
        consume(i, s1_ref, True)

    for h in range(NSA_GH):
        o_ref[h * dh:(h + 1) * dh, :] = acc_ref[h] * (1.0 / l_ref[h])


def _nsa_win_kernel(qt_ref, k_ref, vt_ref, ocmp_ref, oslc_ref, gates_ref, gp_ref, o_ref, m_ref, l_ref, acc_ref):
    tq = qt_ref.shape[1]
    dh = NSA_HEAD_DIM
    i = pl.program_id(1)
    t = i * tq + lax.broadcasted_iota(jnp.int32, (1, tq), 1)
    c_idx = lax.broadcasted_iota(jnp.int32, (tq, 1), 0)
    _flash_init(m_ref, l_ref, acc_ref)

    def scores(jt):
        k_tile = k_ref[pl.ds(pl.multiple_of(jt * tq, tq), tq), :]
        return [jnp.dot(k_tile, qt_ref[h * dh:(h + 1) * dh, :], preferred_element_type=F32) for h in range(NSA_GH)]

    def consume(jt, s, mask_of):
        k0 = pl.multiple_of(jt * tq, tq)
        mask = None if mask_of is None else mask_of(c_idx + k0)
        _flash_consume(s, vt_ref[:, pl.ds(k0, tq)], mask, m_ref, l_ref, acc_ref)

    edge = lambda key: key > t - NSA_WINDOW
    causal = lambda key: key <= t

    @pl.when(i >= 2)
    def _():
        s_old, s_mid, s_new = scores(i - 2), scores(i - 1), scores(i)
        consume(i - 2, s_old, edge)
        consume(i - 1, s_mid, None)
        consume(i, s_new, causal)

    @pl.when(i == 1)
    def _():
        s_mid, s_new = scores(0), scores(1)
        consume(0, s_mid, None)
        consume(1, s_new, causal)

    @pl.when(i == 0)
    def _():
        consume(0, scores(0), causal)
    gates_t = jnp.transpose(jax.nn.sigmoid(gates_ref[...]))
    for h in range(NSA_GH):
        rows = slice(h * dh, (h + 1) * dh)
        o_t = (gates_t[h:h + 1, :] * ocmp_ref[rows, :]
               + gates_t[NSA_GH + h:NSA_GH + h + 1, :] * oslc_ref[rows, :]
               + gates_t[2 * NSA_GH + h:2 * NSA_GH + h + 1, :] * (acc_ref[h] * (1.0 / l_ref[h])))
        gp = gp_ref[:, rows]
        o_ref[:, rows] = (jnp.transpose(o_t) * (gp * jax.nn.sigmoid(gp))).astype(o_ref.dtype)


def _nsa_scratch(tq):
    return [pltpu.VMEM((NSA_GH, 1, tq), F32), pltpu.VMEM((NSA_GH, 1, tq), F32),
            pltpu.VMEM((NSA_GH, NSA_HEAD_DIM, tq), F32)]


def _nsa_slc(at, an, sel):
    t_len = at.shape[1]
    tq = NSA_TQ
    ns = sel.shape[1]
    g_n, dh = NSA_KV_GROUPS, NSA_HEAD_DIM
    gw = NSA_GH * dh
    block_of_key = jnp.asarray(np.arange(t_len)[:, None] // NSA_SLC_BLOCK == np.arange(ns)[None], dtype=BF16)
    return pl.pallas_call(
        _nsa_slc_kernel,
        out_shape=jax.ShapeDtypeStruct((MIX_W, t_len), F32),
        grid=(g_n, t_len // tq),
        in_specs=[
            pl.BlockSpec((gw, tq), lambda g, i: (g, i)),
            pl.BlockSpec((t_len, dh), lambda g, i: (0, OD_AN_KSLC + g)),
            pl.BlockSpec((dh, t_len), lambda g, i: (OD_AT_VSLC + g, 0)),
            pl.BlockSpec((1, ns, tq), lambda g, i: (g, 0, i)),
            pl.BlockSpec((t_len, ns), lambda g, i: (0, 0)),
        ],
        out_specs=pl.BlockSpec((gw, tq), lambda g, i: (g, i)),
        scratch_shapes=[pltpu.VMEM((NSA_GH, dh + ns, tq), BF16), pltpu.VMEM((NSA_GH, tq, tq), F32),
                        pltpu.VMEM((NSA_GH, tq, tq), F32)] + _nsa_scratch(tq),
        compiler_params=pltpu.CompilerParams(
            dimension_semantics=("parallel", "parallel"), vmem_limit_bytes=VMEM_LIMIT),
        name="nsa_slc",
    )(at, an, at, sel, block_of_key)


def _nsa_win(at, an, ub, ocmp, oslc):
    t_len = at.shape[1]
    tq = NSA_TQ
    assert NSA_WINDOW == 2 * tq
    g_n, dh = NSA_KV_GROUPS, NSA_HEAD_DIM
    gw = NSA_GH * dh
    return pl.pallas_call(
        _nsa_win_kernel,
        out_shape=jax.ShapeDtypeStruct((t_len, MIX_W), BF16),
        grid=(g_n, t_len // tq),
        in_specs=[
            pl.BlockSpec((gw, tq), lambda g, i: (g, i)),
            pl.BlockSpec((t_len, dh), lambda g, i: (0, OD_AN_KWIN + g)),
            pl.BlockSpec((dh, t_len), lambda g, i: (OD_AT_VWIN + g, 0)),
            pl.BlockSpec((gw, tq), lambda g, i: (g, i)),
            pl.BlockSpec((gw, tq), lambda g, i: (g, i)),
            pl.BlockSpec((tq, LANES), lambda g, i: (i, OD_B_GATES // LANES + g)),
            pl.BlockSpec((tq, gw), lambda g, i: (i, OD_B_GP // gw + g)),
        ],
        out_specs=pl.BlockSpec((tq, gw), lambda g, i: (i, g)),
        scratch_shapes=_nsa_scratch(tq),
        compiler_params=pltpu.CompilerParams(
            dimension_semantics=("parallel", "parallel"), vmem_limit_bytes=VMEM_LIMIT),
        name="nsa_win",
    )(at, an, at, ocmp, oslc, ub, ub)


def _nsa_overlap_t(t_len):
    nc = t_len // NSA_CMP_STRIDE
    n_cmp = (t_len - NSA_CMP_BLOCK) // NSA_CMP_STRIDE + 1
    cs = np.arange(n_cmp) * NSA_CMP_STRIDE
    ss = np.arange(t_len // NSA_SLC_BLOCK) * NSA_SLC_BLOCK
    ov = np.clip(np.minimum(cs[:, None] + NSA_CMP_BLOCK, ss[None] + NSA_SLC_BLOCK)
                 - np.maximum(cs[:, None], ss[None]), 0, None) / NSA_CMP_BLOCK
    ov = np.concatenate([ov, np.zeros((nc - n_cmp, ov.shape[1]))], axis=0)
    return jnp.asarray(ov.T, dtype=BF16)


def _nsa_mixer(at, an, ub, cmp_pe, cmp_w1, cmp_w2):
    t_len = an.shape[0]
    kc, vct = _nsa_compress(ub, cmp_pe, cmp_w1, cmp_w2)
    ocmp, sel = _nsa_cmp(at, kc, vct, _nsa_overlap_t(t_len))
    oslc = _nsa_slc(at, an, sel)
    return _nsa_win(at, an, ub, ocmp, oslc)


CONV_TAIL = 8


def _cumsum_rows(x):
    n = x.shape[0]
    row = lax.broadcasted_iota(jnp.int32, x.shape, 0)
    s = 1
    while s < n:
        x = x + jnp.where(row >= s, pltpu.roll(x, s, axis=0), 0.0)
        s *= 2
    return x


def _split_bf16(v, parts):
    out = []
    for _ in range(parts - 1):
        p = v.astype(BF16)
        out.append(p)
        v = v - p.astype(F32)
    out.append(v.astype(BF16))
    return out


def _expand_lanes(v, onehot):
    return sum(jnp.dot(p, onehot, preferred_element_type=F32) for p in _split_bf16(v, 3))


def _conv_silu(x, tail, w, b):
    row = lax.broadcasted_iota(jnp.int32, (CONV_TAIL, 1), 0)
    y = b + w[SSD_CONV - 1:SSD_CONV, :] * x
    for s in range(1, SSD_CONV):
        xs = pltpu.roll(x, s, axis=0)
        head = jnp.where(row < s, pltpu.roll(tail, s, axis=0), xs[:CONV_TAIL, :])
        y = y + w[SSD_CONV - 1 - s:SSD_CONV - s, :] * jnp.concatenate([head, xs[CONV_TAIL:, :]], axis=0)
    return y * jax.nn.sigmoid(y)


def _softplus(x):
    return jnp.maximum(x, 0.0) + jnp.log1p(jnp.exp(-jnp.abs(x)))


_TN = (((0,), (0,)), ((), ()))


def _ssd_kernel(z_ref, x_ref, b_ref, c_ref, dt_ref, cwx_ref, cwb_ref, cwc_ref, cbx_ref, cbb_ref, cbc_ref,
                dtb_ref, alog_ref, dsk_ref, nw_ref, e_ref, o_ref, tail_ref, state_ref, y_ref):
    L = SSD_CHUNK
    n, gw = SSD_STATE, (SSD_HEADS // SSD_GROUPS) * SSD_HEAD_DIM

    @pl.when(pl.program_id(0) == 0)
    def _():
        tail_ref[...] = jnp.zeros(tail_ref.shape, F32)
        state_ref[...] = jnp.zeros(state_ref.shape, F32)

    xr, br, cr = x_ref[...], b_ref[...], c_ref[...]
    x = _conv_silu(xr, tail_ref[:, :MIX_W], cwx_ref[...], cbx_ref[...])
    bm = _conv_silu(br, tail_ref[:, MIX_W:MIX_W + SSD_GN], cwb_ref[...], cbb_ref[...])
    cm = _conv_silu(cr, tail_ref[:, MIX_W + SSD_GN:], cwc_ref[...], cbc_ref[...])
    tail_ref[:, :MIX_W] = xr[L - CONV_TAIL:, :]
    tail_ref[:, MIX_W:MIX_W + SSD_GN] = br[L - CONV_TAIL:, :]
    tail_ref[:, MIX_W + SSD_GN:] = cr[L - CONV_TAIL:, :]

    dt = _softplus(dt_ref[...] + dtb_ref[...])
    cum = _cumsum_rows(dt * -jnp.exp(alog_ref[...]))
    cum_t = jnp.transpose(cum)
    onehot = e_ref[...]
    cum_e = _expand_lanes(cum, onehot)
    xdt = x * _expand_lanes(dt, onehot)
    last_e = cum_e[L - 1:L, :]
    xdt16 = xdt.astype(BF16)
    xw16 = (xdt * jnp.exp(last_e - cum_e)).astype(BF16)
    ecum_e = jnp.exp(cum_e)
    causal = lax.broadcasted_iota(jnp.int32, (L, L), 0) >= lax.broadcasted_iota(jnp.int32, (L, L), 1)
    lane = lax.broadcasted_iota(jnp.int32, (1, gw), 1)
    for g in range(SSD_GROUPS):
        cols = slice(g * gw, (g + 1) * gw)
        bg = bm[:, g * n:(g + 1) * n]
        cg16 = cm[:, g * n:(g + 1) * n].astype(BF16)
        sc = lax.dot_general(cg16, bg.astype(BF16), _NT, preferred_element_type=F32)
        state = state_ref[g]
        yg = jnp.dot(cg16, state.astype(BF16), preferred_element_type=F32) * ecum_e[:, cols]
        xg = xdt16[:, cols]
        for r in range(SSD_HEADS // SSD_GROUPS):
            h = g * (SSD_HEADS // SSD_GROUPS) + r
            decay = jnp.where(causal, jnp.exp(cum[:, h:h + 1] - cum_t[h:h + 1, :]), 0.0)
            in_head = (lane >= r * SSD_HEAD_DIM) & (lane < (r + 1) * SSD_HEAD_DIM)
            yg += jnp.dot((sc * decay).astype(BF16), jnp.where(in_head, xg, jnp.zeros_like(xg)),
                          preferred_element_type=F32)
        y_ref[:, cols] = yg
        bg_t = jnp.transpose(bg).astype(BF16)
        state_ref[g] = state * jnp.exp(last_e[:, cols]) + jnp.dot(bg_t, xw16[:, cols], preferred_element_type=F32)

    y = y_ref[...] + x * dsk_ref[...]
    zz = z_ref[...]
    y = y * (zz * jax.nn.sigmoid(zz))
    ms = jnp.mean(y * y, axis=-1, keepdims=True)
    o_ref[...] = (y * lax.rsqrt(ms + EPS) * nw_ref[...]).astype(o_ref.dtype)


def _ssd_mixer(u, conv_w, conv_b, dt_bias, a_log, d_skip, norm_w):
    t_len = u.shape[0]
    L = SSD_CHUNK
    pad = lambda v: jnp.pad(v, (0, LANES - SSD_HEADS)).reshape(1, LANES)
    onehot = np.zeros((LANES, MIX_W), np.float32)
    onehot[np.arange(MIX_W) // SSD_HEAD_DIM, np.arange(MIX_W)] = 1.0
    row = lambda width, start: pl.BlockSpec((L, width), lambda c: (c, start // width))
    fixed = lambda rows, width, start: pl.BlockSpec((rows, width), lambda c: (0, start // width))
    return pl.pallas_call(
        _ssd_kernel,
        out_shape=jax.ShapeDtypeStruct((t_len, MIX_W), BF16),
        grid=(t_len // L,),
        in_specs=[
            row(MIX_W, EV_Z), row(MIX_W, EV_X), row(SSD_GN, EV_B), row(SSD_GN, EV_C), row(LANES, EV_DT),
            fixed(SSD_CONV, MIX_W, 0), fixed(SSD_CONV, SSD_GN, MIX_W), fixed(SSD_CONV, SSD_GN, MIX_W + SSD_GN),
            fixed(1, MIX_W, 0), fixed(1, SSD_GN, MIX_W), fixed(1, SSD_GN, MIX_W + SSD_GN),
            fixed(1, LANES, 0), fixed(1, LANES, 0), fixed(1, MIX_W, 0), fixed(1, MIX_W, 0),
            fixed(LANES, MIX_W, 0),
        ],
        out_specs=pl.BlockSpec((L, MIX_W), lambda c: (c, 0)),
        scratch_shapes=[pltpu.VMEM((CONV_TAIL, SSD_XBC), F32),
                        pltpu.VMEM((SSD_GROUPS, SSD_STATE, MIX_W // SSD_GROUPS), F32),
                        pltpu.VMEM((L, MIX_W), F32)],
        compiler_params=pltpu.CompilerParams(dimension_semantics=("arbitrary",), vmem_limit_bytes=VMEM_LIMIT),
        name="ssd_mixer",
    )(u, u, u, u, u, conv_w, conv_w, conv_w, conv_b.reshape(1, -1), conv_b.reshape(1, -1), conv_b.reshape(1, -1),
      pad(dt_bias), pad(a_log), jnp.repeat(d_skip, SSD_HEAD_DIM).reshape(1, MIX_W), norm_w.reshape(1, MIX_W),
      jnp.asarray(onehot, dtype=BF16))


def _gla_kernel(q_ref, k_ref, v_ref, r_ref, glr_ref, wg_ref, bg_ref, nw_ref, o_ref, state_ref):
    L, dk, dv = GLA_CHUNK, GLA_DK, GLA_DV

    @pl.when(pl.program_id(0) == 0)
    def _():
        state_ref[...] = jnp.zeros(state_ref.shape, F32)

    g_hi, g_lo = _split_bf16(glr_ref[...], 2)
    w_hi, w_lo = _split_bf16(wg_ref[...], 2)
    logits = (jnp.dot(g_hi, w_hi, preferred_element_type=F32) + jnp.dot(g_hi, w_lo, preferred_element_type=F32)
              + jnp.dot(g_lo, w_hi, preferred_element_type=F32) + bg_ref[...])
    cum = _cumsum_rows(-_softplus(-logits) * (1.0 / GLA_TAU))
    tot = cum[L - 1:L, :]
    half = 0.5 * tot
    q = q_ref[...] * dk ** -0.5
    k = k_ref[...]
    qe = (q * jnp.exp(cum - half)).astype(BF16)
    ke = (k * jnp.exp(half - cum)).astype(BF16)
    qd = (q * jnp.exp(cum)).astype(BF16)
    kd = k * jnp.exp(tot - cum)
    etot = jnp.exp(tot)
    causal = lax.broadcasted_iota(jnp.int32, (L, L), 0) >= lax.broadcasted_iota(jnp.int32, (L, L), 1)
    for h in range(GLA_HEADS):
        ks = slice(h * dk, (h + 1) * dk)
        vs = slice(h * dv, (h + 1) * dv)
        att = lax.dot_general(qe[:, ks], ke[:, ks], _NT, preferred_element_type=F32)
        att = jnp.where(causal, att, 0.0).astype(BF16)
        vh = v_ref[:, vs].astype(BF16)
        state = state_ref[h]
        y = jnp.dot(att, vh, preferred_element_type=F32) + jnp.dot(qd[:, ks], state.astype(BF16),
                                                                   preferred_element_type=F32)
        kd_t = jnp.transpose(kd[:, ks]).astype(BF16)
        decay_col = jnp.transpose(jnp.broadcast_to(etot[:, ks], (8, dk)))[:, 0:1]
        state_ref[h] = state * decay_col + jnp.dot(kd_t, vh, preferred_element_type=F32)
        ms = jnp.mean(y * y, axis=-1, keepdims=True)
        r = r_ref[:, vs]
        o_ref[:, vs] = (y * lax.rsqrt(ms + EPS) * nw_ref[...] * (r * jax.nn.sigmoid(r))).astype(o_ref.dtype)


def _gla_mixer(u, w_gate, b_gate, norm_w):
    t_len = u.shape[0]
    L = GLA_CHUNK
    row = lambda width, start: pl.BlockSpec((L, width), lambda c: (c, start // width))
    full = lambda a: pl.BlockSpec(a.shape, lambda c: (0, 0))
    wg = jnp.pad(w_gate, ((0, LANES - GLA_RANK), (0, 0)))
    bg = b_gate.reshape(1, GLA_DKT)
    nw = norm_w.reshape(1, GLA_DV)
    return pl.pallas_call(
        _gla_kernel,
        out_shape=jax.ShapeDtypeStruct((t_len, MIX_W), BF16),
        grid=(t_len // L,),
        in_specs=[row(GLA_DKT, EV_Q), row(GLA_DKT, EV_K), row(MIX_W, EV_V), row(MIX_W, EV_R), row(LANES, EV_GLR),
                  full(wg), full(bg), full(nw)],
        out_specs=pl.BlockSpec((L, MIX_W), lambda c: (c, 0)),
        scratch_shapes=[pltpu.VMEM((GLA_HEADS, GLA_DK, GLA_DV), F32)],
        compiler_params=pltpu.CompilerParams(dimension_semantics=("arbitrary",), vmem_limit_bytes=VMEM_LIMIT),
        name="gla_mixer",
    )(u, u, u, u, u, wg, bg, nw)


def _unit_lower_inverses(mats):
    n = mats[0].shape[0]
    eye = (lax.broadcasted_iota(jnp.int32, (n, n), 0) == lax.broadcasted_iota(jnp.int32, (n, n), 1)).astype(F32)
    ps = [eye - a for a in mats]
    xs = list(mats)
    for _ in range(int(math.log2(n)) - 1):
        x16 = [x.astype(BF16) for x in xs]
        xs = [jnp.dot(x, x, preferred_element_type=F32) for x in x16]
        ps = [p + jnp.dot(p.astype(BF16), x.astype(BF16), preferred_element_type=F32) for p, x in zip(ps, xs)]
    return ps


def _gdn_kernel(q_ref, k_ref, v_ref, z_ref, ab_ref, cwq_ref, cwk_ref, cwv_ref, cbq_ref, cbk_ref, cbv_ref,
                alog_ref, dtb_ref, nw_ref, o_ref, tail_ref, state_ref):
    L, dk, dv = GDN_CHUNK, GDN_DK, GDN_DV
    rep = GDN_V_HEADS // GDN_QK_HEADS

    @pl.when(pl.program_id(0) == 0)
    def _():
        tail_ref[...] = jnp.zeros(tail_ref.shape, F32)
        state_ref[...] = jnp.zeros(state_ref.shape, F32)

    qr, kr, vr = q_ref[...], k_ref[...], v_ref[...]
    q = _conv_silu(qr, tail_ref[:, :GDN_QKW], cwq_ref[...], cbq_ref[...])
    k = _conv_silu(kr, tail_ref[:, GDN_QKW:2 * GDN_QKW], cwk_ref[...], cbk_ref[...])
    v = _conv_silu(vr, tail_ref[:, 2 * GDN_QKW:], cwv_ref[...], cbv_ref[...])
    tail_ref[:, :GDN_QKW] = qr[L - CONV_TAIL:, :]
    tail_ref[:, GDN_QKW:2 * GDN_QKW] = kr[L - CONV_TAIL:, :]
    tail_ref[:, 2 * GDN_QKW:] = vr[L - CONV_TAIL:, :]

    ab = ab_ref[...]
    beta = jax.nn.sigmoid(ab)
    gc = _cumsum_rows(-jnp.exp(alog_ref[...]) * _softplus(ab + dtb_ref[...]))
    gc_t = jnp.transpose(gc)
    egc = jnp.exp(gc)
    last = gc[L - 1:L, :]
    e_end = jnp.exp(last - gc)
    e_last = jnp.exp(last)
    rows = lax.broadcasted_iota(jnp.int32, (L, L), 0)
    cols = lax.broadcasted_iota(jnp.int32, (L, L), 1)
    heads = range(GDN_V_HEADS)
    qn, kn, kk, qk = [], [], [], []
    for j in range(GDN_QK_HEADS):
        qj = q[:, j * dk:(j + 1) * dk]
        kj = k[:, j * dk:(j + 1) * dk]
        qn.append(qj * (lax.rsqrt(jnp.sum(qj * qj, axis=-1, keepdims=True) + EPS) * dk ** -0.5))
        kn.append(kj * lax.rsqrt(jnp.sum(kj * kj, axis=-1, keepdims=True) + EPS))
        kn16 = kn[j].astype(BF16)
        kk.append(lax.dot_general(kn16, kn16, _NT, preferred_element_type=F32))
        qk.append(lax.dot_general(qn[j].astype(BF16), kn16, _NT, preferred_element_type=F32))
    decay = [jnp.where(rows >= cols, jnp.exp(gc[:, h:h + 1] - gc_t[h:h + 1, :]), 0.0) for h in heads]
    b_col = [beta[:, GDN_V_HEADS + h:GDN_V_HEADS + h + 1] for h in heads]
    t_mat = _unit_lower_inverses([jnp.where(rows > cols, b_col[h] * kk[h // rep] * decay[h], 0.0) for h in heads])
    rhs = [jnp.concatenate([v[:, h * dv:(h + 1) * dv] * b_col[h], kn[h // rep] * (b_col[h] * egc[:, h:h + 1])],
                           axis=1).astype(BF16) for h in heads]
    uw = [jnp.dot(t_mat[h].astype(BF16), rhs[h], preferred_element_type=F32) for h in heads]
    state = [state_ref[h] for h in heads]
    s16 = [s.astype(BF16) for s in state]
    vn16 = [(uw[h][:, :dv] - jnp.dot(uw[h][:, dv:].astype(BF16), s16[h], preferred_element_type=F32)).astype(BF16)
            for h in heads]
    out = [jnp.dot((qn[h // rep] * egc[:, h:h + 1]).astype(BF16), s16[h], preferred_element_type=F32)
           + jnp.dot((qk[h // rep] * decay[h]).astype(BF16), vn16[h], preferred_element_type=F32) for h in heads]
    for h in heads:
        kd_t = jnp.transpose(kn[h // rep] * e_end[:, h:h + 1]).astype(BF16)
        state_ref[h] = state[h] * e_last[:, h:h + 1] + jnp.dot(kd_t, vn16[h], preferred_element_type=F32)
    for h in heads:
        o = out[h]
        ms = jnp.mean(o * o, axis=-1, keepdims=True)
        zz = z_ref[:, h * dv:(h + 1) * dv]
        o_ref[:, h * dv:(h + 1) * dv] = (o * lax.rsqrt(ms + EPS) * nw_ref[...]
                                         * (zz * jax.nn.sigmoid(zz))).astype(o_ref.dtype)


def _gdn_mixer(ub, conv_w, conv_b, a_log, dt_bias, norm_w):
    t_len = ub.shape[0]
    L = GDN_CHUNK
    pad = lambda v: jnp.pad(v, (0, LANES - GDN_V_HEADS)).reshape(1, LANES)
    row = lambda width, start: pl.BlockSpec((L, width), lambda c: (c, start // width))
    fixed = lambda rows, width, start: pl.BlockSpec((rows, width), lambda c: (0, start // width))
    cb = conv_b.reshape(1, -1)
    return pl.pallas_call(
        _gdn_kernel,
        out_shape=jax.ShapeDtypeStruct((t_len, MIX_W), BF16),
        grid=(t_len // L,),
        in_specs=[
            row(GDN_QKW, OD_B_CONV), row(GDN_QKW, OD_B_CONV + GDN_QKW), row(MIX_W, OD_B_CONV + 2 * GDN_QKW),
            row(MIX_W, OD_B_Z), row(LANES, OD_B_AB),
            fixed(GDN_CONV, GDN_QKW, 0), fixed(GDN_CONV, GDN_QKW, GDN_QKW), fixed(GDN_CONV, MIX_W, 2 * GDN_QKW),
            fixed(1, GDN_QKW, 0), fixed(1, GDN_QKW, GDN_QKW), fixed(1, MIX_W, 2 * GDN_QKW),
            fixed(1, LANES, 0), fixed(1, LANES, 0), fixed(1, GDN_DV, 0),
        ],
        out_specs=pl.BlockSpec((L, MIX_W), lambda c: (c, 0)),
        scratch_shapes=[pltpu.VMEM((CONV_TAIL, GDN_CONV_CH), F32),
                        pltpu.VMEM((GDN_V_HEADS, GDN_DK, GDN_DV), F32)],
        compiler_params=pltpu.CompilerParams(dimension_semantics=("arbitrary",), vmem_limit_bytes=VMEM_LIMIT),
        name="gdn_mixer",
    )(ub, ub, ub, ub, ub, conv_w, conv_w, conv_w, cb, cb, cb, pad(a_log), pad(dt_bias),
      norm_w.reshape(1, GDN_DV))


def _even_layer(xs, ln_w, w_in, w_out, conv_w, conv_b, dt_bias, a_log, d_skip, ssd_norm_w, w_gate, b_gate,
                gla_norm_w):
    u = _proj(_rmsnorm(xs, ln_w, BF16), _gather_cols(w_in, _EV), F32)
    ya = _ssd_mixer(u, conv_w, conv_b, dt_bias, a_log, d_skip, ssd_norm_w)
    yb = _gla_mixer(u, w_gate, b_gate, gla_norm_w)
    return _out_proj(xs, ya, yb, w_out)


def _odd_layer(xs, ln_w, w_in, w_out, cmp_pe, cmp_w1, cmp_w2, conv_w, conv_b, a_log, dt_bias, gdn_norm_w):
    q_scale = np.ones((OD_AT_W, 1), np.float32)
    q_scale[:MIX_W] = NSA_HEAD_DIM ** -0.5 * math.log2(math.e)
    h = _rmsnorm(xs, ln_w, BF16)
    at = _proj_t(h, _gather_cols(w_in, _OD_AT).T, jnp.asarray(q_scale), BF16)
    an = _proj(h, _gather_cols(w_in, _OD_AN), BF16)
    ub = _proj(h, _gather_cols(w_in, _OD_B), F32)
    yc = _nsa_mixer(at, an, ub, cmp_pe, cmp_w1, cmp_w2)
    yd = _gdn_mixer(ub, conv_w, conv_b, a_log, dt_bias, gdn_norm_w)
    return _out_proj(xs, yc, yd, w_out)


def kernel(x, ln_w, final_ln_w, ev_w_in, ev_w_out, ssd_conv_w, ssd_conv_b, ssd_dt_bias, ssd_a_log, ssd_d,
           ssd_norm_w, gla_w_gate, gla_b_gate, gla_norm_w, od_w_in, od_w_out, nsa_cmp_pe, nsa_cmp_w1,
           nsa_cmp_w2, gdn_conv_w, gdn_conv_b, gdn_a_log, gdn_dt_bias, gdn_norm_w):
    bsz, t_len, d = x.shape
    assert bsz == 1 and d == D_MODEL
    xs = x.reshape(t_len, d)
    for layer in range(DEPTH):
        i = layer // 2
        if layer % 2 == 0:
            xs = _even_layer(xs, ln_w[layer], ev_w_in[i], ev_w_out[i], ssd_conv_w[i], ssd_conv_b[i], ssd_dt_bias[i],
                             ssd_a_log[i], ssd_d[i], ssd_norm_w[i], gla_w_gate[i], gla_b_gate[i], gla_norm_w[i])
        else:
            xs = _odd_layer(xs, ln_w[layer], od_w_in[i], od_w_out[i], nsa_cmp_pe[i], nsa_cmp_w1[i], nsa_cmp_w2[i],
                            gdn_conv_w[i], gdn_conv_b[i], gdn_a_log[i], gdn_dt_bias[i], gdn_norm_w[i])
    return _rmsnorm(xs, final_ln_w, F32).reshape(bsz, t_len, d)
```

```python
import math

import numpy as np
import jax
import jax.numpy as jnp
from jax import lax
from jax.experimental import pallas as pl
from jax.experimental.pallas import tpu as pltpu

D_MODEL = 2048
DEPTH = 4
EPS = 1e-6
NEG = -1e30
MIX_W = D_MODEL

SSD_HEAD_DIM = 64
SSD_HEADS = MIX_W // SSD_HEAD_DIM
SSD_STATE = 128
SSD_GROUPS = 8
SSD_CONV = 4
SSD_CHUNK = 128
SSD_GN = SSD_GROUPS * SSD_STATE
SSD_XBC = MIX_W + 2 * SSD_GN
SSD_IN = MIX_W + SSD_XBC + SSD_HEADS

GLA_HEADS = 4
GLA_DKT = MIX_W // 2
GLA_DK = GLA_DKT // GLA_HEADS
GLA_DV = MIX_W // GLA_HEADS
GLA_RANK = 16
GLA_TAU = 16.0
GLA_CHUNK = 128
GLA_IN = 2 * GLA_DKT + 2 * MIX_W + GLA_RANK

NSA_HEADS = 16
NSA_HEAD_DIM = MIX_W // NSA_HEADS
NSA_KV_GROUPS = 4
NSA_GH = NSA_HEADS // NSA_KV_GROUPS
NSA_KVW = NSA_KV_GROUPS * NSA_HEAD_DIM
NSA_CMP_BLOCK = 32
NSA_CMP_STRIDE = 16
NSA_SLC_BLOCK = 64
NSA_TOPK = 16
NSA_WINDOW = 512
NSA_FORCE_BONUS = 1e3
NSA_IN = 2 * MIX_W + 6 * NSA_KVW + 3 * NSA_HEADS

GDN_QK_HEADS = 8
GDN_V_HEADS = 16
GDN_DK = 128
GDN_DV = MIX_W // GDN_V_HEADS
GDN_CONV = 4
GDN_CHUNK = 64
GDN_QKW = GDN_QK_HEADS * GDN_DK
GDN_CONV_CH = 2 * GDN_QKW + MIX_W
GDN_IN = GDN_CONV_CH + 2 * GDN_V_HEADS + MIX_W

LANES = 128
VMEM_LIMIT = 56 * 1024 * 1024
M_INIT = 0.1 * NEG

BF16 = jnp.bfloat16
F32 = jnp.float32


def _width(pieces):
    return sum(w for _, w in pieces)


def _gather_cols(w, pieces):
    cols = [jnp.zeros(w.shape[:2] + (wd,), BF16) if s is None else w[:, :, s:s + wd].astype(BF16) for s, wd in pieces]
    return jnp.concatenate(cols, axis=2)


_KV0 = MIX_W
_GATES0 = MIX_W + 6 * NSA_KVW

_OD_AT = [(0, MIX_W), (_KV0 + 3 * NSA_KVW, NSA_KVW), (_KV0 + 5 * NSA_KVW, NSA_KVW)]
OD_AT_W = _width(_OD_AT)
OD_AT_VSLC = MIX_W // NSA_HEAD_DIM
OD_AT_VWIN = OD_AT_VSLC + NSA_KV_GROUPS
_OD_AN = [(_KV0 + 2 * NSA_KVW, NSA_KVW), (_KV0 + 4 * NSA_KVW, NSA_KVW)]
OD_AN_W = _width(_OD_AN)
OD_AN_KSLC, OD_AN_KWIN = 0, NSA_KV_GROUPS

_OD_B = [(NSA_IN, GDN_CONV_CH), (NSA_IN + GDN_CONV_CH + 2 * GDN_V_HEADS, MIX_W), (_GATES0 + 3 * NSA_HEADS, MIX_W),
         (_KV0, 2 * NSA_KVW)]
for _g in range(NSA_KV_GROUPS):
    _OD_B += [(_GATES0 + br * NSA_HEADS + _g * NSA_GH, NSA_GH) for br in range(3)] + [(None, LANES - 3 * NSA_GH)]
_OD_B += [(NSA_IN + GDN_CONV_CH, 2 * GDN_V_HEADS), (None, LANES - 2 * GDN_V_HEADS)]
OD_B_W = _width(_OD_B)
OD_B_CONV = 0
OD_B_Z = OD_B_CONV + GDN_CONV_CH
OD_B_GP = OD_B_Z + MIX_W
OD_B_KCMP = OD_B_GP + MIX_W
OD_B_VCMP = OD_B_KCMP + NSA_KVW
OD_B_GATES = OD_B_VCMP + NSA_KVW
OD_B_AB = OD_B_GATES + NSA_KV_GROUPS * LANES

_GLA_R0 = SSD_IN + 2 * GLA_DKT + MIX_W + GLA_RANK
_EV = [(0, MIX_W + SSD_XBC), (SSD_IN, 2 * GLA_DKT + MIX_W), (_GLA_R0, MIX_W),
       (MIX_W + SSD_XBC, SSD_HEADS), (None, LANES - SSD_HEADS),
       (_GLA_R0 - GLA_RANK, GLA_RANK), (None, LANES - GLA_RANK)]
EV_W = _width(_EV)
EV_Z, EV_X = 0, MIX_W
EV_B = EV_X + MIX_W
EV_C = EV_B + SSD_GN
EV_Q = EV_C + SSD_GN
EV_K = EV_Q + GLA_DKT
EV_V = EV_K + GLA_DKT
EV_R = EV_V + MIX_W
EV_DT = EV_R + MIX_W
EV_GLR = EV_DT + LANES


def _pick_tn(n, cap=2048):
    best = LANES
    for k in range(1, n // LANES + 1):
        if (n // LANES) % k == 0 and k * LANES <= cap:
            best = k * LANES
    return best


def _rmsnorm_kernel(x_ref, g_ref, o_ref):
    x = x_ref[...]
    ms = jnp.mean(x * x, axis=-1, keepdims=True)
    o_ref[...] = (x * lax.rsqrt(ms + EPS) * g_ref[...]).astype(o_ref.dtype)


def _rmsnorm(x, g, out_dtype, tm=512):
    t_len, d = x.shape
    return pl.pallas_call(
        _rmsnorm_kernel,
        out_shape=jax.ShapeDtypeStruct((t_len, d), out_dtype),
        grid=(t_len // tm,),
        in_specs=[pl.BlockSpec((tm, d), lambda i: (i, 0)), pl.BlockSpec((1, d), lambda i: (0, 0))],
        out_specs=pl.BlockSpec((tm, d), lambda i: (i, 0)),
        compiler_params=pltpu.CompilerParams(dimension_semantics=("parallel",), vmem_limit_bytes=VMEM_LIMIT),
        name="rmsnorm",
    )(x, g.reshape(1, d))


def _proj_kernel(h_ref, w_ref, o_ref):
    o_ref[...] = jnp.dot(h_ref[...], w_ref[...], preferred_element_type=F32).astype(o_ref.dtype)


def _proj_t_kernel(h_ref, wt_ref, rs_ref, o_ref):
    acc = lax.dot_general(wt_ref[...], h_ref[...], (((1,), (1,)), ((), ())), preferred_element_type=F32)
    o_ref[...] = (acc * rs_ref[...]).astype(o_ref.dtype)


PROJ_TM = 1024


def _proj(h, w, layer, out_dtype):
    t_len, d = h.shape
    n = w.shape[2]
    tm, tn = min(PROJ_TM, t_len), _pick_tn(n)
    return pl.pallas_call(
        _proj_kernel,
        out_shape=jax.ShapeDtypeStruct((t_len, n), out_dtype),
        grid=(n // tn, t_len // tm),
        in_specs=[pl.BlockSpec((tm, d), lambda j, i: (i, 0)), pl.BlockSpec((None, d, tn), lambda j, i: (layer, 0, j))],
        out_specs=pl.BlockSpec((tm, tn), lambda j, i: (i, j)),
        compiler_params=pltpu.CompilerParams(
            dimension_semantics=("parallel", "parallel"), vmem_limit_bytes=VMEM_LIMIT),
        name="proj",
    )(h, w)


def _proj_t(h, wt, layer, rowscale, out_dtype):
    t_len, d = h.shape
    n = wt.shape[1]
    tm, tn = min(PROJ_TM, t_len), _pick_tn(n)
    return pl.pallas_call(
        _proj_t_kernel,
        out_shape=jax.ShapeDtypeStruct((n, t_len), out_dtype),
        grid=(n // tn, t_len // tm),
        in_specs=[pl.BlockSpec((tm, d), lambda j, i: (i, 0)),
                  pl.BlockSpec((None, tn, d), lambda j, i: (layer, j, 0)),
                  pl.BlockSpec((tn, 1), lambda j, i: (j, 0))],
        out_specs=pl.BlockSpec((tn, tm), lambda j, i: (j, i)),
        compiler_params=pltpu.CompilerParams(
            dimension_semantics=("parallel", "parallel"), vmem_limit_bytes=VMEM_LIMIT),
        name="proj_t",
    )(h, wt, rowscale)


def _out_proj_kernel(x_ref, ya_ref, yb_ref, wa_ref, wb_ref, o_ref, wa16_ref, wb16_ref):
    @pl.when(pl.program_id(1) == 0)
    def _():
        wa16_ref[...] = wa_ref[...].astype(BF16)
        wb16_ref[...] = wb_ref[...].astype(BF16)

    acc = jnp.dot(ya_ref[...], wa16_ref[...], preferred_element_type=F32)
    acc += jnp.dot(yb_ref[...], wb16_ref[...], preferred_element_type=F32)
    o_ref[...] = x_ref[...] + acc


def _out_proj(x, ya, yb, w, layer, tm=1024, tn=512):
    t_len, d = x.shape
    k = ya.shape[1]
    return pl.pallas_call(
        _out_proj_kernel,
        out_shape=jax.ShapeDtypeStruct((t_len, d), F32),
        grid=(d // tn, t_len // tm),
        in_specs=[
            pl.BlockSpec((tm, tn), lambda j, i: (i, j)),
            pl.BlockSpec((tm, k), lambda j, i: (i, 0)),
            pl.BlockSpec((tm, k), lambda j, i: (i, 0)),
            pl.BlockSpec((None, k, tn), lambda j, i: (layer, 0, j)),
            pl.BlockSpec((None, k, tn), lambda j, i: (layer, 1, j)),
        ],
        out_specs=pl.BlockSpec((tm, tn), lambda j, i: (i, j)),
        scratch_shapes=[pltpu.VMEM((k, tn), BF16), pltpu.VMEM((k, tn), BF16)],
        compiler_params=pltpu.CompilerParams(
            dimension_semantics=("parallel", "arbitrary"), vmem_limit_bytes=VMEM_LIMIT),
        name="out_proj",
    )(x, ya, yb, w, w)


NSA_TQ = 256
_NT = (((1,), (1,)), ((), ()))


def _nsa_compress_hidden(u_ref, pe_ref, w1_ref, nc):
    dh = NSA_HEAD_DIM
    acc_a = jnp.zeros((nc, dh), F32)
    acc_b = jnp.zeros((nc, dh), F32)
    for l in range(NSA_CMP_STRIDE):
        x = u_ref[pl.ds(l, nc, stride=NSA_CMP_STRIDE), :]
        xa = (x + pe_ref[l:l + 1, :]).astype(BF16)
        xb = (x + pe_ref[NSA_CMP_STRIDE + l:NSA_CMP_STRIDE + l + 1, :]).astype(BF16)
        acc_a += jnp.dot(xa, w1_ref[l * dh:(l + 1) * dh, :].astype(BF16), preferred_element_type=F32)
        acc_b += jnp.dot(xb, w1_ref[(NSA_CMP_STRIDE + l) * dh:(NSA_CMP_STRIDE + l + 1) * dh, :].astype(BF16),
                         preferred_element_type=F32)
    h = acc_a + pltpu.roll(acc_b, nc - 1, axis=0)
    return (h * jax.nn.sigmoid(h)).astype(BF16)


def _nsa_compress_kernel(uk_ref, uv_ref, pe_ref, w1_ref, w2k_ref, w2vt_ref, kc_ref, vct_ref):
    nc = kc_ref.shape[1]
    hk = _nsa_compress_hidden(uk_ref, pe_ref.at[0], w1_ref.at[0], nc)
    kc_ref[0] = jnp.dot(hk, w2k_ref[...].astype(BF16), preferred_element_type=F32).astype(kc_ref.dtype)
    hv = _nsa_compress_hidden(uv_ref, pe_ref.at[1], w1_ref.at[1], nc)
    vct_ref[0] = lax.dot_general(w2vt_ref[...].astype(BF16), hv, _NT,
                                 preferred_element_type=F32).astype(vct_ref.dtype)


def _nsa_compress(ub, pe, w1, w2):
    t_len = ub.shape[0]
    nc = t_len // NSA_CMP_STRIDE
    g_n, dh = NSA_KV_GROUPS, NSA_HEAD_DIM
    return pl.pallas_call(
        _nsa_compress_kernel,
        out_shape=(jax.ShapeDtypeStruct((g_n, nc, dh), BF16), jax.ShapeDtypeStruct((g_n, dh, nc), BF16)),
        grid=(g_n,),
        in_specs=[
            pl.BlockSpec((t_len, dh), lambda g: (0, OD_B_KCMP // dh + g)),
            pl.BlockSpec((t_len, dh), lambda g: (0, OD_B_VCMP // dh + g)),
            pl.BlockSpec((2, NSA_CMP_BLOCK, dh), lambda g: (0, 0, 0)),
            pl.BlockSpec((2, NSA_CMP_BLOCK * dh, dh), lambda g: (0, 0, 0)),
            pl.BlockSpec((dh, dh), lambda g: (0, 0)),
            pl.BlockSpec((dh, dh), lambda g: (0, 0)),
        ],
        out_specs=(pl.BlockSpec((1, nc, dh), lambda g: (g, 0, 0)), pl.BlockSpec((1, dh, nc), lambda g: (g, 0, 0))),
        compiler_params=pltpu.CompilerParams(dimension_semantics=("parallel",), vmem_limit_bytes=VMEM_LIMIT),
        name="nsa_compress",
    )(ub, ub, pe, w1, w2[0], w2[1].T)


def _nsa_cmp_kernel(qt_ref, kc_ref, vct_ref, ovt_ref, ocmp_ref, sel_ref):
    tq = qt_ref.shape[1]
    nc = kc_ref.shape[1]
    ns = ovt_ref.shape[0]
    dh = NSA_HEAD_DIM
    t = pl.program_id(1) * tq + lax.broadcasted_iota(jnp.int32, (1, tq), 1)
    n_idx = lax.broadcasted_iota(jnp.int32, (nc, 1), 0)
    valid = n_idx * NSA_CMP_STRIDE + (NSA_CMP_BLOCK - 1) <= t
    kc = kc_ref[0]
    vct = vct_ref[0]
    heads = range(NSA_GH)
    s = [jnp.where(valid, jnp.dot(kc, qt_ref[h * dh:(h + 1) * dh, :], preferred_element_type=F32), NEG)
         for h in heads]
    m = [jnp.max(s[h], axis=0, keepdims=True) for h in heads]
    p = [jnp.where(valid, jnp.exp2(s[h] - m[h]), 0.0) for h in heads]
    l = [jnp.sum(p[h], axis=0, keepdims=True) for h in heads]
    p = [p[h] * (1.0 / jnp.where(l[h] > 0.0, l[h], 1.0)) for h in heads]
    for h in heads:
        ocmp_ref[h * dh:(h + 1) * dh, :] = jnp.dot(vct, p[h].astype(BF16), preferred_element_type=F32)
    psum = sum(p[1:], p[0])
    hi = psum.astype(BF16)
    lo = (psum - hi.astype(F32)).astype(BF16)
    ovt = ovt_ref[...]
    imp = jnp.dot(ovt, hi, preferred_element_type=F32) + jnp.dot(ovt, lo, preferred_element_type=F32)
    j = lax.broadcasted_iota(jnp.int32, (ns, 1), 0)
    cur = lax.shift_right_logical(t, int(math.log2(NSA_SLC_BLOCK)))
    forced = (j == 0) | (j == cur) | (j == cur - 1)
    ok = j * NSA_SLC_BLOCK <= t
    imp = jnp.where(ok, imp + jnp.where(forced, NSA_FORCE_BONUS, 0.0), NEG)
    jf = j.astype(F32)

    def pick(_, left):
        mx = jnp.max(left, axis=0, keepdims=True)
        first = jnp.min(jnp.where(left == mx, jf, float(ns)), axis=0, keepdims=True)
        return jnp.where(jf == first, -jnp.inf, left)

    left = lax.fori_loop(0, min(NSA_TOPK, ns), pick, imp)
    sel_ref[0] = jnp.where((left == -jnp.inf) & ok, 1.0, 0.0).astype(sel_ref.dtype)


def _nsa_cmp(at, kc, vct, overlap_t):
    t_len = at.shape[1]
    tq = min(NSA_TQ, t_len)
    ns, nc = overlap_t.shape
    g_n, dh = NSA_KV_GROUPS, NSA_HEAD_DIM
    gw = NSA_GH * dh
    return pl.pallas_call(
        _nsa_cmp_kernel,
        out_shape=(jax.ShapeDtypeStruct((MIX_W, t_len), F32), jax.ShapeDtypeStruct((g_n, ns, t_len), BF16)),
        grid=(g_n, t_len // tq),
        in_specs=[
            pl.BlockSpec((gw, tq), lambda g, i: (g, i)),
            pl.BlockSpec((1, nc, dh), lambda g, i: (g, 0, 0)),
            pl.BlockSpec((1, dh, nc), lambda g, i: (g, 0, 0)),
            pl.BlockSpec((ns, nc), lambda g, i: (0, 0)),
        ],
        out_specs=(pl.BlockSpec((gw, tq), lambda g, i: (g, i)), pl.BlockSpec((1, ns, tq), lambda g, i: (g, 0, i))),
        compiler_params=pltpu.CompilerParams(
            dimension_semantics=("parallel", "parallel"), vmem_limit_bytes=VMEM_LIMIT),
        name="nsa_cmp",
    )(at, kc, vct, overlap_t)


def _flash_init(m_ref, l_ref, acc_ref):
    m_ref[...] = jnp.full(m_ref.shape, M_INIT, F32)
    l_ref[...] = jnp.zeros(l_ref.shape, F32)
    acc_ref[...] = jnp.zeros(acc_ref.shape, F32)


def _flash_tile(q_of, k_tile, vt_tile, mask, m_ref, l_ref, acc_ref):
    _flash_consume([jnp.dot(k_tile, q_of(h), preferred_element_type=F32) for h in range(NSA_GH)], vt_tile, mask,
                   m_ref, l_ref, acc_ref)


def _flash_consume(s, vt_tile, mask, m_ref, l_ref, acc_ref):
    heads = range(NSA_GH)
    if mask is not None:
        s = [jnp.where(mask, s[h], NEG) for h in heads]
    m_old = [m_ref[h] for h in heads]
    m_new = [jnp.maximum(m_old[h], jnp.max(s[h], axis=0, keepdims=True)) for h in heads]
    alpha = [jnp.exp2(m_old[h] - m_new[h]) for h in heads]
    p = [jnp.exp2(s[h] - m_new[h]) for h in heads]
    for h in heads:
        l_ref[h] = alpha[h] * l_ref[h] + jnp.sum(p[h], axis=0, keepdims=True)
        m_ref[h] = m_new[h]
    pv = [jnp.dot(vt_tile, p[h].astype(BF16), preferred_element_type=F32) for h in heads]
    for h in heads:
        acc_ref[h] = alpha[h] * acc_ref[h] + pv[h]


def _nsa_slc_kernel(qt_ref, k_ref, vt_ref, sel_ref, eb_ref, o_ref, qa_ref, s0_ref, s1_ref, m_ref, l_ref, acc_ref):
    tq = qt_ref.shape[1]
    dh = NSA_HEAD_DIM
    i = pl.program_id(1)
    t = i * tq + lax.broadcasted_iota(jnp.int32, (1, tq), 1)
    c_idx = lax.broadcasted_iota(jnp.int32, (tq, 1), 0)
    bias = ((sel_ref[0].astype(F32) - 1.0) * -NEG).astype(BF16)
    for h in range(NSA_GH):
        qa_ref[h, :dh, :] = qt_ref[h * dh:(h + 1) * dh, :]
        qa_ref[h, dh:, :] = bias
    _flash_init(m_ref, l_ref, acc_ref)

    def scores(jt, s_ref):
        k0 = pl.multiple_of(jt * tq, tq)
        k_aug = jnp.concatenate([k_ref[pl.ds(k0, tq), :], eb_ref[pl.ds(k0, tq), :]], axis=1)
        for h in range(NSA_GH):
            s_ref[h] = jnp.dot(k_aug, qa_ref[h], preferred_element_type=F32)

    def consume(jt, s_ref, causal):
        k0 = pl.multiple_of(jt * tq, tq)
        mask = (c_idx + k0 <= t) if causal else None
        _flash_consume([s_ref[h] for h in range(NSA_GH)], vt_ref[:, pl.ds(k0, tq)], mask, m_ref, l_ref, acc_ref)

    def pair(p, carry):
        scores(2 * p + 1, s1_ref)
        consume(2 * p, s0_ref, False)
        scores(2 * p + 2, s0_ref)
        consume(2 * p + 1, s1_ref, False)
        return carry

    scores(0, s0_ref)
    lax.fori_loop(0, i // 2, pair, 0)

    @pl.when(i % 2 == 0)
    def _():
        consume(i, s0_ref, True)

    @pl.when(i % 2 == 1)
    def _():
        scores(i, s1_ref)
        consume(i - 1, s0_ref, False)
        consume(i, s1_ref, True)

    for h in range(NSA_GH):
        o_ref[h * dh:(h + 1) * dh, :] = acc_ref[h] * (1.0 / l_ref[h])


def _nsa_win_kernel(qt_ref, k_ref, vt_ref, ocmp_ref, oslc_ref, gates_ref, gp_ref, o_ref, m_ref, l_ref, acc_ref):
    tq = qt_ref.shape[1]
    dh = NSA_HEAD_DIM
    i = pl.program_id(1)
    t = i * tq + lax.broadcasted_iota(jnp.int32, (1, tq), 1)
    c_idx = lax.broadcasted_iota(jnp.int32, (tq, 1), 0)
    _flash_init(m_ref, l_ref, acc_ref)

    def scores(jt):
        k_tile = k_ref[pl.ds(pl.multiple_of(jt * tq, tq), tq), :]
        return [jnp.dot(k_tile, qt_ref[h * dh:(h + 1) * dh, :], preferred_element_type=F32) for h in range(NSA_GH)]

    def consume(jt, s, mask_of):
        k0 = pl.multiple_of(jt * tq, tq)
        mask = None if mask_of is None else mask_of(c_idx + k0)
        _flash_consume(s, vt_ref[:, pl.ds(k0, tq)], mask, m_ref, l_ref, acc_ref)

    edge = lambda key: key > t - NSA_WINDOW
    causal = lambda key: key <= t

    @pl.when(i >= 2)
    def _():
        s_old, s_mid, s_new = scores(i - 2), scores(i - 1), scores(i)
        consume(i - 2, s_old, edge)
        consume(i - 1, s_mid, None)
        consume(i, s_new, causal)

    @pl.when(i == 1)
    def _():
        s_mid, s_new = scores(0), scores(1)
        consume(0, s_mid, None)
        consume(1, s_new, causal)

    @pl.when(i == 0)
    def _():
        consume(0, scores(0), causal)
    gates_t = jnp.transpose(jax.nn.sigmoid(gates_ref[...]))
    for h in range(NSA_GH):
        rows = slice(h * dh, (h + 1) * dh)
        o_t = (gates_t[h:h + 1, :] * ocmp_ref[rows, :]
               + gates_t[NSA_GH + h:NSA_GH + h + 1, :] * oslc_ref[rows, :]
               + gates_t[2 * NSA_GH + h:2 * NSA_GH + h + 1, :] * (acc_ref[h] * (1.0 / l_ref[h])))
        gp = gp_ref[:, rows]
        o_ref[:, rows] = (jnp.transpose(o_t) * (gp * jax.nn.sigmoid(gp))).astype(o_ref.dtype)


def _nsa_scratch(tq):
    return [pltpu.VMEM((NSA_GH, 1, tq), F32), pltpu.VMEM((NSA_GH, 1, tq), F32),
            pltpu.VMEM((NSA_GH, NSA_HEAD_DIM, tq), F32)]


def _nsa_slc(at, an, sel):
    t_len = at.shape[1]
    tq = NSA_TQ
    ns = sel.shape[1]
    g_n, dh = NSA_KV_GROUPS, NSA_HEAD_DIM
    gw = NSA_GH * dh
    block_of_key = jnp.asarray(np.arange(t_len)[:, None] // NSA_SLC_BLOCK == np.arange(ns)[None], dtype=BF16)
    return pl.pallas_call(
        _nsa_slc_kernel,
        out_shape=jax.ShapeDtypeStruct((MIX_W, t_len), F32),
        grid=(g_n, t_len // tq),
        in_specs=[
            pl.BlockSpec((gw, tq), lambda g, i: (g, i)),
            pl.BlockSpec((t_len, dh), lambda g, i: (0, OD_AN_KSLC + g)),
            pl.BlockSpec((dh, t_len), lambda g, i: (OD_AT_VSLC + g, 0)),
            pl.BlockSpec((1, ns, tq), lambda g, i: (g, 0, i)),
            pl.BlockSpec((t_len, ns), lambda g, i: (0, 0)),
        ],
        out_specs=pl.BlockSpec((gw, tq), lambda g, i: (g, i)),
        scratch_shapes=[pltpu.VMEM((NSA_GH, dh + ns, tq), BF16), pltpu.VMEM((NSA_GH, tq, tq), F32),
                        pltpu.VMEM((NSA_GH, tq, tq), F32)] + _nsa_scratch(tq),
        compiler_params=pltpu.CompilerParams(
            dimension_semantics=("parallel", "parallel"), vmem_limit_bytes=VMEM_LIMIT),
        name="nsa_slc",
    )(at, an, at, sel, block_of_key)


def _nsa_win(at, an, ub, ocmp, oslc):
    t_len = at.shape[1]
    tq = NSA_TQ
    assert NSA_WINDOW == 2 * tq
    g_n, dh = NSA_KV_GROUPS, NSA_HEAD_DIM
    gw = NSA_GH * dh
    return pl.pallas_call(
        _nsa_win_kernel,
        out_shape=jax.ShapeDtypeStruct((t_len, MIX_W), BF16),
        grid=(g_n, t_len // tq),
        in_specs=[
            pl.BlockSpec((gw, tq), lambda g, i: (g, i)),
            pl.BlockSpec((t_len, dh), lambda g, i: (0, OD_AN_KWIN + g)),
            pl.BlockSpec((dh, t_len), lambda g, i: (OD_AT_VWIN + g, 0)),
            pl.BlockSpec((gw, tq), lambda g, i: (g, i)),
            pl.BlockSpec((gw, tq), lambda g, i: (g, i)),
            pl.BlockSpec((tq, LANES), lambda g, i: (i, OD_B_GATES // LANES + g)),
            pl.BlockSpec((tq, gw), lambda g, i: (i, OD_B_GP // gw + g)),
        ],
        out_specs=pl.BlockSpec((tq, gw), lambda g, i: (i, g)),
        scratch_shapes=_nsa_scratch(tq),
        compiler_params=pltpu.CompilerParams(
            dimension_semantics=("parallel", "parallel"), vmem_limit_bytes=VMEM_LIMIT),
        name="nsa_win",
    )(at, an, at, ocmp, oslc, ub, ub)


def _nsa_overlap_t(t_len):
    nc = t_len // NSA_CMP_STRIDE
    n_cmp = (t_len - NSA_CMP_BLOCK) // NSA_CMP_STRIDE + 1
    cs = np.arange(n_cmp) * NSA_CMP_STRIDE
    ss = np.arange(t_len // NSA_SLC_BLOCK) * NSA_SLC_BLOCK
    ov = np.clip(np.minimum(cs[:, None] + NSA_CMP_BLOCK, ss[None] + NSA_SLC_BLOCK)
                 - np.maximum(cs[:, None], ss[None]), 0, None) / NSA_CMP_BLOCK
    ov = np.concatenate([ov, np.zeros((nc - n_cmp, ov.shape[1]))], axis=0)
    return jnp.asarray(ov.T, dtype=BF16)


def _nsa_mixer(at, an, ub, cmp_pe, cmp_w1, cmp_w2):
    t_len = an.shape[0]
    kc, vct = _nsa_compress(ub, cmp_pe, cmp_w1, cmp_w2)
    ocmp, sel = _nsa_cmp(at, kc, vct, _nsa_overlap_t(t_len))
    oslc = _nsa_slc(at, an, sel)
    return _nsa_win(at, an, ub, ocmp, oslc)


CONV_TAIL = 8


def _cumsum_rows(x):
    n = x.shape[0]
    row = lax.broadcasted_iota(jnp.int32, x.shape, 0)
    s = 1
    while s < n:
        x = x + jnp.where(row >= s, pltpu.roll(x, s, axis=0), 0.0)
        s *= 2
    return x


def _split_bf16(v, parts):
    out = []
    for _ in range(parts - 1):
        p = v.astype(BF16)
        out.append(p)
        v = v - p.astype(F32)
    out.append(v.astype(BF16))
    return out


def _expand_lanes(v, onehot):
    return sum(jnp.dot(p, onehot, preferred_element_type=F32) for p in _split_bf16(v, 3))


def _conv_silu(x, tail, w, b):
    row = lax.broadcasted_iota(jnp.int32, (CONV_TAIL, 1), 0)
    y = b + w[SSD_CONV - 1:SSD_CONV, :] * x
    for s in range(1, SSD_CONV):
        xs = pltpu.roll(x, s, axis=0)
        head = jnp.where(row < s, pltpu.roll(tail, s, axis=0), xs[:CONV_TAIL, :])
        y = y + w[SSD_CONV - 1 - s:SSD_CONV - s, :] * jnp.concatenate([head, xs[CONV_TAIL:, :]], axis=0)
    return y * jax.nn.sigmoid(y)


def _softplus(x):
    return jnp.maximum(x, 0.0) + jnp.log1p(jnp.exp(-jnp.abs(x)))


_TN = (((0,), (0,)), ((), ()))


def _ssd_kernel(z_ref, x_ref, b_ref, c_ref, dt_ref, cwx_ref, cwb_ref, cwc_ref, cbx_ref, cbb_ref, cbc_ref,
                dtb_ref, alog_ref, dsk_ref, nw_ref, e_ref, o_ref, tail_ref, state_ref, y_ref):
    L = SSD_CHUNK
    n, gw = SSD_STATE, (SSD_HEADS // SSD_GROUPS) * SSD_HEAD_DIM

    @pl.when(pl.program_id(0) == 0)
    def _():
        tail_ref[...] = jnp.zeros(tail_ref.shape, F32)
        state_ref[...] = jnp.zeros(state_ref.shape, F32)

    xr, br, cr = x_ref[...], b_ref[...], c_ref[...]
    x = _conv_silu(xr, tail_ref[:, :MIX_W], cwx_ref[...], cbx_ref[...])
    bm = _conv_silu(br, tail_ref[:, MIX_W:MIX_W + SSD_GN], cwb_ref[...], cbb_ref[...])
    cm = _conv_silu(cr, tail_ref[:, MIX_W + SSD_GN:], cwc_ref[...], cbc_ref[...])
    tail_ref[:, :MIX_W] = xr[L - CONV_TAIL:, :]
    tail_ref[:, MIX_W:MIX_W + SSD_GN] = br[L - CONV_TAIL:, :]
    tail_ref[:, MIX_W + SSD_GN:] = cr[L - CONV_TAIL:, :]

    dt = _softplus(dt_ref[...] + dtb_ref[...])
    cum = _cumsum_rows(dt * -jnp.exp(alog_ref[...]))
    cum_t = jnp.transpose(cum)
    onehot = e_ref[...]
    cum_e = _expand_lanes(cum, onehot)
    xdt = x * _expand_lanes(dt, onehot)
    last_e = cum_e[L - 1:L, :]
    xdt16 = xdt.astype(BF16)
    xw16 = (xdt * jnp.exp(last_e - cum_e)).astype(BF16)
    ecum_e = jnp.exp(cum_e)
    causal = lax.broadcasted_iota(jnp.int32, (L, L), 0) >= lax.broadcasted_iota(jnp.int32, (L, L), 1)
    lane = lax.broadcasted_iota(jnp.int32, (1, gw), 1)
    for g in range(SSD_GROUPS):
        cols = slice(g * gw, (g + 1) * gw)
        bg = bm[:, g * n:(g + 1) * n]
        cg16 = cm[:, g * n:(g + 1) * n].astype(BF16)
        sc = lax.dot_general(cg16, bg.astype(BF16), _NT, preferred_element_type=F32)
        state = state_ref[g]
        yg = jnp.dot(cg16, state.astype(BF16), preferred_element_type=F32) * ecum_e[:, cols]
        xg = xdt16[:, cols]
        for r in range(SSD_HEADS // SSD_GROUPS):
            h = g * (SSD_HEADS // SSD_GROUPS) + r
            decay = jnp.where(causal, jnp.exp(cum[:, h:h + 1] - cum_t[h:h + 1, :]), 0.0)
            in_head = (lane >= r * SSD_HEAD_DIM) & (lane < (r + 1) * SSD_HEAD_DIM)
            yg += jnp.dot((sc * decay).astype(BF16), jnp.where(in_head, xg, jnp.zeros_like(xg)),
                          preferred_element_type=F32)
        y_ref[:, cols] = yg
        bg_t = jnp.transpose(bg).astype(BF16)
        state_ref[g] = state * jnp.exp(last_e[:, cols]) + jnp.dot(bg_t, xw16[:, cols], preferred_element_type=F32)

    y = y_ref[...] + x * dsk_ref[...]
    zz = z_ref[...]
    y = y * (zz * jax.nn.sigmoid(zz))
    ms = jnp.mean(y * y, axis=-1, keepdims=True)
    o_ref[...] = (y * lax.rsqrt(ms + EPS) * nw_ref[...]).astype(o_ref.dtype)


def _ssd_mixer(u, conv_w, conv_b, dt_bias, a_log, d_skip, norm_w):
    t_len = u.shape[0]
    L = SSD_CHUNK
    pad = lambda v: jnp.pad(v, (0, LANES - SSD_HEADS)).reshape(1, LANES)
    onehot = np.zeros((LANES, MIX_W), np.float32)
    onehot[np.arange(MIX_W) // SSD_HEAD_DIM, np.arange(MIX_W)] = 1.0
    row = lambda width, start: pl.BlockSpec((L, width), lambda c: (c, start // width))
    fixed = lambda rows, width, start: pl.BlockSpec((rows, width), lambda c: (0, start // width))
    return pl.pallas_call(
        _ssd_kernel,
        out_shape=jax.ShapeDtypeStruct((t_len, MIX_W), BF16),
        grid=(t_len // L,),
        in_specs=[
            row(MIX_W, EV_Z), row(MIX_W, EV_X), row(SSD_GN, EV_B), row(SSD_GN, EV_C), row(LANES, EV_DT),
            fixed(SSD_CONV, MIX_W, 0), fixed(SSD_CONV, SSD_GN, MIX_W), fixed(SSD_CONV, SSD_GN, MIX_W + SSD_GN),
            fixed(1, MIX_W, 0), fixed(1, SSD_GN, MIX_W), fixed(1, SSD_GN, MIX_W + SSD_GN),
            fixed(1, LANES, 0), fixed(1, LANES, 0), fixed(1, MIX_W, 0), fixed(1, MIX_W, 0),
            fixed(LANES, MIX_W, 0),
        ],
        out_specs=pl.BlockSpec((L, MIX_W), lambda c: (c, 0)),
        scratch_shapes=[pltpu.VMEM((CONV_TAIL, SSD_XBC), F32),
                        pltpu.VMEM((SSD_GROUPS, SSD_STATE, MIX_W // SSD_GROUPS), F32),
                        pltpu.VMEM((L, MIX_W), F32)],
        compiler_params=pltpu.CompilerParams(dimension_semantics=("arbitrary",), vmem_limit_bytes=VMEM_LIMIT),
        name="ssd_mixer",
    )(u, u, u, u, u, conv_w, conv_w, conv_w, conv_b.reshape(1, -1), conv_b.reshape(1, -1), conv_b.reshape(1, -1),
      pad(dt_bias), pad(a_log), jnp.repeat(d_skip, SSD_HEAD_DIM).reshape(1, MIX_W), norm_w.reshape(1, MIX_W),
      jnp.asarray(onehot, dtype=BF16))


def _gla_kernel(q_ref, k_ref, v_ref, r_ref, glr_ref, wg_ref, bg_ref, nw_ref, o_ref, state_ref):
    L, dk, dv = GLA_CHUNK, GLA_DK, GLA_DV

    @pl.when(pl.program_id(0) == 0)
    def _():
        state_ref[...] = jnp.zeros(state_ref.shape, F32)

    g_hi, g_lo = _split_bf16(glr_ref[...], 2)
    w_hi, w_lo = _split_bf16(wg_ref[...], 2)
    logits = (jnp.dot(g_hi, w_hi, preferred_element_type=F32) + jnp.dot(g_hi, w_lo, preferred_element_type=F32)
              + jnp.dot(g_lo, w_hi, preferred_element_type=F32) + bg_ref[...])
    cum = _cumsum_rows(-_softplus(-logits) * (1.0 / GLA_TAU))
    tot = cum[L - 1:L, :]
    half = 0.5 * tot
    q = q_ref[...] * dk ** -0.5
    k = k_ref[...]
    qe = (q * jnp.exp(cum - half)).astype(BF16)
    ke = (k * jnp.exp(half - cum)).astype(BF16)
    qd = (q * jnp.exp(cum)).astype(BF16)
    kd = k * jnp.exp(tot - cum)
    etot = jnp.exp(tot)
    causal = lax.broadcasted_iota(jnp.int32, (L, L), 0) >= lax.broadcasted_iota(jnp.int32, (L, L), 1)
    for h in range(GLA_HEADS):
        ks = slice(h * dk, (h + 1) * dk)
        vs = slice(h * dv, (h + 1) * dv)
        att = lax.dot_general(qe[:, ks], ke[:, ks], _NT, preferred_element_type=F32)
        att = jnp.where(causal, att, 0.0).astype(BF16)
        vh = v_ref[:, vs].astype(BF16)
        state = state_ref[h]
        y = jnp.dot(att, vh, preferred_element_type=F32) + jnp.dot(qd[:, ks], state.astype(BF16),
                                                                   preferred_element_type=F32)
        kd_t = jnp.transpose(kd[:, ks]).astype(BF16)
        decay_col = jnp.transpose(jnp.broadcast_to(etot[:, ks], (8, dk)))[:, 0:1]
        state_ref[h] = state * decay_col + jnp.dot(kd_t, vh, preferred_element_type=F32)
        ms = jnp.mean(y * y, axis=-1, keepdims=True)
        r = r_ref[:, vs]
        o_ref[:, vs] = (y * lax.rsqrt(ms + EPS) * nw_ref[...] * (r * jax.nn.sigmoid(r))).astype(o_ref.dtype)


def _gla_mixer(u, w_gate, b_gate, norm_w):
    t_len = u.shape[0]
    L = GLA_CHUNK
    row = lambda width, start: pl.BlockSpec((L, width), lambda c: (c, start // width))
    full = lambda a: pl.BlockSpec(a.shape, lambda c: (0, 0))
    wg = jnp.pad(w_gate, ((0, LANES - GLA_RANK), (0, 0)))
    bg = b_gate.reshape(1, GLA_DKT)
    nw = norm_w.reshape(1, GLA_DV)
    return pl.pallas_call(
        _gla_kernel,
        out_shape=jax.ShapeDtypeStruct((t_len, MIX_W), BF16),
        grid=(t_len // L,),
        in_specs=[row(GLA_DKT, EV_Q), row(GLA_DKT, EV_K), row(MIX_W, EV_V), row(MIX_W, EV_R), row(LANES, EV_GLR),
                  full(wg), full(bg), full(nw)],
        out_specs=pl.BlockSpec((L, MIX_W), lambda c: (c, 0)),
        scratch_shapes=[pltpu.VMEM((GLA_HEADS, GLA_DK, GLA_DV), F32)],
        compiler_params=pltpu.CompilerParams(dimension_semantics=("arbitrary",), vmem_limit_bytes=VMEM_LIMIT),
        name="gla_mixer",
    )(u, u, u, u, u, wg, bg, nw)


def _unit_lower_inverses(mats):
    n = mats[0].shape[0]
    eye = (lax.broadcasted_iota(jnp.int32, (n, n), 0) == lax.broadcasted_iota(jnp.int32, (n, n), 1)).astype(F32)
    ps = [eye - a for a in mats]
    xs = list(mats)
    for _ in range(int(math.log2(n)) - 1):
        x16 = [x.astype(BF16) for x in xs]
        xs = [jnp.dot(x, x, preferred_element_type=F32) for x in x16]
        ps = [p + jnp.dot(p.astype(BF16), x.astype(BF16), preferred_element_type=F32) for p, x in zip(ps, xs)]
    return ps


def _gdn_kernel(q_ref, k_ref, v_ref, z_ref, ab_ref, cwq_ref, cwk_ref, cwv_ref, cbq_ref, cbk_ref, cbv_ref,
                alog_ref, dtb_ref, nw_ref, o_ref, tail_ref, state_ref):
    L, dk, dv = GDN_CHUNK, GDN_DK, GDN_DV
    rep = GDN_V_HEADS // GDN_QK_HEADS

    @pl.when(pl.program_id(0) == 0)
    def _():
        tail_ref[...] = jnp.zeros(tail_ref.shape, F32)
        state_ref[...] = jnp.zeros(state_ref.shape, F32)

    qr, kr, vr = q_ref[...], k_ref[...], v_ref[...]
    q = _conv_silu(qr, tail_ref[:, :GDN_QKW], cwq_ref[...], cbq_ref[...])
    k = _conv_silu(kr, tail_ref[:, GDN_QKW:2 * GDN_QKW], cwk_ref[...], cbk_ref[...])
    v = _conv_silu(vr, tail_ref[:, 2 * GDN_QKW:], cwv_ref[...], cbv_ref[...])
    tail_ref[:, :GDN_QKW] = qr[L - CONV_TAIL:, :]
    tail_ref[:, GDN_QKW:2 * GDN_QKW] = kr[L - CONV_TAIL:, :]
    tail_ref[:, 2 * GDN_QKW:] = vr[L - CONV_TAIL:, :]

    ab = ab_ref[...]
    beta = jax.nn.sigmoid(ab)
    gc = _cumsum_rows(-jnp.exp(alog_ref[...]) * _softplus(ab + dtb_ref[...]))
    gc_t = jnp.transpose(gc)
    egc = jnp.exp(gc)
    last = gc[L - 1:L, :]
    e_end = jnp.exp(last - gc)
    e_last = jnp.exp(last)
    rows = lax.broadcasted_iota(jnp.int32, (L, L), 0)
    cols = lax.broadcasted_iota(jnp.int32, (L, L), 1)
    heads = range(GDN_V_HEADS)
    qn, kn, kk, qk = [], [], [], []
    for j in range(GDN_QK_HEADS):
        qj = q[:, j * dk:(j + 1) * dk]
        kj = k[:, j * dk:(j + 1) * dk]
        qn.append(qj * (lax.rsqrt(jnp.sum(qj * qj, axis=-1, keepdims=True) + EPS) * dk ** -0.5))
        kn.append(kj * lax.rsqrt(jnp.sum(kj * kj, axis=-1, keepdims=True) + EPS))
        kn16 = kn[j].astype(BF16)
        kk.append(lax.dot_general(kn16, kn16, _NT, preferred_element_type=F32))
        qk.append(lax.dot_general(qn[j].astype(BF16), kn16, _NT, preferred_element_type=F32))
    decay = [jnp.where(rows >= cols, jnp.exp(gc[:, h:h + 1] - gc_t[h:h + 1, :]), 0.0) for h in heads]
    b_col = [beta[:, GDN_V_HEADS + h:GDN_V_HEADS + h + 1] for h in heads]
    t_mat = _unit_lower_inverses([jnp.where(rows > cols, b_col[h] * kk[h // rep] * decay[h], 0.0) for h in heads])
    rhs = [jnp.concatenate([v[:, h * dv:(h + 1) * dv] * b_col[h], kn[h // rep] * (b_col[h] * egc[:, h:h + 1])],
                           axis=1).astype(BF16) for h in heads]
    uw = [jnp.dot(t_mat[h].astype(BF16), rhs[h], preferred_element_type=F32) for h in heads]
    state = [state_ref[h] for h in heads]
    s16 = [s.astype(BF16) for s in state]
    vn16 = [(uw[h][:, :dv] - jnp.dot(uw[h][:, dv:].astype(BF16), s16[h], preferred_element_type=F32)).astype(BF16)
            for h in heads]
    out = [jnp.dot((qn[h // rep] * egc[:, h:h + 1]).astype(BF16), s16[h], preferred_element_type=F32)
           + jnp.dot((qk[h // rep] * decay[h]).astype(BF16), vn16[h], preferred_element_type=F32) for h in heads]
    for h in heads:
        kd_t = jnp.transpose(kn[h // rep] * e_end[:, h:h + 1]).astype(BF16)
        state_ref[h] = state[h] * e_last[:, h:h + 1] + jnp.dot(kd_t, vn16[h], preferred_element_type=F32)
    for h in heads:
        o = out[h]
        ms = jnp.mean(o * o, axis=-1, keepdims=True)
        zz = z_ref[:, h * dv:(h + 1) * dv]
        o_ref[:, h * dv:(h + 1) * dv] = (o * lax.rsqrt(ms + EPS) * nw_ref[...]
                                         * (zz * jax.nn.sigmoid(zz))).astype(o_ref.dtype)


def _gdn_mixer(ub, conv_w, conv_b, a_log, dt_bias, norm_w):
    t_len = ub.shape[0]
    L = GDN_CHUNK
    pad = lambda v: jnp.pad(v, (0, LANES - GDN_V_HEADS)).reshape(1, LANES)
    row = lambda width, start: pl.BlockSpec((L, width), lambda c: (c, start // width))
    fixed = lambda rows, width, start: pl.BlockSpec((rows, width), lambda c: (0, start // width))
    cb = conv_b.reshape(1, -1)
    return pl.pallas_call(
        _gdn_kernel,
        out_shape=jax.ShapeDtypeStruct((t_len, MIX_W), BF16),
        grid=(t_len // L,),
        in_specs=[
            row(GDN_QKW, OD_B_CONV), row(GDN_QKW, OD_B_CONV + GDN_QKW), row(MIX_W, OD_B_CONV + 2 * GDN_QKW),
            row(MIX_W, OD_B_Z), row(LANES, OD_B_AB),
            fixed(GDN_CONV, GDN_QKW, 0), fixed(GDN_CONV, GDN_QKW, GDN_QKW), fixed(GDN_CONV, MIX_W, 2 * GDN_QKW),
            fixed(1, GDN_QKW, 0), fixed(1, GDN_QKW, GDN_QKW), fixed(1, MIX_W, 2 * GDN_QKW),
            fixed(1, LANES, 0), fixed(1, LANES, 0), fixed(1, GDN_DV, 0),
        ],
        out_specs=pl.BlockSpec((L, MIX_W), lambda c: (c, 0)),
        scratch_shapes=[pltpu.VMEM((CONV_TAIL, GDN_CONV_CH), F32),
                        pltpu.VMEM((GDN_V_HEADS, GDN_DK, GDN_DV), F32)],
        compiler_params=pltpu.CompilerParams(dimension_semantics=("arbitrary",), vmem_limit_bytes=VMEM_LIMIT),
        name="gdn_mixer",
    )(ub, ub, ub, ub, ub, conv_w, conv_w, conv_w, cb, cb, cb, pad(a_log), pad(dt_bias),
      norm_w.reshape(1, GDN_DV))


def _even_layer(xs, i, ln_w, w_ev, w_out, conv_w, conv_b, dt_bias, a_log, d_skip, ssd_norm_w, w_gate, b_gate,
                gla_norm_w):
    u = _proj(_rmsnorm(xs, ln_w, BF16), w_ev, i, F32)
    ya = _ssd_mixer(u, conv_w, conv_b, dt_bias, a_log, d_skip, ssd_norm_w)
    yb = _gla_mixer(u, w_gate, b_gate, gla_norm_w)
    return _out_proj(xs, ya, yb, w_out, i)


def _odd_layer(xs, i, ln_w, w_at, w_an, w_b, w_out, cmp_pe, cmp_w1, cmp_w2, conv_w, conv_b, a_log, dt_bias,
               gdn_norm_w):
    q_scale = np.ones((OD_AT_W, 1), np.float32)
    q_scale[:MIX_W] = NSA_HEAD_DIM ** -0.5 * math.log2(math.e)
    h = _rmsnorm(xs, ln_w, BF16)
    at = _proj_t(h, w_at, i, jnp.asarray(q_scale), BF16)
    an = _proj(h, w_an, i, BF16)
    ub = _proj(h, w_b, i, F32)
    yc = _nsa_mixer(at, an, ub, cmp_pe, cmp_w1, cmp_w2)
    yd = _gdn_mixer(ub, conv_w, conv_b, a_log, dt_bias, gdn_norm_w)
    return _out_proj(xs, yc, yd, w_out, i)


def kernel(x, ln_w, final_ln_w, ev_w_in, ev_w_out, ssd_conv_w, ssd_conv_b, ssd_dt_bias, ssd_a_log, ssd_d,
           ssd_norm_w, gla_w_gate, gla_b_gate, gla_norm_w, od_w_in, od_w_out, nsa_cmp_pe, nsa_cmp_w1,
           nsa_cmp_w2, gdn_conv_w, gdn_conv_b, gdn_a_log, gdn_dt_bias, gdn_norm_w):
    bsz, t_len, d = x.shape
    assert bsz == 1 and d == D_MODEL
    xs = x.reshape(t_len, d)
    w_ev = _gather_cols(ev_w_in, _EV)
    w_at = jnp.swapaxes(_gather_cols(od_w_in, _OD_AT), 1, 2)
    w_an = _gather_cols(od_w_in, _OD_AN)
    w_b = _gather_cols(od_w_in, _OD_B)
    for layer in range(DEPTH):
        i = layer // 2
        if layer % 2 == 0:
            xs = _even_layer(xs, i, ln_w[layer], w_ev, ev_w_out, ssd_conv_w[i], ssd_conv_b[i], ssd_dt_bias[i],
                             ssd_a_log[i], ssd_d[i], ssd_norm_w[i], gla_w_gate[i], gla_b_gate[i], gla_norm_w[i])
        else:
            xs = _odd_layer(xs, i, ln_w[layer], w_at, w_an, w_b, od_w_out, nsa_cmp_pe[i], nsa_cmp_w1[i],
                            nsa_cmp_w2[i], gdn_conv_w[i], gdn_conv_b[i], gdn_a_log[i], gdn_dt_bias[i], gdn_norm_w[i])
    return _rmsnorm(xs, final_ln_w, F32).reshape(bsz, t_len, d)
```

```python
import functools
import math

import numpy as np
import jax
import jax.numpy as jnp
from jax import lax
from jax.experimental import pallas as pl
from jax.experimental.pallas import tpu as pltpu

D_MODEL = 2048
DEPTH = 4
EPS = 1e-6
NEG = -1e30
MIX_W = D_MODEL

SSD_HEAD_DIM = 64
SSD_HEADS = MIX_W // SSD_HEAD_DIM
SSD_STATE = 128
SSD_GROUPS = 8
SSD_CONV = 4
SSD_CHUNK = 128
SSD_GN = SSD_GROUPS * SSD_STATE
SSD_XBC = MIX_W + 2 * SSD_GN
SSD_IN = MIX_W + SSD_XBC + SSD_HEADS

GLA_HEADS = 4
GLA_DKT = MIX_W // 2
GLA_DK = GLA_DKT // GLA_HEADS
GLA_DV = MIX_W // GLA_HEADS
GLA_RANK = 16
GLA_TAU = 16.0
GLA_CHUNK = 128
GLA_IN = 2 * GLA_DKT + 2 * MIX_W + GLA_RANK

NSA_HEADS = 16
NSA_HEAD_DIM = MIX_W // NSA_HEADS
NSA_KV_GROUPS = 4
NSA_GH = NSA_HEADS // NSA_KV_GROUPS
NSA_KVW = NSA_KV_GROUPS * NSA_HEAD_DIM
NSA_CMP_BLOCK = 32
NSA_CMP_STRIDE = 16
NSA_SLC_BLOCK = 64
NSA_TOPK = 16
NSA_WINDOW = 512
NSA_FORCE_BONUS = 1e3
NSA_IN = 2 * MIX_W + 6 * NSA_KVW + 3 * NSA_HEADS

GDN_QK_HEADS = 8
GDN_V_HEADS = 16
GDN_DK = 128
GDN_DV = MIX_W // GDN_V_HEADS
GDN_CONV = 4
GDN_CHUNK = 64
GDN_QKW = GDN_QK_HEADS * GDN_DK
GDN_CONV_CH = 2 * GDN_QKW + MIX_W
GDN_IN = GDN_CONV_CH + 2 * GDN_V_HEADS + MIX_W

LANES = 128
VMEM_LIMIT = 56 * 1024 * 1024
M_INIT = 0.1 * NEG

BF16 = jnp.bfloat16
F32 = jnp.float32


def _width(pieces):
    return sum(w for _, w in pieces)


def _pack_kernel(pieces, w_ref, o_ref):
    x = w_ref[...]
    cols = [jnp.zeros((x.shape[0], wd), BF16) if s is None else x[:, s:s + wd].astype(BF16) for s, wd in pieces]
    o_ref[...] = jnp.concatenate(cols, axis=1)


def _gather_cols(w, pieces, rows=128):
    layers, d, n_in = w.shape
    width = _width(pieces)
    used = max(s + wd for s, wd in pieces if s is not None)
    n_in = n_in if used > n_in // LANES * LANES else -(-used // LANES) * LANES
    return pl.pallas_call(
        functools.partial(_pack_kernel, pieces),
        out_shape=jax.ShapeDtypeStruct((layers, d, width), BF16),
        grid=(layers, d // rows),
        in_specs=[pl.BlockSpec((None, rows, n_in), lambda l, r: (l, r, 0))],
        out_specs=pl.BlockSpec((None, rows, width), lambda l, r: (l, r, 0)),
        compiler_params=pltpu.CompilerParams(
            dimension_semantics=("parallel", "parallel"), vmem_limit_bytes=VMEM_LIMIT),
        name="pack_weights",
    )(w)


_KV0 = MIX_W
_GATES0 = MIX_W + 6 * NSA_KVW

_OD_AT = [(0, MIX_W), (_KV0 + 3 * NSA_KVW, NSA_KVW), (_KV0 + 5 * NSA_KVW, NSA_KVW)]
OD_AT_W = _width(_OD_AT)
OD_AT_VSLC = MIX_W // NSA_HEAD_DIM
OD_AT_VWIN = OD_AT_VSLC + NSA_KV_GROUPS
_OD_AN = [(_KV0 + 2 * NSA_KVW, NSA_KVW), (_KV0 + 4 * NSA_KVW, NSA_KVW)]
OD_AN_W = _width(_OD_AN)
OD_AN_KSLC, OD_AN_KWIN = 0, NSA_KV_GROUPS

_OD_B = [(NSA_IN, GDN_CONV_CH), (NSA_IN + GDN_CONV_CH + 2 * GDN_V_HEADS, MIX_W), (_GATES0 + 3 * NSA_HEADS, MIX_W),
         (_KV0, 2 * NSA_KVW)]
for _g in range(NSA_KV_GROUPS):
    _OD_B += [(_GATES0 + br * NSA_HEADS + _g * NSA_GH, NSA_GH) for br in range(3)] + [(None, LANES - 3 * NSA_GH)]
_OD_B += [(NSA_IN + GDN_CONV_CH, 2 * GDN_V_HEADS), (None, LANES - 2 * GDN_V_HEADS)]
OD_B_W = _width(_OD_B)
OD_B_CONV = 0
OD_B_Z = OD_B_CONV + GDN_CONV_CH
OD_B_GP = OD_B_Z + MIX_W
OD_B_KCMP = OD_B_GP + MIX_W
OD_B_VCMP = OD_B_KCMP + NSA_KVW
OD_B_GATES = OD_B_VCMP + NSA_KVW
OD_B_AB = OD_B_GATES + NSA_KV_GROUPS * LANES

_GLA_R0 = SSD_IN + 2 * GLA_DKT + MIX_W + GLA_RANK
_EV = [(0, MIX_W + SSD_XBC), (SSD_IN, 2 * GLA_DKT + MIX_W), (_GLA_R0, MIX_W),
       (MIX_W + SSD_XBC, SSD_HEADS), (None, LANES - SSD_HEADS),
       (_GLA_R0 - GLA_RANK, GLA_RANK), (None, LANES - GLA_RANK)]
EV_W = _width(_EV)
EV_Z, EV_X = 0, MIX_W
EV_B = EV_X + MIX_W
EV_C = EV_B + SSD_GN
EV_Q = EV_C + SSD_GN
EV_K = EV_Q + GLA_DKT
EV_V = EV_K + GLA_DKT
EV_R = EV_V + MIX_W
EV_DT = EV_R + MIX_W
EV_GLR = EV_DT + LANES


def _pick_tn(n, cap=2048):
    best = LANES
    for k in range(1, n // LANES + 1):
        if (n // LANES) % k == 0 and k * LANES <= cap:
            best = k * LANES
    return best


def _rmsnorm_kernel(x_ref, g_ref, o_ref):
    x = x_ref[...]
    ms = jnp.mean(x * x, axis=-1, keepdims=True)
    o_ref[...] = (x * lax.rsqrt(ms + EPS) * g_ref[...]).astype(o_ref.dtype)


def _rmsnorm(x, g, out_dtype, tm=512):
    t_len, d = x.shape
    return pl.pallas_call(
        _rmsnorm_kernel,
        out_shape=jax.ShapeDtypeStruct((t_len, d), out_dtype),
        grid=(t_len // tm,),
        in_specs=[pl.BlockSpec((tm, d), lambda i: (i, 0)), pl.BlockSpec((1, d), lambda i: (0, 0))],
        out_specs=pl.BlockSpec((tm, d), lambda i: (i, 0)),
        compiler_params=pltpu.CompilerParams(dimension_semantics=("parallel",), vmem_limit_bytes=VMEM_LIMIT),
        name="rmsnorm",
    )(x, g.reshape(1, d))


def _proj_kernel(h_ref, w_ref, o_ref):
    o_ref[...] = jnp.dot(h_ref[...], w_ref[...], preferred_element_type=F32).astype(o_ref.dtype)


def _proj_t_kernel(h_ref, wt_ref, rs_ref, o_ref):
    acc = lax.dot_general(wt_ref[...], h_ref[...], (((1,), (1,)), ((), ())), preferred_element_type=F32)
    o_ref[...] = (acc * rs_ref[...]).astype(o_ref.dtype)


PROJ_TM = 1024


def _proj(h, w, layer, out_dtype):
    t_len, d = h.shape
    n = w.shape[2]
    tm, tn = min(PROJ_TM, t_len), _pick_tn(n)
    return pl.pallas_call(
        _proj_kernel,
        out_shape=jax.ShapeDtypeStruct((t_len, n), out_dtype),
        grid=(n // tn, t_len // tm),
        in_specs=[pl.BlockSpec((tm, d), lambda j, i: (i, 0)), pl.BlockSpec((None, d, tn), lambda j, i: (layer, 0, j))],
        out_specs=pl.BlockSpec((tm, tn), lambda j, i: (i, j)),
        compiler_params=pltpu.CompilerParams(
            dimension_semantics=("parallel", "parallel"), vmem_limit_bytes=VMEM_LIMIT),
        name="proj",
    )(h, w)


def _proj_t(h, wt, layer, rowscale, out_dtype):
    t_len, d = h.shape
    n = wt.shape[1]
    tm, tn = min(PROJ_TM, t_len), _pick_tn(n)
    return pl.pallas_call(
        _proj_t_kernel,
        out_shape=jax.ShapeDtypeStruct((n, t_len), out_dtype),
        grid=(n // tn, t_len // tm),
        in_specs=[pl.BlockSpec((tm, d), lambda j, i: (i, 0)),
                  pl.BlockSpec((None, tn, d), lambda j, i: (layer, j, 0)),
                  pl.BlockSpec((tn, 1), lambda j, i: (j, 0))],
        out_specs=pl.BlockSpec((tn, tm), lambda j, i: (j, i)),
        compiler_params=pltpu.CompilerParams(
            dimension_semantics=("parallel", "parallel"), vmem_limit_bytes=VMEM_LIMIT),
        name="proj_t",
    )(h, wt, rowscale)


def _out_proj_kernel(x_ref, ya_ref, yb_ref, wa_ref, wb_ref, o_ref, wa16_ref, wb16_ref):
    @pl.when(pl.program_id(1) == 0)
    def _():
        wa16_ref[...] = wa_ref[...].astype(BF16)
        wb16_ref[...] = wb_ref[...].astype(BF16)

    acc = jnp.dot(ya_ref[...], wa16_ref[...], preferred_element_type=F32)
    acc += jnp.dot(yb_ref[...], wb16_ref[...], preferred_element_type=F32)
    o_ref[...] = x_ref[...] + acc


def _out_proj(x, ya, yb, w, layer, tm=1024, tn=512):
    t_len, d = x.shape
    k = ya.shape[1]
    return pl.pallas_call(
        _out_proj_kernel,
        out_shape=jax.ShapeDtypeStruct((t_len, d), F32),
        grid=(d // tn, t_len // tm),
        in_specs=[
            pl.BlockSpec((tm, tn), lambda j, i: (i, j)),
            pl.BlockSpec((tm, k), lambda j, i: (i, 0)),
            pl.BlockSpec((tm, k), lambda j, i: (i, 0)),
            pl.BlockSpec((None, k, tn), lambda j, i: (layer, 0, j)),
            pl.BlockSpec((None, k, tn), lambda j, i: (layer, 1, j)),
        ],
        out_specs=pl.BlockSpec((tm, tn), lambda j, i: (i, j)),
        scratch_shapes=[pltpu.VMEM((k, tn), BF16), pltpu.VMEM((k, tn), BF16)],
        compiler_params=pltpu.CompilerParams(
            dimension_semantics=("parallel", "arbitrary"), vmem_limit_bytes=VMEM_LIMIT),
        name="out_proj",
    )(x, ya, yb, w, w)


NSA_TQ = 256
_NT = (((1,), (1,)), ((), ()))


def _nsa_compress_hidden(u_ref, pe_ref, w1_ref, nc):
    dh = NSA_HEAD_DIM
    acc_a = jnp.zeros((nc, dh), F32)
    acc_b = jnp.zeros((nc, dh), F32)
    for l in range(NSA_CMP_STRIDE):
        x = u_ref[pl.ds(l, nc, stride=NSA_CMP_STRIDE), :]
        xa = (x + pe_ref[l:l + 1, :]).astype(BF16)
        xb = (x + pe_ref[NSA_CMP_STRIDE + l:NSA_CMP_STRIDE + l + 1, :]).astype(BF16)
        acc_a += jnp.dot(xa, w1_ref[l * dh:(l + 1) * dh, :].astype(BF16), preferred_element_type=F32)
        acc_b += jnp.dot(xb, w1_ref[(NSA_CMP_STRIDE + l) * dh:(NSA_CMP_STRIDE + l + 1) * dh, :].astype(BF16),
                         preferred_element_type=F32)
    h = acc_a + pltpu.roll(acc_b, nc - 1, axis=0)
    return (h * jax.nn.sigmoid(h)).astype(BF16)


def _nsa_compress_kernel(uk_ref, uv_ref, pe_ref, w1_ref, w2k_ref, w2vt_ref, kc_ref, vct_ref):
    nc = kc_ref.shape[1]
    hk = _nsa_compress_hidden(uk_ref, pe_ref.at[0], w1_ref.at[0], nc)
    kc_ref[0] = jnp.dot(hk, w2k_ref[...].astype(BF16), preferred_element_type=F32).astype(kc_ref.dtype)
    hv = _nsa_compress_hidden(uv_ref, pe_ref.at[1], w1_ref.at[1], nc)
    vct_ref[0] = lax.dot_general(w2vt_ref[...].astype(BF16), hv, _NT,
                                 preferred_element_type=F32).astype(vct_ref.dtype)


def _nsa_compress(ub, pe, w1, w2):
    t_len = ub.shape[0]
    nc = t_len // NSA_CMP_STRIDE
    g_n, dh = NSA_KV_GROUPS, NSA_HEAD_DIM
    return pl.pallas_call(
        _nsa_compress_kernel,
        out_shape=(jax.ShapeDtypeStruct((g_n, nc, dh), BF16), jax.ShapeDtypeStruct((g_n, dh, nc), BF16)),
        grid=(g_n,),
        in_specs=[
            pl.BlockSpec((t_len, dh), lambda g: (0, OD_B_KCMP // dh + g)),
            pl.BlockSpec((t_len, dh), lambda g: (0, OD_B_VCMP // dh + g)),
            pl.BlockSpec((2, NSA_CMP_BLOCK, dh), lambda g: (0, 0, 0)),
            pl.BlockSpec((2, NSA_CMP_BLOCK * dh, dh), lambda g: (0, 0, 0)),
            pl.BlockSpec((dh, dh), lambda g: (0, 0)),
            pl.BlockSpec((dh, dh), lambda g: (0, 0)),
        ],
        out_specs=(pl.BlockSpec((1, nc, dh), lambda g: (g, 0, 0)), pl.BlockSpec((1, dh, nc), lambda g: (g, 0, 0))),
        compiler_params=pltpu.CompilerParams(dimension_semantics=("parallel",), vmem_limit_bytes=VMEM_LIMIT),
        name="nsa_compress",
    )(ub, ub, pe, w1, w2[0], w2[1].T)


def _nsa_cmp_kernel(qt_ref, kc_ref, vct_ref, ovt_ref, ocmp_ref, sel_ref):
    tq = qt_ref.shape[1]
    nc = kc_ref.shape[1]
    ns = ovt_ref.shape[0]
    dh = NSA_HEAD_DIM
    t = pl.program_id(1) * tq + lax.broadcasted_iota(jnp.int32, (1, tq), 1)
    n_idx = lax.broadcasted_iota(jnp.int32, (nc, 1), 0)
    valid = n_idx * NSA_CMP_STRIDE + (NSA_CMP_BLOCK - 1) <= t
    kc = kc_ref[0]
    vct = vct_ref[0]
    heads = range(NSA_GH)
    s = [jnp.where(valid, jnp.dot(kc, qt_ref[h * dh:(h + 1) * dh, :], preferred_element_type=F32), NEG)
         for h in heads]
    m = [jnp.max(s[h], axis=0, keepdims=True) for h in heads]
    p = [jnp.where(valid, jnp.exp2(s[h] - m[h]), 0.0) for h in heads]
    l = [jnp.sum(p[h], axis=0, keepdims=True) for h in heads]
    p = [p[h] * (1.0 / jnp.where(l[h] > 0.0, l[h], 1.0)) for h in heads]
    for h in heads:
        ocmp_ref[h * dh:(h + 1) * dh, :] = jnp.dot(vct, p[h].astype(BF16), preferred_element_type=F32)
    psum = sum(p[1:], p[0])
    hi = psum.astype(BF16)
    lo = (psum - hi.astype(F32)).astype(BF16)
    ovt = ovt_ref[...]
    imp = jnp.dot(ovt, hi, preferred_element_type=F32) + jnp.dot(ovt, lo, preferred_element_type=F32)
    j = lax.broadcasted_iota(jnp.int32, (ns, 1), 0)
    cur = lax.shift_right_logical(t, int(math.log2(NSA_SLC_BLOCK)))
    forced = (j == 0) | (j == cur) | (j == cur - 1)
    ok = j * NSA_SLC_BLOCK <= t
    imp = jnp.where(ok, imp + jnp.where(forced, NSA_FORCE_BONUS, 0.0), NEG)
    jf = j.astype(F32)

    def pick(_, left):
        mx = jnp.max(left, axis=0, keepdims=True)
        first = jnp.min(jnp.where(left == mx, jf, float(ns)), axis=0, keepdims=True)
        return jnp.where(jf == first, -jnp.inf, left)

    left = lax.fori_loop(0, min(NSA_TOPK, ns), pick, imp)
    sel_ref[0] = jnp.where((left == -jnp.inf) & ok, 1.0, 0.0).astype(sel_ref.dtype)


def _nsa_cmp(at, kc, vct, overlap_t):
    t_len = at.shape[1]
    tq = min(NSA_TQ, t_len)
    ns, nc = overlap_t.shape
    g_n, dh = NSA_KV_GROUPS, NSA_HEAD_DIM
    gw = NSA_GH * dh
    return pl.pallas_call(
        _nsa_cmp_kernel,
        out_shape=(jax.ShapeDtypeStruct((MIX_W, t_len), F32), jax.ShapeDtypeStruct((g_n, ns, t_len), BF16)),
        grid=(g_n, t_len // tq),
        in_specs=[
            pl.BlockSpec((gw, tq), lambda g, i: (g, i)),
            pl.BlockSpec((1, nc, dh), lambda g, i: (g, 0, 0)),
            pl.BlockSpec((1, dh, nc), lambda g, i: (g, 0, 0)),
            pl.BlockSpec((ns, nc), lambda g, i: (0, 0)),
        ],
        out_specs=(pl.BlockSpec((gw, tq), lambda g, i: (g, i)), pl.BlockSpec((1, ns, tq), lambda g, i: (g, 0, i))),
        compiler_params=pltpu.CompilerParams(
            dimension_semantics=("parallel", "parallel"), vmem_limit_bytes=VMEM_LIMIT),
        name="nsa_cmp",
    )(at, kc, vct, overlap_t)


def _flash_init(m_ref, l_ref, acc_ref):
    m_ref[...] = jnp.full(m_ref.shape, M_INIT, F32)
    l_ref[...] = jnp.zeros(l_ref.shape, F32)
    acc_ref[...] = jnp.zeros(acc_ref.shape, F32)


def _flash_tile(q_of, k_tile, vt_tile, mask, m_ref, l_ref, acc_ref):
    _flash_consume([jnp.dot(k_tile, q_of(h), preferred_element_type=F32) for h in range(NSA_GH)], vt_tile, mask,
                   m_ref, l_ref, acc_ref)


def _flash_consume(s, vt_tile, mask, m_ref, l_ref, acc_ref):
    heads = range(NSA_GH)
    if mask is not None:
        s = [jnp.where(mask, s[h], NEG) for h in heads]
    m_old = [m_ref[h] for h in heads]
    m_new = [jnp.maximum(m_old[h], jnp.max(s[h], axis=0, keepdims=True)) for h in heads]
    alpha = [jnp.exp2(m_old[h] - m_new[h]) for h in heads]
    p = [jnp.exp2(s[h] - m_new[h]) for h in heads]
    for h in heads:
        l_ref[h] = alpha[h] * l_ref[h] + jnp.sum(p[h], axis=0, keepdims=True)
        m_ref[h] = m_new[h]
    pv = [jnp.dot(vt_tile, p[h].astype(BF16), preferred_element_type=F32) for h in heads]
    for h in heads:
        acc_ref[h] = alpha[h] * acc_ref[h] + pv[h]


def _nsa_slc_kernel(qt_ref, k_ref, vt_ref, sel_ref, eb_ref, o_ref, qa_ref, s0_ref, s1_ref, m_ref, l_ref, acc_ref):
    tq = qt_ref.shape[1]
    dh = NSA_HEAD_DIM
    i = pl.program_id(1)
    t = i * tq + lax.broadcasted_iota(jnp.int32, (1, tq), 1)
    c_idx = lax.broadcasted_iota(jnp.int32, (tq, 1), 0)
    bias = ((sel_ref[0].astype(F32) - 1.0) * -NEG).astype(BF16)
    for h in range(NSA_GH):
        qa_ref[h, :dh, :] = qt_ref[h * dh:(h + 1) * dh, :]
        qa_ref[h, dh:, :] = bias
    _flash_init(m_ref, l_ref, acc_ref)

    def scores(jt, s_ref):
        k0 = pl.multiple_of(jt * tq, tq)
        k_aug = jnp.concatenate([k_ref[pl.ds(k0, tq), :], eb_ref[pl.ds(k0, tq), :]], axis=1)
        for h in range(NSA_GH):
            s_ref[h] = jnp.dot(k_aug, qa_ref[h], preferred_element_type=F32)

    def consume(jt, s_ref, causal):
        k0 = pl.multiple_of(jt * tq, tq)
        mask = (c_idx + k0 <= t) if causal else None
        _flash_consume([s_ref[h] for h in range(NSA_GH)], vt_ref[:, pl.ds(k0, tq)], mask, m_ref, l_ref, acc_ref)

    def pair(p, carry):
        scores(2 * p + 1, s1_ref)
        consume(2 * p, s0_ref, False)
        scores(2 * p + 2, s0_ref)
        consume(2 * p + 1, s1_ref, False)
        return carry

    scores(0, s0_ref)
    lax.fori_loop(0, i // 2, pair, 0)

    @pl.when(i % 2 == 0)
    def _():
        consume(i, s0_ref, True)

    @pl.when(i % 2 == 1)
    def _():
        scores(i, s1_ref)
        consume(i - 1, s0_ref, False)
        consume(i, s1_ref, True)

    for h in range(NSA_GH):
        o_ref[h * dh:(h + 1) * dh, :] = acc_ref[h] * (1.0 / l_ref[h])


def _nsa_win_kernel(qt_ref, k_ref, vt_ref, ocmp_ref, oslc_ref, gates_ref, gp_ref, o_ref, m_ref, l_ref, acc_ref):
    tq = qt_ref.shape[1]
    dh = NSA_HEAD_DIM
    i = pl.program_id(1)
    t = i * tq + lax.broadcasted_iota(jnp.int32, (1, tq), 1)
    c_idx = lax.broadcasted_iota(jnp.int32, (tq, 1), 0)
    _flash_init(m_ref, l_ref, acc_ref)

    def scores(jt):
        k_tile = k_ref[pl.ds(pl.multiple_of(jt * tq, tq), tq), :]
        return [jnp.dot(k_tile, qt_ref[h * dh:(h + 1) * dh, :], preferred_element_type=F32) for h in range(NSA_GH)]

    def consume(jt, s, mask_of):
        k0 = pl.multiple_of(jt * tq, tq)
        mask = None if mask_of is None else mask_of(c_idx + k0)
        _flash_consume(s, vt_ref[:, pl.ds(k0, tq)], mask, m_ref, l_ref, acc_ref)

    edge = lambda key: key > t - NSA_WINDOW
    causal = lambda key: key <= t

    @pl.when(i >= 2)
    def _():
        s_old, s_mid, s_new = scores(i - 2), scores(i - 1), scores(i)
        consume(i - 2, s_old, edge)
        consume(i - 1, s_mid, None)
        consume(i, s_new, causal)

    @pl.when(i == 1)
    def _():
        s_mid, s_new = scores(0), scores(1)
        consume(0, s_mid, None)
        consume(1, s_new, causal)

    @pl.when(i == 0)
    def _():
        consume(0, scores(0), causal)
    gates_t = jnp.transpose(jax.nn.sigmoid(gates_ref[...]))
    for h in range(NSA_GH):
        rows = slice(h * dh, (h + 1) * dh)
        o_t = (gates_t[h:h + 1, :] * ocmp_ref[rows, :]
               + gates_t[NSA_GH + h:NSA_GH + h + 1, :] * oslc_ref[rows, :]
               + gates_t[2 * NSA_GH + h:2 * NSA_GH + h + 1, :] * (acc_ref[h] * (1.0 / l_ref[h])))
        gp = gp_ref[:, rows]
        o_ref[:, rows] = (jnp.transpose(o_t) * (gp * jax.nn.sigmoid(gp))).astype(o_ref.dtype)


def _nsa_scratch(tq):
    return [pltpu.VMEM((NSA_GH, 1, tq), F32), pltpu.VMEM((NSA_GH, 1, tq), F32),
            pltpu.VMEM((NSA_GH, NSA_HEAD_DIM, tq), F32)]


def _nsa_slc(at, an, sel):
    t_len = at.shape[1]
    tq = NSA_TQ
    ns = sel.shape[1]
    g_n, dh = NSA_KV_GROUPS, NSA_HEAD_DIM
    gw = NSA_GH * dh
    block_of_key = jnp.asarray(np.arange(t_len)[:, None] // NSA_SLC_BLOCK == np.arange(ns)[None], dtype=BF16)
    return pl.pallas_call(
        _nsa_slc_kernel,
        out_shape=jax.ShapeDtypeStruct((MIX_W, t_len), F32),
        grid=(g_n, t_len // tq),
        in_specs=[
            pl.BlockSpec((gw, tq), lambda g, i: (g, i)),
            pl.BlockSpec((t_len, dh), lambda g, i: (0, OD_AN_KSLC + g)),
            pl.BlockSpec((dh, t_len), lambda g, i: (OD_AT_VSLC + g, 0)),
            pl.BlockSpec((1, ns, tq), lambda g, i: (g, 0, i)),
            pl.BlockSpec((t_len, ns), lambda g, i: (0, 0)),
        ],
        out_specs=pl.BlockSpec((gw, tq), lambda g, i: (g, i)),
        scratch_shapes=[pltpu.VMEM((NSA_GH, dh + ns, tq), BF16), pltpu.VMEM((NSA_GH, tq, tq), F32),
                        pltpu.VMEM((NSA_GH, tq, tq), F32)] + _nsa_scratch(tq),
        compiler_params=pltpu.CompilerParams(
            dimension_semantics=("parallel", "parallel"), vmem_limit_bytes=VMEM_LIMIT),
        name="nsa_slc",
    )(at, an, at, sel, block_of_key)


def _nsa_win(at, an, ub, ocmp, oslc):
    t_len = at.shape[1]
    tq = NSA_TQ
    assert NSA_WINDOW == 2 * tq
    g_n, dh = NSA_KV_GROUPS, NSA_HEAD_DIM
    gw = NSA_GH * dh
    return pl.pallas_call(
        _nsa_win_kernel,
        out_shape=jax.ShapeDtypeStruct((t_len, MIX_W), BF16),
        grid=(g_n, t_len // tq),
        in_specs=[
            pl.BlockSpec((gw, tq), lambda g, i: (g, i)),
            pl.BlockSpec((t_len, dh), lambda g, i: (0, OD_AN_KWIN + g)),
            pl.BlockSpec((dh, t_len), lambda g, i: (OD_AT_VWIN + g, 0)),
            pl.BlockSpec((gw, tq), lambda g, i: (g, i)),
            pl.BlockSpec((gw, tq), lambda g, i: (g, i)),
            pl.BlockSpec((tq, LANES), lambda g, i: (i, OD_B_GATES // LANES + g)),
            pl.BlockSpec((tq, gw), lambda g, i: (i, OD_B_GP // gw + g)),
        ],
        out_specs=pl.BlockSpec((tq, gw), lambda g, i: (i, g)),
        scratch_shapes=_nsa_scratch(tq),
        compiler_params=pltpu.CompilerParams(
            dimension_semantics=("parallel", "parallel"), vmem_limit_bytes=VMEM_LIMIT),
        name="nsa_win",
    )(at, an, at, ocmp, oslc, ub, ub)


def _nsa_overlap_t(t_len):
    nc = t_len // NSA_CMP_STRIDE
    n_cmp = (t_len - NSA_CMP_BLOCK) // NSA_CMP_STRIDE + 1
    cs = np.arange(n_cmp) * NSA_CMP_STRIDE
    ss = np.arange(t_len // NSA_SLC_BLOCK) * NSA_SLC_BLOCK
    ov = np.clip(np.minimum(cs[:, None] + NSA_CMP_BLOCK, ss[None] + NSA_SLC_BLOCK)
                 - np.maximum(cs[:, None], ss[None]), 0, None) / NSA_CMP_BLOCK
    ov = np.concatenate([ov, np.zeros((nc - n_cmp, ov.shape[1]))], axis=0)
    return jnp.asarray(ov.T, dtype=BF16)


def _nsa_mixer(at, an, ub, cmp_pe, cmp_w1, cmp_w2):
    t_len = an.shape[0]
    kc, vct = _nsa_compress(ub, cmp_pe, cmp_w1, cmp_w2)
    ocmp, sel = _nsa_cmp(at, kc, vct, _nsa_overlap_t(t_len))
    oslc = _nsa_slc(at, an, sel)
    return _nsa_win(at, an, ub, ocmp, oslc)


CONV_TAIL = 8


def _cumsum_rows(x):
    n = x.shape[0]
    row = lax.broadcasted_iota(jnp.int32, x.shape, 0)
    s = 1
    while s < n:
        x = x + jnp.where(row >= s, pltpu.roll(x, s, axis=0), 0.0)
        s *= 2
    return x


def _split_bf16(v, parts):
    out = []
    for _ in range(parts - 1):
        p = v.astype(BF16)
        out.append(p)
        v = v - p.astype(F32)
    out.append(v.astype(BF16))
    return out


def _expand_lanes(v, onehot):
    return sum(jnp.dot(p, onehot, preferred_element_type=F32) for p in _split_bf16(v, 3))


def _conv_silu(x, tail, w, b):
    row = lax.broadcasted_iota(jnp.int32, (CONV_TAIL, 1), 0)
    y = b + w[SSD_CONV - 1:SSD_CONV, :] * x
    for s in range(1, SSD_CONV):
        xs = pltpu.roll(x, s, axis=0)
        head = jnp.where(row < s, pltpu.roll(tail, s, axis=0), xs[:CONV_TAIL, :])
        y = y + w[SSD_CONV - 1 - s:SSD_CONV - s, :] * jnp.concatenate([head, xs[CONV_TAIL:, :]], axis=0)
    return y * jax.nn.sigmoid(y)


def _softplus(x):
    return jnp.maximum(x, 0.0) + jnp.log1p(jnp.exp(-jnp.abs(x)))


_TN = (((0,), (0,)), ((), ()))


def _ssd_kernel(z_ref, x_ref, b_ref, c_ref, dt_ref, cwx_ref, cwb_ref, cwc_ref, cbx_ref, cbb_ref, cbc_ref,
                dtb_ref, alog_ref, dsk_ref, nw_ref, e_ref, o_ref, tail_ref, state_ref, y_ref):
    L = SSD_CHUNK
    n, gw = SSD_STATE, (SSD_HEADS // SSD_GROUPS) * SSD_HEAD_DIM

    @pl.when(pl.program_id(0) == 0)
    def _():
        tail_ref[...] = jnp.zeros(tail_ref.shape, F32)
        state_ref[...] = jnp.zeros(state_ref.shape, F32)

    xr, br, cr = x_ref[...], b_ref[...], c_ref[...]
    x = _conv_silu(xr, tail_ref[:, :MIX_W], cwx_ref[...], cbx_ref[...])
    bm = _conv_silu(br, tail_ref[:, MIX_W:MIX_W + SSD_GN], cwb_ref[...], cbb_ref[...])
    cm = _conv_silu(cr, tail_ref[:, MIX_W + SSD_GN:], cwc_ref[...], cbc_ref[...])
    tail_ref[:, :MIX_W] = xr[L - CONV_TAIL:, :]
    tail_ref[:, MIX_W:MIX_W + SSD_GN] = br[L - CONV_TAIL:, :]
    tail_ref[:, MIX_W + SSD_GN:] = cr[L - CONV_TAIL:, :]

    dt = _softplus(dt_ref[...] + dtb_ref[...])
    cum = _cumsum_rows(dt * -jnp.exp(alog_ref[...]))
    cum_t = jnp.transpose(cum)
    onehot = e_ref[...]
    cum_e = _expand_lanes(cum, onehot)
    xdt = x * _expand_lanes(dt, onehot)
    last_e = cum_e[L - 1:L, :]
    xdt16 = xdt.astype(BF16)
    xw16 = (xdt * jnp.exp(last_e - cum_e)).astype(BF16)
    ecum_e = jnp.exp(cum_e)
    causal = lax.broadcasted_iota(jnp.int32, (L, L), 0) >= lax.broadcasted_iota(jnp.int32, (L, L), 1)
    lane = lax.broadcasted_iota(jnp.int32, (1, gw), 1)
    for g in range(SSD_GROUPS):
        cols = slice(g * gw, (g + 1) * gw)
        bg = bm[:, g * n:(g + 1) * n]
        cg16 = cm[:, g * n:(g + 1) * n].astype(BF16)
        sc = lax.dot_general(cg16, bg.astype(BF16), _NT, preferred_element_type=F32)
        state = state_ref[g]
        yg = jnp.dot(cg16, state.astype(BF16), preferred_element_type=F32) * ecum_e[:, cols]
        xg = xdt16[:, cols]
        for r in range(SSD_HEADS // SSD_GROUPS):
            h = g * (SSD_HEADS // SSD_GROUPS) + r
            decay = jnp.where(causal, jnp.exp(cum[:, h:h + 1] - cum_t[h:h + 1, :]), 0.0)
            in_head = (lane >= r * SSD_HEAD_DIM) & (lane < (r + 1) * SSD_HEAD_DIM)
            yg += jnp.dot((sc * decay).astype(BF16), jnp.where(in_head, xg, jnp.zeros_like(xg)),
                          preferred_element_type=F32)
        y_ref[:, cols] = yg
        bg_t = jnp.transpose(bg).astype(BF16)
        state_ref[g] = state * jnp.exp(last_e[:, cols]) + jnp.dot(bg_t, xw16[:, cols], preferred_element_type=F32)

    y = y_ref[...] + x * dsk_ref[...]
    zz = z_ref[...]
    y = y * (zz * jax.nn.sigmoid(zz))
    ms = jnp.mean(y * y, axis=-1, keepdims=True)
    o_ref[...] = (y * lax.rsqrt(ms + EPS) * nw_ref[...]).astype(o_ref.dtype)


def _ssd_mixer(u, conv_w, conv_b, dt_bias, a_log, d_skip, norm_w):
    t_len = u.shape[0]
    L = SSD_CHUNK
    pad = lambda v: jnp.pad(v, (0, LANES - SSD_HEADS)).reshape(1, LANES)
    onehot = np.zeros((LANES, MIX_W), np.float32)
    onehot[np.arange(MIX_W) // SSD_HEAD_DIM, np.arange(MIX_W)] = 1.0
    row = lambda width, start: pl.BlockSpec((L, width), lambda c: (c, start // width))
    fixed = lambda rows, width, start: pl.BlockSpec((rows, width), lambda c: (0, start // width))
    return pl.pallas_call(
        _ssd_kernel,
        out_shape=jax.ShapeDtypeStruct((t_len, MIX_W), BF16),
        grid=(t_len // L,),
        in_specs=[
            row(MIX_W, EV_Z), row(MIX_W, EV_X), row(SSD_GN, EV_B), row(SSD_GN, EV_C), row(LANES, EV_DT),
            fixed(SSD_CONV, MIX_W, 0), fixed(SSD_CONV, SSD_GN, MIX_W), fixed(SSD_CONV, SSD_GN, MIX_W + SSD_GN),
            fixed(1, MIX_W, 0), fixed(1, SSD_GN, MIX_W), fixed(1, SSD_GN, MIX_W + SSD_GN),
            fixed(1, LANES, 0), fixed(1, LANES, 0), fixed(1, MIX_W, 0), fixed(1, MIX_W, 0),
            fixed(LANES, MIX_W, 0),
        ],
        out_specs=pl.BlockSpec((L, MIX_W), lambda c: (c, 0)),
        scratch_shapes=[pltpu.VMEM((CONV_TAIL, SSD_XBC), F32),
                        pltpu.VMEM((SSD_GROUPS, SSD_STATE, MIX_W // SSD_GROUPS), F32),
                        pltpu.VMEM((L, MIX_W), F32)],
        compiler_params=pltpu.CompilerParams(dimension_semantics=("arbitrary",), vmem_limit_bytes=VMEM_LIMIT),
        name="ssd_mixer",
    )(u, u, u, u, u, conv_w, conv_w, conv_w, conv_b.reshape(1, -1), conv_b.reshape(1, -1), conv_b.reshape(1, -1),
      pad(dt_bias), pad(a_log), jnp.repeat(d_skip, SSD_HEAD_DIM).reshape(1, MIX_W), norm_w.reshape(1, MIX_W),
      jnp.asarray(onehot, dtype=BF16))


def _gla_kernel(q_ref, k_ref, v_ref, r_ref, glr_ref, wg_ref, bg_ref, nw_ref, o_ref, state_ref):
    L, dk, dv = GLA_CHUNK, GLA_DK, GLA_DV

    @pl.when(pl.program_id(0) == 0)
    def _():
        state_ref[...] = jnp.zeros(state_ref.shape, F32)

    g_hi, g_lo = _split_bf16(glr_ref[...], 2)
    w_hi, w_lo = _split_bf16(wg_ref[...], 2)
    logits = (jnp.dot(g_hi, w_hi, preferred_element_type=F32) + jnp.dot(g_hi, w_lo, preferred_element_type=F32)
              + jnp.dot(g_lo, w_hi, preferred_element_type=F32) + bg_ref[...])
    cum = _cumsum_rows(-_softplus(-logits) * (1.0 / GLA_TAU))
    tot = cum[L - 1:L, :]
    half = 0.5 * tot
    q = q_ref[...] * dk ** -0.5
    k = k_ref[...]
    qe = (q * jnp.exp(cum - half)).astype(BF16)
    ke = (k * jnp.exp(half - cum)).astype(BF16)
    qd = (q * jnp.exp(cum)).astype(BF16)
    kd = k * jnp.exp(tot - cum)
    etot = jnp.exp(tot)
    causal = lax.broadcasted_iota(jnp.int32, (L, L), 0) >= lax.broadcasted_iota(jnp.int32, (L, L), 1)
    for h in range(GLA_HEADS):
        ks = slice(h * dk, (h + 1) * dk)
        vs = slice(h * dv, (h + 1) * dv)
        att = lax.dot_general(qe[:, ks], ke[:, ks], _NT, preferred_element_type=F32)
        att = jnp.where(causal, att, 0.0).astype(BF16)
        vh = v_ref[:, vs].astype(BF16)
        state = state_ref[h]
        y = jnp.dot(att, vh, preferred_element_type=F32) + jnp.dot(qd[:, ks], state.astype(BF16),
                                                                   preferred_element_type=F32)
        kd_t = jnp.transpose(kd[:, ks]).astype(BF16)
        decay_col = jnp.transpose(jnp.broadcast_to(etot[:, ks], (8, dk)))[:, 0:1]
        state_ref[h] = state * decay_col + jnp.dot(kd_t, vh, preferred_element_type=F32)
        ms = jnp.mean(y * y, axis=-1, keepdims=True)
        r = r_ref[:, vs]
        o_ref[:, vs] = (y * lax.rsqrt(ms + EPS) * nw_ref[...] * (r * jax.nn.sigmoid(r))).astype(o_ref.dtype)


def _gla_mixer(u, w_gate, b_gate, norm_w):
    t_len = u.shape[0]
    L = GLA_CHUNK
    row = lambda width, start: pl.BlockSpec((L, width), lambda c: (c, start // width))
    full = lambda a: pl.BlockSpec(a.shape, lambda c: (0, 0))
    wg = jnp.pad(w_gate, ((0, LANES - GLA_RANK), (0, 0)))
    bg = b_gate.reshape(1, GLA_DKT)
    nw = norm_w.reshape(1, GLA_DV)
    return pl.pallas_call(
        _gla_kernel,
        out_shape=jax.ShapeDtypeStruct((t_len, MIX_W), BF16),
        grid=(t_len // L,),
        in_specs=[row(GLA_DKT, EV_Q), row(GLA_DKT, EV_K), row(MIX_W, EV_V), row(MIX_W, EV_R), row(LANES, EV_GLR),
                  full(wg), full(bg), full(nw)],
        out_specs=pl.BlockSpec((L, MIX_W), lambda c: (c, 0)),
        scratch_shapes=[pltpu.VMEM((GLA_HEADS, GLA_DK, GLA_DV), F32)],
        compiler_params=pltpu.CompilerParams(dimension_semantics=("arbitrary",), vmem_limit_bytes=VMEM_LIMIT),
        name="gla_mixer",
    )(u, u, u, u, u, wg, bg, nw)


def _unit_lower_inverses(mats):
    n = mats[0].shape[0]
    eye = (lax.broadcasted_iota(jnp.int32, (n, n), 0) == lax.broadcasted_iota(jnp.int32, (n, n), 1)).astype(F32)
    ps = [eye - a for a in mats]
    xs = list(mats)
    for _ in range(int(math.log2(n)) - 1):
        x16 = [x.astype(BF16) for x in xs]
        xs = [jnp.dot(x, x, preferred_element_type=F32) for x in x16]
        ps = [p + jnp.dot(p.astype(BF16), x.astype(BF16), preferred_element_type=F32) for p, x in zip(ps, xs)]
    return ps


def _gdn_kernel(q_ref, k_ref, v_ref, z_ref, ab_ref, cwq_ref, cwk_ref, cwv_ref, cbq_ref, cbk_ref, cbv_ref,
                alog_ref, dtb_ref, nw_ref, o_ref, tail_ref, state_ref):
    L, dk, dv = GDN_CHUNK, GDN_DK, GDN_DV
    rep = GDN_V_HEADS // GDN_QK_HEADS

    @pl.when(pl.program_id(0) == 0)
    def _():
        tail_ref[...] = jnp.zeros(tail_ref.shape, F32)
        state_ref[...] = jnp.zeros(state_ref.shape, F32)

    qr, kr, vr = q_ref[...], k_ref[...], v_ref[...]
    q = _conv_silu(qr, tail_ref[:, :GDN_QKW], cwq_ref[...], cbq_ref[...])
    k = _conv_silu(kr, tail_ref[:, GDN_QKW:2 * GDN_QKW], cwk_ref[...], cbk_ref[...])
    v = _conv_silu(vr, tail_ref[:, 2 * GDN_QKW:], cwv_ref[...], cbv_ref[...])
    tail_ref[:, :GDN_QKW] = qr[L - CONV_TAIL:, :]
    tail_ref[:, GDN_QKW:2 * GDN_QKW] = kr[L - CONV_TAIL:, :]
    tail_ref[:, 2 * GDN_QKW:] = vr[L - CONV_TAIL:, :]

    ab = ab_ref[...]
    beta = jax.nn.sigmoid(ab)
    gc = _cumsum_rows(-jnp.exp(alog_ref[...]) * _softplus(ab + dtb_ref[...]))
    gc_t = jnp.transpose(gc)
    egc = jnp.exp(gc)
    last = gc[L - 1:L, :]
    e_end = jnp.exp(last - gc)
    e_last = jnp.exp(last)
    rows = lax.broadcasted_iota(jnp.int32, (L, L), 0)
    cols = lax.broadcasted_iota(jnp.int32, (L, L), 1)
    heads = range(GDN_V_HEADS)
    qn, kn, kk, qk = [], [], [], []
    for j in range(GDN_QK_HEADS):
        qj = q[:, j * dk:(j + 1) * dk]
        kj = k[:, j * dk:(j + 1) * dk]
        qn.append(qj * (lax.rsqrt(jnp.sum(qj * qj, axis=-1, keepdims=True) + EPS) * dk ** -0.5))
        kn.append(kj * lax.rsqrt(jnp.sum(kj * kj, axis=-1, keepdims=True) + EPS))
        kn16 = kn[j].astype(BF16)
        kk.append(lax.dot_general(kn16, kn16, _NT, preferred_element_type=F32))
        qk.append(lax.dot_general(qn[j].astype(BF16), kn16, _NT, preferred_element_type=F32))
    decay = [jnp.where(rows >= cols, jnp.exp(gc[:, h:h + 1] - gc_t[h:h + 1, :]), 0.0) for h in heads]
    b_col = [beta[:, GDN_V_HEADS + h:GDN_V_HEADS + h + 1] for h in heads]
    t_mat = _unit_lower_inverses([jnp.where(rows > cols, b_col[h] * kk[h // rep] * decay[h], 0.0) for h in heads])
    rhs = [jnp.concatenate([v[:, h * dv:(h + 1) * dv] * b_col[h], kn[h // rep] * (b_col[h] * egc[:, h:h + 1])],
                           axis=1).astype(BF16) for h in heads]
    uw = [jnp.dot(t_mat[h].astype(BF16), rhs[h], preferred_element_type=F32) for h in heads]
    state = [state_ref[h] for h in heads]
    s16 = [s.astype(BF16) for s in state]
    vn16 = [(uw[h][:, :dv] - jnp.dot(uw[h][:, dv:].astype(BF16), s16[h], preferred_element_type=F32)).astype(BF16)
            for h in heads]
    out = [jnp.dot((qn[h // rep] * egc[:, h:h + 1]).astype(BF16), s16[h], preferred_element_type=F32)
           + jnp.dot((qk[h // rep] * decay[h]).astype(BF16), vn16[h], preferred_element_type=F32) for h in heads]
    for h in heads:
        kd_t = jnp.transpose(kn[h // rep] * e_end[:, h:h + 1]).astype(BF16)
        state_ref[h] = state[h] * e_last[:, h:h + 1] + jnp.dot(kd_t, vn16[h], preferred_element_type=F32)
    for h in heads:
        o = out[h]
        ms = jnp.mean(o * o, axis=-1, keepdims=True)
        zz = z_ref[:, h * dv:(h + 1) * dv]
        o_ref[:, h * dv:(h + 1) * dv] = (o * lax.rsqrt(ms + EPS) * nw_ref[...]
                                         * (zz * jax.nn.sigmoid(zz))).astype(o_ref.dtype)


def _gdn_mixer(ub, conv_w, conv_b, a_log, dt_bias, norm_w):
    t_len = ub.shape[0]
    L = GDN_CHUNK
    pad = lambda v: jnp.pad(v, (0, LANES - GDN_V_HEADS)).reshape(1, LANES)
    row = lambda width, start: pl.BlockSpec((L, width), lambda c: (c, start // width))
    fixed = lambda rows, width, start: pl.BlockSpec((rows, width), lambda c: (0, start // width))
    cb = conv_b.reshape(1, -1)
    return pl.pallas_call(
        _gdn_kernel,
        out_shape=jax.ShapeDtypeStruct((t_len, MIX_W), BF16),
        grid=(t_len // L,),
        in_specs=[
            row(GDN_QKW, OD_B_CONV), row(GDN_QKW, OD_B_CONV + GDN_QKW), row(MIX_W, OD_B_CONV + 2 * GDN_QKW),
            row(MIX_W, OD_B_Z), row(LANES, OD_B_AB),
            fixed(GDN_CONV, GDN_QKW, 0), fixed(GDN_CONV, GDN_QKW, GDN_QKW), fixed(GDN_CONV, MIX_W, 2 * GDN_QKW),
            fixed(1, GDN_QKW, 0), fixed(1, GDN_QKW, GDN_QKW), fixed(1, MIX_W, 2 * GDN_QKW),
            fixed(1, LANES, 0), fixed(1, LANES, 0), fixed(1, GDN_DV, 0),
        ],
        out_specs=pl.BlockSpec((L, MIX_W), lambda c: (c, 0)),
        scratch_shapes=[pltpu.VMEM((CONV_TAIL, GDN_CONV_CH), F32),
                        pltpu.VMEM((GDN_V_HEADS, GDN_DK, GDN_DV), F32)],
        compiler_params=pltpu.CompilerParams(dimension_semantics=("arbitrary",), vmem_limit_bytes=VMEM_LIMIT),
        name="gdn_mixer",
    )(ub, ub, ub, ub, ub, conv_w, conv_w, conv_w, cb, cb, cb, pad(a_log), pad(dt_bias),
      norm_w.reshape(1, GDN_DV))


def _even_layer(xs, i, ln_w, w_ev, w_out, conv_w, conv_b, dt_bias, a_log, d_skip, ssd_norm_w, w_gate, b_gate,
                gla_norm_w):
    u = _proj(_rmsnorm(xs, ln_w, BF16), w_ev, i, F32)
    ya = _ssd_mixer(u, conv_w, conv_b, dt_bias, a_log, d_skip, ssd_norm_w)
    yb = _gla_mixer(u, w_gate, b_gate, gla_norm_w)
    return _out_proj(xs, ya, yb, w_out, i)


def _odd_layer(xs, i, ln_w, w_at, w_an, w_b, w_out, cmp_pe, cmp_w1, cmp_w2, conv_w, conv_b, a_log, dt_bias,
               gdn_norm_w):
    q_scale = np.ones((OD_AT_W, 1), np.float32)
    q_scale[:MIX_W] = NSA_HEAD_DIM ** -0.5 * math.log2(math.e)
    h = _rmsnorm(xs, ln_w, BF16)
    at = _proj_t(h, w_at, i, jnp.asarray(q_scale), BF16)
    an = _proj(h, w_an, i, BF16)
    ub = _proj(h, w_b, i, F32)
    yc = _nsa_mixer(at, an, ub, cmp_pe, cmp_w1, cmp_w2)
    yd = _gdn_mixer(ub, conv_w, conv_b, a_log, dt_bias, gdn_norm_w)
    return _out_proj(xs, yc, yd, w_out, i)


def kernel(x, ln_w, final_ln_w, ev_w_in, ev_w_out, ssd_conv_w, ssd_conv_b, ssd_dt_bias, ssd_a_log, ssd_d,
           ssd_norm_w, gla_w_gate, gla_b_gate, gla_norm_w, od_w_in, od_w_out, nsa_cmp_pe, nsa_cmp_w1,
           nsa_cmp_w2, gdn_conv_w, gdn_conv_b, gdn_a_log, gdn_dt_bias, gdn_norm_w):
    bsz, t_len, d = x.shape
    assert bsz == 1 and d == D_MODEL
    xs = x.reshape(t_len, d)
    w_ev = _gather_cols(ev_w_in, _EV)
    w_at = jnp.swapaxes(_gather_cols(od_w_in, _OD_AT), 1, 2)
    w_an = _gather_cols(od_w_in, _OD_AN)
    w_b = _gather_cols(od_w_in, _OD_B)
    for layer in range(DEPTH):
        i = layer // 2
        if layer % 2 == 0:
            xs = _even_layer(xs, i, ln_w[layer], w_ev, ev_w_out, ssd_conv_w[i], ssd_conv_b[i], ssd_dt_bias[i],
                             ssd_a_log[i], ssd_d[i], ssd_norm_w[i], gla_w_gate[i], gla_b_gate[i], gla_norm_w[i])
        else:
            xs = _odd_layer(xs, i, ln_w[layer], w_at, w_an, w_b, od_w_out, nsa_cmp_pe[i], nsa_cmp_w1[i],
                            nsa_cmp_w2[i], gdn_conv_w[i], gdn_conv_b[i], gdn_a_log[i], gdn_dt_bias[i], gdn_norm_w[i])
    return _rmsnorm(xs, final_ln_w, F32).reshape(bsz, t_len, d)
```

```python
import functools
import math

import numpy as np
import jax
import jax.numpy as jnp
from jax import lax
from jax.experimental import pallas as pl
from jax.experimental.pallas import tpu as pltpu

D_MODEL = 2048
DEPTH = 4
EPS = 1e-6
NEG = -1e30
MIX_W = D_MODEL

SSD_HEAD_DIM = 64
SSD_HEADS = MIX_W // SSD_HEAD_DIM
SSD_STATE = 128
SSD_GROUPS = 8
SSD_CONV = 4
SSD_CHUNK = 128
SSD_GN = SSD_GROUPS * SSD_STATE
SSD_XBC = MIX_W + 2 * SSD_GN
SSD_IN = MIX_W + SSD_XBC + SSD_HEADS

GLA_HEADS = 4
GLA_DKT = MIX_W // 2
GLA_DK = GLA_DKT // GLA_HEADS
GLA_DV = MIX_W // GLA_HEADS
GLA_RANK = 16
GLA_TAU = 16.0
GLA_CHUNK = 128
GLA_IN = 2 * GLA_DKT + 2 * MIX_W + GLA_RANK

NSA_HEADS = 16
NSA_HEAD_DIM = MIX_W // NSA_HEADS
NSA_KV_GROUPS = 4
NSA_GH = NSA_HEADS // NSA_KV_GROUPS
NSA_KVW = NSA_KV_GROUPS * NSA_HEAD_DIM
NSA_CMP_BLOCK = 32
NSA_CMP_STRIDE = 16
NSA_SLC_BLOCK = 64
NSA_TOPK = 16
NSA_WINDOW = 512
NSA_FORCE_BONUS = 1e3
NSA_IN = 2 * MIX_W + 6 * NSA_KVW + 3 * NSA_HEADS

GDN_QK_HEADS = 8
GDN_V_HEADS = 16
GDN_DK = 128
GDN_DV = MIX_W // GDN_V_HEADS
GDN_CONV = 4
GDN_CHUNK = 64
GDN_QKW = GDN_QK_HEADS * GDN_DK
GDN_CONV_CH = 2 * GDN_QKW + MIX_W
GDN_IN = GDN_CONV_CH + 2 * GDN_V_HEADS + MIX_W

LANES = 128
VMEM_LIMIT = 56 * 1024 * 1024
M_INIT = 0.1 * NEG

BF16 = jnp.bfloat16
F32 = jnp.float32


def _width(pieces):
    return sum(w for _, w in pieces)


SUBLANES = 8


def _pack_kernel(pieces, wt_ref, o_ref):
    x = wt_ref[...]
    rows = [jnp.zeros((wd, x.shape[1]), F32) if s is None else x[s:s + wd, :] for s, wd in pieces]
    o_ref[...] = jnp.concatenate(rows, axis=0).astype(BF16)


def _gather_cols(w, pieces, dc=LANES):
    wt = jnp.swapaxes(w, 1, 2)
    layers, n_in, d = wt.shape
    width = _width(pieces)
    used = max(s + wd for s, wd in pieces if s is not None)
    n_in = min(n_in, -(-used // SUBLANES) * SUBLANES)
    return pl.pallas_call(
        functools.partial(_pack_kernel, pieces),
        out_shape=jax.ShapeDtypeStruct((layers, width, d), BF16),
        grid=(layers, d // dc),
        in_specs=[pl.BlockSpec((None, n_in, dc), lambda l, c: (l, 0, c))],
        out_specs=pl.BlockSpec((None, width, dc), lambda l, c: (l, 0, c)),
        compiler_params=pltpu.CompilerParams(
            dimension_semantics=("parallel", "parallel"), vmem_limit_bytes=VMEM_LIMIT),
        name="pack_weights",
    )(wt)


_KV0 = MIX_W
_GATES0 = MIX_W + 6 * NSA_KVW

_OD_AT = [(0, MIX_W), (_KV0 + 3 * NSA_KVW, NSA_KVW), (_KV0 + 5 * NSA_KVW, NSA_KVW)]
OD_AT_W = _width(_OD_AT)
OD_AT_VSLC = MIX_W // NSA_HEAD_DIM
OD_AT_VWIN = OD_AT_VSLC + NSA_KV_GROUPS
_OD_AN = [(_KV0 + 2 * NSA_KVW, NSA_KVW), (_KV0 + 4 * NSA_KVW, NSA_KVW)]
OD_AN_W = _width(_OD_AN)
OD_AN_KSLC, OD_AN_KWIN = 0, NSA_KV_GROUPS

_OD_B = [(NSA_IN, GDN_CONV_CH), (NSA_IN + GDN_CONV_CH + 2 * GDN_V_HEADS, MIX_W), (_GATES0 + 3 * NSA_HEADS, MIX_W),
         (_KV0, 2 * NSA_KVW)]
for _g in range(NSA_KV_GROUPS):
    _OD_B += [(_GATES0 + br * NSA_HEADS + _g * NSA_GH, NSA_GH) for br in range(3)] + [(None, LANES - 3 * NSA_GH)]
_OD_B += [(NSA_IN + GDN_CONV_CH, 2 * GDN_V_HEADS), (None, LANES - 2 * GDN_V_HEADS)]
OD_B_W = _width(_OD_B)
OD_B_CONV = 0
OD_B_Z = OD_B_CONV + GDN_CONV_CH
OD_B_GP = OD_B_Z + MIX_W
OD_B_KCMP = OD_B_GP + MIX_W
OD_B_VCMP = OD_B_KCMP + NSA_KVW
OD_B_GATES = OD_B_VCMP + NSA_KVW
OD_B_AB = OD_B_GATES + NSA_KV_GROUPS * LANES

_GLA_R0 = SSD_IN + 2 * GLA_DKT + MIX_W + GLA_RANK
_EV = [(0, MIX_W + SSD_XBC), (SSD_IN, 2 * GLA_DKT + MIX_W), (_GLA_R0, MIX_W),
       (MIX_W + SSD_XBC, SSD_HEADS), (None, LANES - SSD_HEADS),
       (_GLA_R0 - GLA_RANK, GLA_RANK), (None, LANES - GLA_RANK)]
EV_W = _width(_EV)
EV_Z, EV_X = 0, MIX_W
EV_B = EV_X + MIX_W
EV_C = EV_B + SSD_GN
EV_Q = EV_C + SSD_GN
EV_K = EV_Q + GLA_DKT
EV_V = EV_K + GLA_DKT
EV_R = EV_V + MIX_W
EV_DT = EV_R + MIX_W
EV_GLR = EV_DT + LANES


def _pick_tn(n, cap=2048):
    best = LANES
    for k in range(1, n // LANES + 1):
        if (n // LANES) % k == 0 and k * LANES <= cap:
            best = k * LANES
    return best


def _rmsnorm_kernel(x_ref, g_ref, o_ref):
    x = x_ref[...]
    ms = jnp.mean(x * x, axis=-1, keepdims=True)
    o_ref[...] = (x * lax.rsqrt(ms + EPS) * g_ref[...]).astype(o_ref.dtype)


def _rmsnorm(x, g, out_dtype, tm=512):
    t_len, d = x.shape
    return pl.pallas_call(
        _rmsnorm_kernel,
        out_shape=jax.ShapeDtypeStruct((t_len, d), out_dtype),
        grid=(t_len // tm,),
        in_specs=[pl.BlockSpec((tm, d), lambda i: (i, 0)), pl.BlockSpec((1, d), lambda i: (0, 0))],
        out_specs=pl.BlockSpec((tm, d), lambda i: (i, 0)),
        compiler_params=pltpu.CompilerParams(dimension_semantics=("parallel",), vmem_limit_bytes=VMEM_LIMIT),
        name="rmsnorm",
    )(x, g.reshape(1, d))


def _proj_kernel(h_ref, wt_ref, o_ref):
    acc = lax.dot_general(h_ref[...], wt_ref[...], (((1,), (1,)), ((), ())), preferred_element_type=F32)
    o_ref[...] = acc.astype(o_ref.dtype)


def _proj_t_kernel(h_ref, wt_ref, rs_ref, o_ref):
    acc = lax.dot_general(wt_ref[...], h_ref[...], (((1,), (1,)), ((), ())), preferred_element_type=F32)
    o_ref[...] = (acc * rs_ref[...]).astype(o_ref.dtype)


PROJ_TM = 1024


def _proj(h, wt, layer, out_dtype):
    t_len, d = h.shape
    n = wt.shape[1]
    tm, tn = min(PROJ_TM, t_len), _pick_tn(n)
    return pl.pallas_call(
        _proj_kernel,
        out_shape=jax.ShapeDtypeStruct((t_len, n), out_dtype),
        grid=(n // tn, t_len // tm),
        in_specs=[pl.BlockSpec((tm, d), lambda j, i: (i, 0)), pl.BlockSpec((None, tn, d), lambda j, i: (layer, j, 0))],
        out_specs=pl.BlockSpec((tm, tn), lambda j, i: (i, j)),
        compiler_params=pltpu.CompilerParams(
            dimension_semantics=("parallel", "parallel"), vmem_limit_bytes=VMEM_LIMIT),
        name="proj",
    )(h, wt)


def _proj_t(h, wt, layer, rowscale, out_dtype):
    t_len, d = h.shape
    n = wt.shape[1]
    tm, tn = min(PROJ_TM, t_len), _pick_tn(n)
    return pl.pallas_call(
        _proj_t_kernel,
        out_shape=jax.ShapeDtypeStruct((n, t_len), out_dtype),
        grid=(n // tn, t_len // tm),
        in_specs=[pl.BlockSpec((tm, d), lambda j, i: (i, 0)),
                  pl.BlockSpec((None, tn, d), lambda j, i: (layer, j, 0)),
                  pl.BlockSpec((tn, 1), lambda j, i: (j, 0))],
        out_specs=pl.BlockSpec((tn, tm), lambda j, i: (j, i)),
        compiler_params=pltpu.CompilerParams(
            dimension_semantics=("parallel", "parallel"), vmem_limit_bytes=VMEM_LIMIT),
        name="proj_t",
    )(h, wt, rowscale)


def _out_proj_kernel(x_ref, ya_ref, yb_ref, wa_ref, wb_ref, o_ref, wa16_ref, wb16_ref):
    @pl.when(pl.program_id(1) == 0)
    def _():
        wa16_ref[...] = wa_ref[...].astype(BF16)
        wb16_ref[...] = wb_ref[...].astype(BF16)

    acc = jnp.dot(ya_ref[...], wa16_ref[...], preferred_element_type=F32)
    acc += jnp.dot(yb_ref[...], wb16_ref[...], preferred_element_type=F32)
    o_ref[...] = x_ref[...] + acc


def _out_proj(x, ya, yb, w, layer, tm=1024, tn=512):
    t_len, d = x.shape
    k = ya.shape[1]
    return pl.pallas_call(
        _out_proj_kernel,
        out_shape=jax.ShapeDtypeStruct((t_len, d), F32),
        grid=(d // tn, t_len // tm),
        in_specs=[
            pl.BlockSpec((tm, tn), lambda j, i: (i, j)),
            pl.BlockSpec((tm, k), lambda j, i: (i, 0)),
            pl.BlockSpec((tm, k), lambda j, i: (i, 0)),
            pl.BlockSpec((None, k, tn), lambda j, i: (layer, 0, j)),
            pl.BlockSpec((None, k, tn), lambda j, i: (layer, 1, j)),
        ],
        out_specs=pl.BlockSpec((tm, tn), lambda j, i: (i, j)),
        scratch_shapes=[pltpu.VMEM((k, tn), BF16), pltpu.VMEM((k, tn), BF16)],
        compiler_params=pltpu.CompilerParams(
            dimension_semantics=("parallel", "arbitrary"), vmem_limit_bytes=VMEM_LIMIT),
        name="out_proj",
    )(x, ya, yb, w, w)


NSA_TQ = 256
_NT = (((1,), (1,)), ((), ()))


def _nsa_compress_hidden(u_ref, pe_ref, w1_ref, nc):
    dh = NSA_HEAD_DIM
    acc_a = jnp.zeros((nc, dh), F32)
    acc_b = jnp.zeros((nc, dh), F32)
    for l in range(NSA_CMP_STRIDE):
        x = u_ref[pl.ds(l, nc, stride=NSA_CMP_STRIDE), :]
        xa = (x + pe_ref[l:l + 1, :]).astype(BF16)
        xb = (x + pe_ref[NSA_CMP_STRIDE + l:NSA_CMP_STRIDE + l + 1, :]).astype(BF16)
        acc_a += jnp.dot(xa, w1_ref[l * dh:(l + 1) * dh, :].astype(BF16), preferred_element_type=F32)
        acc_b += jnp.dot(xb, w1_ref[(NSA_CMP_STRIDE + l) * dh:(NSA_CMP_STRIDE + l + 1) * dh, :].astype(BF16),
                         preferred_element_type=F32)
    h = acc_a + pltpu.roll(acc_b, nc - 1, axis=0)
    return (h * jax.nn.sigmoid(h)).astype(BF16)


def _nsa_compress_kernel(uk_ref, uv_ref, pe_ref, w1_ref, w2k_ref, w2vt_ref, kc_ref, vct_ref):
    nc = kc_ref.shape[1]
    hk = _nsa_compress_hidden(uk_ref, pe_ref.at[0], w1_ref.at[0], nc)
    kc_ref[0] = jnp.dot(hk, w2k_ref[...].astype(BF16), preferred_element_type=F32).astype(kc_ref.dtype)
    hv = _nsa_compress_hidden(uv_ref, pe_ref.at[1], w1_ref.at[1], nc)
    vct_ref[0] = lax.dot_general(w2vt_ref[...].astype(BF16), hv, _NT,
                                 preferred_element_type=F32).astype(vct_ref.dtype)


def _nsa_compress(ub, pe, w1, w2):
    t_len = ub.shape[0]
    nc = t_len // NSA_CMP_STRIDE
    g_n, dh = NSA_KV_GROUPS, NSA_HEAD_DIM
    return pl.pallas_call(
        _nsa_compress_kernel,
        out_shape=(jax.ShapeDtypeStruct((g_n, nc, dh), BF16), jax.ShapeDtypeStruct((g_n, dh, nc), BF16)),
        grid=(g_n,),
        in_specs=[
            pl.BlockSpec((t_len, dh), lambda g: (0, OD_B_KCMP // dh + g)),
            pl.BlockSpec((t_len, dh), lambda g: (0, OD_B_VCMP // dh + g)),
            pl.BlockSpec((2, NSA_CMP_BLOCK, dh), lambda g: (0, 0, 0)),
            pl.BlockSpec((2, NSA_CMP_BLOCK * dh, dh), lambda g: (0, 0, 0)),
            pl.BlockSpec((dh, dh), lambda g: (0, 0)),
            pl.BlockSpec((dh, dh), lambda g: (0, 0)),
        ],
        out_specs=(pl.BlockSpec((1, nc, dh), lambda g: (g, 0, 0)), pl.BlockSpec((1, dh, nc), lambda g: (g, 0, 0))),
        compiler_params=pltpu.CompilerParams(dimension_semantics=("parallel",), vmem_limit_bytes=VMEM_LIMIT),
        name="nsa_compress",
    )(ub, ub, pe, w1, w2[0], w2[1].T)


def _nsa_cmp_kernel(qt_ref, kc_ref, vct_ref, ovt_ref, ocmp_ref, sel_ref):
    tq = qt_ref.shape[1]
    nc = kc_ref.shape[1]
    ns = ovt_ref.shape[0]
    dh = NSA_HEAD_DIM
    t = pl.program_id(1) * tq + lax.broadcasted_iota(jnp.int32, (1, tq), 1)
    n_idx = lax.broadcasted_iota(jnp.int32, (nc, 1), 0)
    valid = n_idx * NSA_CMP_STRIDE + (NSA_CMP_BLOCK - 1) <= t
    kc = kc_ref[0]
    vct = vct_ref[0]
    heads = range(NSA_GH)
    s = [jnp.where(valid, jnp.dot(kc, qt_ref[h * dh:(h + 1) * dh, :], preferred_element_type=F32), NEG)
         for h in heads]
    m = [jnp.max(s[h], axis=0, keepdims=True) for h in heads]
    p = [jnp.where(valid, jnp.exp2(s[h] - m[h]), 0.0) for h in heads]
    l = [jnp.sum(p[h], axis=0, keepdims=True) for h in heads]
    p = [p[h] * (1.0 / jnp.where(l[h] > 0.0, l[h], 1.0)) for h in heads]
    for h in heads:
        ocmp_ref[h * dh:(h + 1) * dh, :] = jnp.dot(vct, p[h].astype(BF16), preferred_element_type=F32)
    psum = sum(p[1:], p[0])
    hi = psum.astype(BF16)
    lo = (psum - hi.astype(F32)).astype(BF16)
    ovt = ovt_ref[...]
    imp = jnp.dot(ovt, hi, preferred_element_type=F32) + jnp.dot(ovt, lo, preferred_element_type=F32)
    j = lax.broadcasted_iota(jnp.int32, (ns, 1), 0)
    cur = lax.shift_right_logical(t, int(math.log2(NSA_SLC_BLOCK)))
    forced = (j == 0) | (j == cur) | (j == cur - 1)
    ok = j * NSA_SLC_BLOCK <= t
    imp = jnp.where(ok, imp + jnp.where(forced, NSA_FORCE_BONUS, 0.0), NEG)
    jf = j.astype(F32)

    def pick(_, left):
        mx = jnp.max(left, axis=0, keepdims=True)
        first = jnp.min(jnp.where(left == mx, jf, float(ns)), axis=0, keepdims=True)
        return jnp.where(jf == first, -jnp.inf, left)

    left = lax.fori_loop(0, min(NSA_TOPK, ns), pick, imp)
    sel_ref[0] = jnp.where((left == -jnp.inf) & ok, 1.0, 0.0).astype(sel_ref.dtype)


def _nsa_cmp(at, kc, vct, overlap_t):
    t_len = at.shape[1]
    tq = min(NSA_TQ, t_len)
    ns, nc = overlap_t.shape
    g_n, dh = NSA_KV_GROUPS, NSA_HEAD_DIM
    gw = NSA_GH * dh
    return pl.pallas_call(
        _nsa_cmp_kernel,
        out_shape=(jax.ShapeDtypeStruct((MIX_W, t_len), F32), jax.ShapeDtypeStruct((g_n, ns, t_len), BF16)),
        grid=(g_n, t_len // tq),
        in_specs=[
            pl.BlockSpec((gw, tq), lambda g, i: (g, i)),
            pl.BlockSpec((1, nc, dh), lambda g, i: (g, 0, 0)),
            pl.BlockSpec((1, dh, nc), lambda g, i: (g, 0, 0)),
            pl.BlockSpec((ns, nc), lambda g, i: (0, 0)),
        ],
        out_specs=(pl.BlockSpec((gw, tq), lambda g, i: (g, i)), pl.BlockSpec((1, ns, tq), lambda g, i: (g, 0, i))),
        compiler_params=pltpu.CompilerParams(
            dimension_semantics=("parallel", "parallel"), vmem_limit_bytes=VMEM_LIMIT),
        name="nsa_cmp",
    )(at, kc, vct, overlap_t)


def _flash_init(m_ref, l_ref, acc_ref):
    m_ref[...] = jnp.full(m_ref.shape, M_INIT, F32)
    l_ref[...] = jnp.zeros(l_ref.shape, F32)
    acc_ref[...] = jnp.zeros(acc_ref.shape, F32)


def _flash_tile(q_of, k_tile, vt_tile, mask, m_ref, l_ref, acc_ref):
    _flash_consume([jnp.dot(k_tile, q_of(h), preferred_element_type=F32) for h in range(NSA_GH)], vt_tile, mask,
                   m_ref, l_ref, acc_ref)


def _flash_consume(s, vt_tile, mask, m_ref, l_ref, acc_ref):
    heads = range(NSA_GH)
    if mask is not None:
        s = [jnp.where(mask, s[h], NEG) for h in heads]
    m_old = [m_ref[h] for h in heads]
    m_new = [jnp.maximum(m_old[h], jnp.max(s[h], axis=0, keepdims=True)) for h in heads]
    alpha = [jnp.exp2(m_old[h] - m_new[h]) for h in heads]
    p = [jnp.exp2(s[h] - m_new[h]) for h in heads]
    for h in heads:
        l_ref[h] = alpha[h] * l_ref[h] + jnp.sum(p[h], axis=0, keepdims=True)
        m_ref[h] = m_new[h]
    pv = [jnp.dot(vt_tile, p[h].astype(BF16), preferred_element_type=F32) for h in heads]
    for h in heads:
        acc_ref[h] = alpha[h] * acc_ref[h] + pv[h]


def _nsa_slc_kernel(qt_ref, k_ref, vt_ref, sel_ref, eb_ref, o_ref, qa_ref, s0_ref, s1_ref, m_ref, l_ref, acc_ref):
    tq = qt_ref.shape[1]
    dh = NSA_HEAD_DIM
    i = pl.program_id(1)
    t = i * tq + lax.broadcasted_iota(jnp.int32, (1, tq), 1)
    c_idx = lax.broadcasted_iota(jnp.int32, (tq, 1), 0)
    bias = ((sel_ref[0].astype(F32) - 1.0) * -NEG).astype(BF16)
    for h in range(NSA_GH):
        qa_ref[h, :dh, :] = qt_ref[h * dh:(h + 1) * dh, :]
        qa_ref[h, dh:, :] = bias
    _flash_init(m_ref, l_ref, acc_ref)

    def scores(jt, s_ref):
        k0 = pl.multiple_of(jt * tq, tq)
        k_aug = jnp.concatenate([k_ref[pl.ds(k0, tq), :], eb_ref[pl.ds(k0, tq), :]], axis=1)
        for h in range(NSA_GH):
            s_ref[h] = jnp.dot(k_aug, qa_ref[h], preferred_element_type=F32)

    def consume(jt, s_ref, causal):
        k0 = pl.multiple_of(jt * tq, tq)
        mask = (c_idx + k0 <= t) if causal else None
        _flash_consume([s_ref[h] for h in range(NSA_GH)], vt_ref[:, pl.ds(k0, tq)], mask, m_ref, l_ref, acc_ref)

    def pair(p, carry):
        scores(2 * p + 1, s1_ref)
        consume(2 * p, s0_ref, False)
        scores(2 * p + 2, s0_ref)
        consume(2 * p + 1, s1_ref, False)
        return carry

    scores(0, s0_ref)
    lax.fori_loop(0, i // 2, pair, 0)

    @pl.when(i % 2 == 0)
    def _():
        consume(i, s0_ref, True)

    @pl.when(i % 2 == 1)
    def _():
        scores(i, s1_ref)
        consume(i - 1, s0_ref, False)
        consume(i, s1_ref, True)

    for h in range(NSA_GH):
        o_ref[h * dh:(h + 1) * dh, :] = acc_ref[h] * (1.0 / l_ref[h])


def _nsa_win_kernel(qt_ref, k_ref, vt_ref, ocmp_ref, oslc_ref, gates_ref, gp_ref, o_ref, m_ref, l_ref, acc_ref):
    tq = qt_ref.shape[1]
    dh = NSA_HEAD_DIM
    i = pl.program_id(1)
    t = i * tq + lax.broadcasted_iota(jnp.int32, (1, tq), 1)
    c_idx = lax.broadcasted_iota(jnp.int32, (tq, 1), 0)
    _flash_init(m_ref, l_ref, acc_ref)

    def scores(jt):
        k_tile = k_ref[pl.ds(pl.multiple_of(jt * tq, tq), tq), :]
        return [jnp.dot(k_tile, qt_ref[h * dh:(h + 1) * dh, :], preferred_element_type=F32) for h in range(NSA_GH)]

    def consume(jt, s, mask_of):
        k0 = pl.multiple_of(jt * tq, tq)
        mask = None if mask_of is None else mask_of(c_idx + k0)
        _flash_consume(s, vt_ref[:, pl.ds(k0, tq)], mask, m_ref, l_ref, acc_ref)

    edge = lambda key: key > t - NSA_WINDOW
    causal = lambda key: key <= t

    @pl.when(i >= 2)
    def _():
        s_old, s_mid, s_new = scores(i - 2), scores(i - 1), scores(i)
        consume(i - 2, s_old, edge)
        consume(i - 1, s_mid, None)
        consume(i, s_new, causal)

    @pl.when(i == 1)
    def _():
        s_mid, s_new = scores(0), scores(1)
        consume(0, s_mid, None)
        consume(1, s_new, causal)

    @pl.when(i == 0)
    def _():
        consume(0, scores(0), causal)
    gates_t = jnp.transpose(jax.nn.sigmoid(gates_ref[...]))
    for h in range(NSA_GH):
        rows = slice(h * dh, (h + 1) * dh)
        o_t = (gates_t[h:h + 1, :] * ocmp_ref[rows, :]
               + gates_t[NSA_GH + h:NSA_GH + h + 1, :] * oslc_ref[rows, :]
               + gates_t[2 * NSA_GH + h:2 * NSA_GH + h + 1, :] * (acc_ref[h] * (1.0 / l_ref[h])))
        gp = gp_ref[:, rows]
        o_ref[:, rows] = (jnp.transpose(o_t) * (gp * jax.nn.sigmoid(gp))).astype(o_ref.dtype)


def _nsa_scratch(tq):
    return [pltpu.VMEM((NSA_GH, 1, tq), F32), pltpu.VMEM((NSA_GH, 1, tq), F32),
            pltpu.VMEM((NSA_GH, NSA_HEAD_DIM, tq), F32)]


def _nsa_slc(at, an, sel):
    t_len = at.shape[1]
    tq = NSA_TQ
    ns = sel.shape[1]
    g_n, dh = NSA_KV_GROUPS, NSA_HEAD_DIM
    gw = NSA_GH * dh
    block_of_key = jnp.asarray(np.arange(t_len)[:, None] // NSA_SLC_BLOCK == np.arange(ns)[None], dtype=BF16)
    return pl.pallas_call(
        _nsa_slc_kernel,
        out_shape=jax.ShapeDtypeStruct((MIX_W, t_len), F32),
        grid=(g_n, t_len // tq),
        in_specs=[
            pl.BlockSpec((gw, tq), lambda g, i: (g, i)),
            pl.BlockSpec((t_len, dh), lambda g, i: (0, OD_AN_KSLC + g)),
            pl.BlockSpec((dh, t_len), lambda g, i: (OD_AT_VSLC + g, 0)),
            pl.BlockSpec((1, ns, tq), lambda g, i: (g, 0, i)),
            pl.BlockSpec((t_len, ns), lambda g, i: (0, 0)),
        ],
        out_specs=pl.BlockSpec((gw, tq), lambda g, i: (g, i)),
        scratch_shapes=[pltpu.VMEM((NSA_GH, dh + ns, tq), BF16), pltpu.VMEM((NSA_GH, tq, tq), F32),
                        pltpu.VMEM((NSA_GH, tq, tq), F32)] + _nsa_scratch(tq),
        compiler_params=pltpu.CompilerParams(
            dimension_semantics=("parallel", "parallel"), vmem_limit_bytes=VMEM_LIMIT),
        name="nsa_slc",
    )(at, an, at, sel, block_of_key)


def _nsa_win(at, an, ub, ocmp, oslc):
    t_len = at.shape[1]
    tq = NSA_TQ
    assert NSA_WINDOW == 2 * tq
    g_n, dh = NSA_KV_GROUPS, NSA_HEAD_DIM
    gw = NSA_GH * dh
    return pl.pallas_call(
        _nsa_win_kernel,
        out_shape=jax.ShapeDtypeStruct((t_len, MIX_W), BF16),
        grid=(g_n, t_len // tq),
        in_specs=[
            pl.BlockSpec((gw, tq), lambda g, i: (g, i)),
            pl.BlockSpec((t_len, dh), lambda g, i: (0, OD_AN_KWIN + g)),
            pl.BlockSpec((dh, t_len), lambda g, i: (OD_AT_VWIN + g, 0)),
            pl.BlockSpec((gw, tq), lambda g, i: (g, i)),
            pl.BlockSpec((gw, tq), lambda g, i: (g, i)),
            pl.BlockSpec((tq, LANES), lambda g, i: (i, OD_B_GATES // LANES + g)),
            pl.BlockSpec((tq, gw), lambda g, i: (i, OD_B_GP // gw + g)),
        ],
        out_specs=pl.BlockSpec((tq, gw), lambda g, i: (i, g)),
        scratch_shapes=_nsa_scratch(tq),
        compiler_params=pltpu.CompilerParams(
            dimension_semantics=("parallel", "parallel"), vmem_limit_bytes=VMEM_LIMIT),
        name="nsa_win",
    )(at, an, at, ocmp, oslc, ub, ub)


def _nsa_overlap_t(t_len):
    nc = t_len // NSA_CMP_STRIDE
    n_cmp = (t_len - NSA_CMP_BLOCK) // NSA_CMP_STRIDE + 1
    cs = np.arange(n_cmp) * NSA_CMP_STRIDE
    ss = np.arange(t_len // NSA_SLC_BLOCK) * NSA_SLC_BLOCK
    ov = np.clip(np.minimum(cs[:, None] + NSA_CMP_BLOCK, ss[None] + NSA_SLC_BLOCK)
                 - np.maximum(cs[:, None], ss[None]), 0, None) / NSA_CMP_BLOCK
    ov = np.concatenate([ov, np.zeros((nc - n_cmp, ov.shape[1]))], axis=0)
    return jnp.asarray(ov.T, dtype=BF16)


def _nsa_mixer(at, an, ub, cmp_pe, cmp_w1, cmp_w2):
    t_len = an.shape[0]
    kc, vct = _nsa_compress(ub, cmp_pe, cmp_w1, cmp_w2)
    ocmp, sel = _nsa_cmp(at, kc, vct, _nsa_overlap_t(t_len))
    oslc = _nsa_slc(at, an, sel)
    return _nsa_win(at, an, ub, ocmp, oslc)


CONV_TAIL = 8


def _cumsum_rows(x):
    n = x.shape[0]
    row = lax.broadcasted_iota(jnp.int32, x.shape, 0)
    s = 1
    while s < n:
        x = x + jnp.where(row >= s, pltpu.roll(x, s, axis=0), 0.0)
        s *= 2
    return x


def _split_bf16(v, parts):
    out = []
    for _ in range(parts - 1):
        p = v.astype(BF16)
        out.append(p)
        v = v - p.astype(F32)
    out.append(v.astype(BF16))
    return out


def _expand_lanes(v, onehot):
    return sum(jnp.dot(p, onehot, preferred_element_type=F32) for p in _split_bf16(v, 3))


def _conv_silu(x, tail, w, b):
    row = lax.broadcasted_iota(jnp.int32, (CONV_TAIL, 1), 0)
    y = b + w[SSD_CONV - 1:SSD_CONV, :] * x
    for s in range(1, SSD_CONV):
        xs = pltpu.roll(x, s, axis=0)
        head = jnp.where(row < s, pltpu.roll(tail, s, axis=0), xs[:CONV_TAIL, :])
        y = y + w[SSD_CONV - 1 - s:SSD_CONV - s, :] * jnp.concatenate([head, xs[CONV_TAIL:, :]], axis=0)
    return y * jax.nn.sigmoid(y)


def _softplus(x):
    return jnp.maximum(x, 0.0) + jnp.log1p(jnp.exp(-jnp.abs(x)))


_TN = (((0,), (0,)), ((), ()))


def _ssd_kernel(z_ref, x_ref, b_ref, c_ref, dt_ref, cwx_ref, cwb_ref, cwc_ref, cbx_ref, cbb_ref, cbc_ref,
                dtb_ref, alog_ref, dsk_ref, nw_ref, e_ref, o_ref, tail_ref, state_ref, y_ref):
    L = SSD_CHUNK
    n, gw = SSD_STATE, (SSD_HEADS // SSD_GROUPS) * SSD_HEAD_DIM

    @pl.when(pl.program_id(0) == 0)
    def _():
        tail_ref[...] = jnp.zeros(tail_ref.shape, F32)
        state_ref[...] = jnp.zeros(state_ref.shape, F32)

    xr, br, cr = x_ref[...], b_ref[...], c_ref[...]
    x = _conv_silu(xr, tail_ref[:, :MIX_W], cwx_ref[...], cbx_ref[...])
    bm = _conv_silu(br, tail_ref[:, MIX_W:MIX_W + SSD_GN], cwb_ref[...], cbb_ref[...])
    cm = _conv_silu(cr, tail_ref[:, MIX_W + SSD_GN:], cwc_ref[...], cbc_ref[...])
    tail_ref[:, :MIX_W] = xr[L - CONV_TAIL:, :]
    tail_ref[:, MIX_W:MIX_W + SSD_GN] = br[L - CONV_TAIL:, :]
    tail_ref[:, MIX_W + SSD_GN:] = cr[L - CONV_TAIL:, :]

    dt = _softplus(dt_ref[...] + dtb_ref[...])
    cum = _cumsum_rows(dt * -jnp.exp(alog_ref[...]))
    cum_t = jnp.transpose(cum)
    onehot = e_ref[...]
    cum_e = _expand_lanes(cum, onehot)
    xdt = x * _expand_lanes(dt, onehot)
    last_e = cum_e[L - 1:L, :]
    xdt16 = xdt.astype(BF16)
    xw16 = (xdt * jnp.exp(last_e - cum_e)).astype(BF16)
    ecum_e = jnp.exp(cum_e)
    causal = lax.broadcasted_iota(jnp.int32, (L, L), 0) >= lax.broadcasted_iota(jnp.int32, (L, L), 1)
    lane = lax.broadcasted_iota(jnp.int32, (1, gw), 1)
    for g in range(SSD_GROUPS):
        cols = slice(g * gw, (g + 1) * gw)
        bg = bm[:, g * n:(g + 1) * n]
        cg16 = cm[:, g * n:(g + 1) * n].astype(BF16)
        sc = lax.dot_general(cg16, bg.astype(BF16), _NT, preferred_element_type=F32)
        state = state_ref[g]
        yg = jnp.dot(cg16, state.astype(BF16), preferred_element_type=F32) * ecum_e[:, cols]
        xg = xdt16[:, cols]
        for r in range(SSD_HEADS // SSD_GROUPS):
            h = g * (SSD_HEADS // SSD_GROUPS) + r
            decay = jnp.where(causal, jnp.exp(cum[:, h:h + 1] - cum_t[h:h + 1, :]), 0.0)
            in_head = (lane >= r * SSD_HEAD_DIM) & (lane < (r + 1) * SSD_HEAD_DIM)
            yg += jnp.dot((sc * decay).astype(BF16), jnp.where(in_head, xg, jnp.zeros_like(xg)),
                          preferred_element_type=F32)
        y_ref[:, cols] = yg
        bg_t = jnp.transpose(bg).astype(BF16)
        state_ref[g] = state * jnp.exp(last_e[:, cols]) + jnp.dot(bg_t, xw16[:, cols], preferred_element_type=F32)

    y = y_ref[...] + x * dsk_ref[...]
    zz = z_ref[...]
    y = y * (zz * jax.nn.sigmoid(zz))
    ms = jnp.mean(y * y, axis=-1, keepdims=True)
    o_ref[...] = (y * lax.rsqrt(ms + EPS) * nw_ref[...]).astype(o_ref.dtype)


def _ssd_mixer(u, conv_w, conv_b, dt_bias, a_log, d_skip, norm_w):
    t_len = u.shape[0]
    L = SSD_CHUNK
    pad = lambda v: jnp.pad(v, (0, LANES - SSD_HEADS)).reshape(1, LANES)
    onehot = np.zeros((LANES, MIX_W), np.float32)
    onehot[np.arange(MIX_W) // SSD_HEAD_DIM, np.arange(MIX_W)] = 1.0
    row = lambda width, start: pl.BlockSpec((L, width), lambda c: (c, start // width))
    fixed = lambda rows, width, start: pl.BlockSpec((rows, width), lambda c: (0, start // width))
    return pl.pallas_call(
        _ssd_kernel,
        out_shape=jax.ShapeDtypeStruct((t_len, MIX_W), BF16),
        grid=(t_len // L,),
        in_specs=[
            row(MIX_W, EV_Z), row(MIX_W, EV_X), row(SSD_GN, EV_B), row(SSD_GN, EV_C), row(LANES, EV_DT),
            fixed(SSD_CONV, MIX_W, 0), fixed(SSD_CONV, SSD_GN, MIX_W), fixed(SSD_CONV, SSD_GN, MIX_W + SSD_GN),
            fixed(1, MIX_W, 0), fixed(1, SSD_GN, MIX_W), fixed(1, SSD_GN, MIX_W + SSD_GN),
            fixed(1, LANES, 0), fixed(1, LANES, 0), fixed(1, MIX_W, 0), fixed(1, MIX_W, 0),
            fixed(LANES, MIX_W, 0),
        ],
        out_specs=pl.BlockSpec((L, MIX_W), lambda c: (c, 0)),
        scratch_shapes=[pltpu.VMEM((CONV_TAIL, SSD_XBC), F32),
                        pltpu.VMEM((SSD_GROUPS, SSD_STATE, MIX_W // SSD_GROUPS), F32),
                        pltpu.VMEM((L, MIX_W), F32)],
        compiler_params=pltpu.CompilerParams(dimension_semantics=("arbitrary",), vmem_limit_bytes=VMEM_LIMIT),
        name="ssd_mixer",
    )(u, u, u, u, u, conv_w, conv_w, conv_w, conv_b.reshape(1, -1), conv_b.reshape(1, -1), conv_b.reshape(1, -1),
      pad(dt_bias), pad(a_log), jnp.repeat(d_skip, SSD_HEAD_DIM).reshape(1, MIX_W), norm_w.reshape(1, MIX_W),
      jnp.asarray(onehot, dtype=BF16))


def _gla_kernel(q_ref, k_ref, v_ref, r_ref, glr_ref, wg_ref, bg_ref, nw_ref, o_ref, state_ref):
    L, dk, dv = GLA_CHUNK, GLA_DK, GLA_DV

    @pl.when(pl.program_id(0) == 0)
    def _():
        state_ref[...] = jnp.zeros(state_ref.shape, F32)

    g_hi, g_lo = _split_bf16(glr_ref[...], 2)
    w_hi, w_lo = _split_bf16(wg_ref[...], 2)
    logits = (jnp.dot(g_hi, w_hi, preferred_element_type=F32) + jnp.dot(g_hi, w_lo, preferred_element_type=F32)
              + jnp.dot(g_lo, w_hi, preferred_element_type=F32) + bg_ref[...])
    cum = _cumsum_rows(-_softplus(-logits) * (1.0 / GLA_TAU))
    tot = cum[L - 1:L, :]
    half = 0.5 * tot
    q = q_ref[...] * dk ** -0.5
    k = k_ref[...]
    qe = (q * jnp.exp(cum - half)).astype(BF16)
    ke = (k * jnp.exp(half - cum)).astype(BF16)
    qd = (q * jnp.exp(cum)).astype(BF16)
    kd = k * jnp.exp(tot - cum)
    etot = jnp.exp(tot)
    causal = lax.broadcasted_iota(jnp.int32, (L, L), 0) >= lax.broadcasted_iota(jnp.int32, (L, L), 1)
    for h in range(GLA_HEADS):
        ks = slice(h * dk, (h + 1) * dk)
        vs = slice(h * dv, (h + 1) * dv)
        att = lax.dot_general(qe[:, ks], ke[:, ks], _NT, preferred_element_type=F32)
        att = jnp.where(causal, att, 0.0).astype(BF16)
        vh = v_ref[:, vs].astype(BF16)
        state = state_ref[h]
        y = jnp.dot(att, vh, preferred_element_type=F32) + jnp.dot(qd[:, ks], state.astype(BF16),
                                                                   preferred_element_type=F32)
        kd_t = jnp.transpose(kd[:, ks]).astype(BF16)
        decay_col = jnp.transpose(jnp.broadcast_to(etot[:, ks], (8, dk)))[:, 0:1]
        state_ref[h] = state * decay_col + jnp.dot(kd_t, vh, preferred_element_type=F32)
        ms = jnp.mean(y * y, axis=-1, keepdims=True)
        r = r_ref[:, vs]
        o_ref[:, vs] = (y * lax.rsqrt(ms + EPS) * nw_ref[...] * (r * jax.nn.sigmoid(r))).astype(o_ref.dtype)


def _gla_mixer(u, w_gate, b_gate, norm_w):
    t_len = u.shape[0]
    L = GLA_CHUNK
    row = lambda width, start: pl.BlockSpec((L, width), lambda c: (c, start // width))
    full = lambda a: pl.BlockSpec(a.shape, lambda c: (0, 0))
    wg = jnp.pad(w_gate, ((0, LANES - GLA_RANK), (0, 0)))
    bg = b_gate.reshape(1, GLA_DKT)
    nw = norm_w.reshape(1, GLA_DV)
    return pl.pallas_call(
        _gla_kernel,
        out_shape=jax.ShapeDtypeStruct((t_len, MIX_W), BF16),
        grid=(t_len // L,),
        in_specs=[row(GLA_DKT, EV_Q), row(GLA_DKT, EV_K), row(MIX_W, EV_V), row(MIX_W, EV_R), row(LANES, EV_GLR),
                  full(wg), full(bg), full(nw)],
        out_specs=pl.BlockSpec((L, MIX_W), lambda c: (c, 0)),
        scratch_shapes=[pltpu.VMEM((GLA_HEADS, GLA_DK, GLA_DV), F32)],
        compiler_params=pltpu.CompilerParams(dimension_semantics=("arbitrary",), vmem_limit_bytes=VMEM_LIMIT),
        name="gla_mixer",
    )(u, u, u, u, u, wg, bg, nw)


def _unit_lower_inverses(mats):
    n = mats[0].shape[0]
    eye = (lax.broadcasted_iota(jnp.int32, (n, n), 0) == lax.broadcasted_iota(jnp.int32, (n, n), 1)).astype(F32)
    ps = [eye - a for a in mats]
    xs = list(mats)
    for _ in range(int(math.log2(n)) - 1):
        x16 = [x.astype(BF16) for x in xs]
        xs = [jnp.dot(x, x, preferred_element_type=F32) for x in x16]
        ps = [p + jnp.dot(p.astype(BF16), x.astype(BF16), preferred_element_type=F32) for p, x in zip(ps, xs)]
    return ps


def _gdn_kernel(q_ref, k_ref, v_ref, z_ref, ab_ref, cwq_ref, cwk_ref, cwv_ref, cbq_ref, cbk_ref, cbv_ref,
                alog_ref, dtb_ref, nw_ref, o_ref, tail_ref, state_ref):
    L, dk, dv = GDN_CHUNK, GDN_DK, GDN_DV
    rep = GDN_V_HEADS // GDN_QK_HEADS

    @pl.when(pl.program_id(0) == 0)
    def _():
        tail_ref[...] = jnp.zeros(tail_ref.shape, F32)
        state_ref[...] = jnp.zeros(state_ref.shape, F32)

    qr, kr, vr = q_ref[...], k_ref[...], v_ref[...]
    q = _conv_silu(qr, tail_ref[:, :GDN_QKW], cwq_ref[...], cbq_ref[...])
    k = _conv_silu(kr, tail_ref[:, GDN_QKW:2 * GDN_QKW], cwk_ref[...], cbk_ref[...])
    v = _conv_silu(vr, tail_ref[:, 2 * GDN_QKW:], cwv_ref[...], cbv_ref[...])
    tail_ref[:, :GDN_QKW] = qr[L - CONV_TAIL:, :]
    tail_ref[:, GDN_QKW:2 * GDN_QKW] = kr[L - CONV_TAIL:, :]
    tail_ref[:, 2 * GDN_QKW:] = vr[L - CONV_TAIL:, :]

    ab = ab_ref[...]
    beta = jax.nn.sigmoid(ab)
    gc = _cumsum_rows(-jnp.exp(alog_ref[...]) * _softplus(ab + dtb_ref[...]))
    gc_t = jnp.transpose(gc)
    egc = jnp.exp(gc)
    last = gc[L - 1:L, :]
    e_end = jnp.exp(last - gc)
    e_last = jnp.exp(last)
    rows = lax.broadcasted_iota(jnp.int32, (L, L), 0)
    cols = lax.broadcasted_iota(jnp.int32, (L, L), 1)
    heads = range(GDN_V_HEADS)
    qn, kn, kk, qk = [], [], [], []
    for j in range(GDN_QK_HEADS):
        qj = q[:, j * dk:(j + 1) * dk]
        kj = k[:, j * dk:(j + 1) * dk]
        qn.append(qj * (lax.rsqrt(jnp.sum(qj * qj, axis=-1, keepdims=True) + EPS) * dk ** -0.5))
        kn.append(kj * lax.rsqrt(jnp.sum(kj * kj, axis=-1, keepdims=True) + EPS))
        kn16 = kn[j].astype(BF16)
        kk.append(lax.dot_general(kn16, kn16, _NT, preferred_element_type=F32))
        qk.append(lax.dot_general(qn[j].astype(BF16), kn16, _NT, preferred_element_type=F32))
    decay = [jnp.where(rows >= cols, jnp.exp(gc[:, h:h + 1] - gc_t[h:h + 1, :]), 0.0) for h in heads]
    b_col = [beta[:, GDN_V_HEADS + h:GDN_V_HEADS + h + 1] for h in heads]
    t_mat = _unit_lower_inverses([jnp.where(rows > cols, b_col[h] * kk[h // rep] * decay[h], 0.0) for h in heads])
    rhs = [jnp.concatenate([v[:, h * dv:(h + 1) * dv] * b_col[h], kn[h // rep] * (b_col[h] * egc[:, h:h + 1])],
                           axis=1).astype(BF16) for h in heads]
    uw = [jnp.dot(t_mat[h].astype(BF16), rhs[h], preferred_element_type=F32) for h in heads]
    state = [state_ref[h] for h in heads]
    s16 = [s.astype(BF16) for s in state]
    vn16 = [(uw[h][:, :dv] - jnp.dot(uw[h][:, dv:].astype(BF16), s16[h], preferred_element_type=F32)).astype(BF16)
            for h in heads]
    out = [jnp.dot((qn[h // rep] * egc[:, h:h + 1]).astype(BF16), s16[h], preferred_element_type=F32)
           + jnp.dot((qk[h // rep] * decay[h]).astype(BF16), vn16[h], preferred_element_type=F32) for h in heads]
    for h in heads:
        kd_t = jnp.transpose(kn[h // rep] * e_end[:, h:h + 1]).astype(BF16)
        state_ref[h] = state[h] * e_last[:, h:h + 1] + jnp.dot(kd_t, vn16[h], preferred_element_type=F32)
    for h in heads:
        o = out[h]
        ms = jnp.mean(o * o, axis=-1, keepdims=True)
        zz = z_ref[:, h * dv:(h + 1) * dv]
        o_ref[:, h * dv:(h + 1) * dv] = (o * lax.rsqrt(ms + EPS) * nw_ref[...]
                                         * (zz * jax.nn.sigmoid(zz))).astype(o_ref.dtype)


def _gdn_mixer(ub, conv_w, conv_b, a_log, dt_bias, norm_w):
    t_len = ub.shape[0]
    L = GDN_CHUNK
    pad = lambda v: jnp.pad(v, (0, LANES - GDN_V_HEADS)).reshape(1, LANES)
    row = lambda width, start: pl.BlockSpec((L, width), lambda c: (c, start // width))
    fixed = lambda rows, width, start: pl.BlockSpec((rows, width), lambda c: (0, start // width))
    cb = conv_b.reshape(1, -1)
    return pl.pallas_call(
        _gdn_kernel,
        out_shape=jax.ShapeDtypeStruct((t_len, MIX_W), BF16),
        grid=(t_len // L,),
        in_specs=[
            row(GDN_QKW, OD_B_CONV), row(GDN_QKW, OD_B_CONV + GDN_QKW), row(MIX_W, OD_B_CONV + 2 * GDN_QKW),
            row(MIX_W, OD_B_Z), row(LANES, OD_B_AB),
            fixed(GDN_CONV, GDN_QKW, 0), fixed(GDN_CONV, GDN_QKW, GDN_QKW), fixed(GDN_CONV, MIX_W, 2 * GDN_QKW),
            fixed(1, GDN_QKW, 0), fixed(1, GDN_QKW, GDN_QKW), fixed(1, MIX_W, 2 * GDN_QKW),
            fixed(1, LANES, 0), fixed(1, LANES, 0), fixed(1, GDN_DV, 0),
        ],
        out_specs=pl.BlockSpec((L, MIX_W), lambda c: (c, 0)),
        scratch_shapes=[pltpu.VMEM((CONV_TAIL, GDN_CONV_CH), F32),
                        pltpu.VMEM((GDN_V_HEADS, GDN_DK, GDN_DV), F32)],
        compiler_params=pltpu.CompilerParams(dimension_semantics=("arbitrary",), vmem_limit_bytes=VMEM_LIMIT),
        name="gdn_mixer",
    )(ub, ub, ub, ub, ub, conv_w, conv_w, conv_w, cb, cb, cb, pad(a_log), pad(dt_bias),
      norm_w.reshape(1, GDN_DV))


def _even_layer(xs, i, ln_w, w_ev, w_out, conv_w, conv_b, dt_bias, a_log, d_skip, ssd_norm_w, w_gate, b_gate,
                gla_norm_w):
    u = _proj(_rmsnorm(xs, ln_w, BF16), w_ev, i, F32)
    ya = _ssd_mixer(u, conv_w, conv_b, dt_bias, a_log, d_skip, ssd_norm_w)
    yb = _gla_mixer(u, w_gate, b_gate, gla_norm_w)
    return _out_proj(xs, ya, yb, w_out, i)


def _odd_layer(xs, i, ln_w, w_at, w_an, w_b, w_out, cmp_pe, cmp_w1, cmp_w2, conv_w, conv_b, a_log, dt_bias,
               gdn_norm_w):
    q_scale = np.ones((OD_AT_W, 1), np.float32)
    q_scale[:MIX_W] = NSA_HEAD_DIM ** -0.5 * math.log2(math.e)
    h = _rmsnorm(xs, ln_w, BF16)
    at = _proj_t(h, w_at, i, jnp.asarray(q_scale), BF16)
    an = _proj(h, w_an, i, BF16)
    ub = _proj(h, w_b, i, F32)
    yc = _nsa_mixer(at, an, ub, cmp_pe, cmp_w1, cmp_w2)
    yd = _gdn_mixer(ub, conv_w, conv_b, a_log, dt_bias, gdn_norm_w)
    return _out_proj(xs, yc, yd, w_out, i)


def kernel(x, ln_w, final_ln_w, ev_w_in, ev_w_out, ssd_conv_w, ssd_conv_b, ssd_dt_bias, ssd_a_log, ssd_d,
           ssd_norm_w, gla_w_gate, gla_b_gate, gla_norm_w, od_w_in, od_w_out, nsa_cmp_pe, nsa_cmp_w1,
           nsa_cmp_w2, gdn_conv_w, gdn_conv_b, gdn_a_log, gdn_dt_bias, gdn_norm_w):
    bsz, t_len, d = x.shape
    assert bsz == 1 and d == D_MODEL
    xs = x.reshape(t_len, d)
    w_ev = _gather_cols(ev_w_in, _EV)
    w_at = _gather_cols(od_w_in, _OD_AT)
    w_an = _gather_cols(od_w_in, _OD_AN)
    w_b = _gather_cols(od_w_in, _OD_B)
    for layer in range(DEPTH):
        i = layer // 2
        if layer % 2 == 0:
            xs = _even_layer(xs, i, ln_w[layer], w_ev, ev_w_out, ssd_conv_w[i], ssd_conv_b[i], ssd_dt_bias[i],
                             ssd_a_log[i], ssd_d[i], ssd_norm_w[i], gla_w_gate[i], gla_b_gate[i], gla_norm_w[i])
        else:
            xs = _odd_layer(xs, i, ln_w[layer], w_at, w_an, w_b, od_w_out, nsa_cmp_pe[i], nsa_cmp_w1[i],
                            nsa_cmp_w2[i], gdn_conv_w[i], gdn_conv_b[i], gdn_a_log[i], gdn_dt_bias[i], gdn_norm_w[i])
    return _rmsnorm(xs, final_ln_w, F32).reshape(bsz, t_len, d)
```

```python
import functools
import math

import numpy as np
import jax
import jax.numpy as jnp
from jax import lax
from jax.experimental import pallas as pl
from jax.experimental.pallas import tpu as pltpu

D_MODEL = 2048
DEPTH = 4
EPS = 1e-6
NEG = -1e30
MIX_W = D_MODEL

SSD_HEAD_DIM = 64
SSD_HEADS = MIX_W // SSD_HEAD_DIM
SSD_STATE = 128
SSD_GROUPS = 8
SSD_CONV = 4
SSD_CHUNK = 128
SSD_GN = SSD_GROUPS * SSD_STATE
SSD_XBC = MIX_W + 2 * SSD_GN
SSD_IN = MIX_W + SSD_XBC + SSD_HEADS

GLA_HEADS = 4
GLA_DKT = MIX_W // 2
GLA_DK = GLA_DKT // GLA_HEADS
GLA_DV = MIX_W // GLA_HEADS
GLA_RANK = 16
GLA_TAU = 16.0
GLA_CHUNK = 128
GLA_IN = 2 * GLA_DKT + 2 * MIX_W + GLA_RANK

NSA_HEADS = 16
NSA_HEAD_DIM = MIX_W // NSA_HEADS
NSA_KV_GROUPS = 4
NSA_GH = NSA_HEADS // NSA_KV_GROUPS
NSA_KVW = NSA_KV_GROUPS * NSA_HEAD_DIM
NSA_CMP_BLOCK = 32
NSA_CMP_STRIDE = 16
NSA_SLC_BLOCK = 64
NSA_TOPK = 16
NSA_WINDOW = 512
NSA_FORCE_BONUS = 1e3
NSA_IN = 2 * MIX_W + 6 * NSA_KVW + 3 * NSA_HEADS

GDN_QK_HEADS = 8
GDN_V_HEADS = 16
GDN_DK = 128
GDN_DV = MIX_W // GDN_V_HEADS
GDN_CONV = 4
GDN_CHUNK = 64
GDN_QKW = GDN_QK_HEADS * GDN_DK
GDN_CONV_CH = 2 * GDN_QKW + MIX_W
GDN_IN = GDN_CONV_CH + 2 * GDN_V_HEADS + MIX_W

LANES = 128
VMEM_LIMIT = 56 * 1024 * 1024
M_INIT = 0.1 * NEG

BF16 = jnp.bfloat16
F32 = jnp.float32


def _width(pieces):
    return sum(w for _, w in pieces)


SUBLANES = 8


def _pack_kernel(pieces, wt_ref, o_ref):
    x = wt_ref[...]
    rows = [jnp.zeros((wd, x.shape[1]), F32) if s is None else x[s:s + wd, :] for s, wd in pieces]
    o_ref[...] = jnp.concatenate(rows, axis=0).astype(BF16)


def _gather_cols(w, pieces, dc=LANES):
    wt = jnp.swapaxes(w, 1, 2)
    layers, n_in, d = wt.shape
    width = _width(pieces)
    used = max(s + wd for s, wd in pieces if s is not None)
    n_in = min(n_in, -(-used // SUBLANES) * SUBLANES)
    return pl.pallas_call(
        functools.partial(_pack_kernel, pieces),
        out_shape=jax.ShapeDtypeStruct((layers, width, d), BF16),
        grid=(layers, d // dc),
        in_specs=[pl.BlockSpec((None, n_in, dc), lambda l, c: (l, 0, c))],
        out_specs=pl.BlockSpec((None, width, dc), lambda l, c: (l, 0, c)),
        compiler_params=pltpu.CompilerParams(
            dimension_semantics=("parallel", "parallel"), vmem_limit_bytes=VMEM_LIMIT),
        name="pack_weights",
    )(wt)


_KV0 = MIX_W
_GATES0 = MIX_W + 6 * NSA_KVW

_OD_AT = [(0, MIX_W), (_KV0 + 3 * NSA_KVW, NSA_KVW), (_KV0 + 5 * NSA_KVW, NSA_KVW)]
OD_AT_W = _width(_OD_AT)
OD_AT_VSLC = MIX_W // NSA_HEAD_DIM
OD_AT_VWIN = OD_AT_VSLC + NSA_KV_GROUPS
_OD_AN = [(_KV0 + 2 * NSA_KVW, NSA_KVW), (_KV0 + 4 * NSA_KVW, NSA_KVW)]
OD_AN_W = _width(_OD_AN)
OD_AN_KSLC, OD_AN_KWIN = 0, NSA_KV_GROUPS

_OD_B = [(NSA_IN, GDN_CONV_CH), (NSA_IN + GDN_CONV_CH + 2 * GDN_V_HEADS, MIX_W), (_GATES0 + 3 * NSA_HEADS, MIX_W),
         (_KV0, 2 * NSA_KVW)]
for _g in range(NSA_KV_GROUPS):
    _OD_B += [(_GATES0 + br * NSA_HEADS + _g * NSA_GH, NSA_GH) for br in range(3)] + [(None, LANES - 3 * NSA_GH)]
_OD_B += [(NSA_IN + GDN_CONV_CH, 2 * GDN_V_HEADS), (None, LANES - 2 * GDN_V_HEADS)]
OD_B_W = _width(_OD_B)
OD_B_CONV = 0
OD_B_Z = OD_B_CONV + GDN_CONV_CH
OD_B_GP = OD_B_Z + MIX_W
OD_B_KCMP = OD_B_GP + MIX_W
OD_B_VCMP = OD_B_KCMP + NSA_KVW
OD_B_GATES = OD_B_VCMP + NSA_KVW
OD_B_AB = OD_B_GATES + NSA_KV_GROUPS * LANES

_GLA_R0 = SSD_IN + 2 * GLA_DKT + MIX_W + GLA_RANK
_EV = [(0, MIX_W + SSD_XBC), (SSD_IN, 2 * GLA_DKT + MIX_W), (_GLA_R0, MIX_W),
       (MIX_W + SSD_XBC, SSD_HEADS), (None, LANES - SSD_HEADS),
       (_GLA_R0 - GLA_RANK, GLA_RANK), (None, LANES - GLA_RANK)]
EV_W = _width(_EV)
EV_Z, EV_X = 0, MIX_W
EV_B = EV_X + MIX_W
EV_C = EV_B + SSD_GN
EV_Q = EV_C + SSD_GN
EV_K = EV_Q + GLA_DKT
EV_V = EV_K + GLA_DKT
EV_R = EV_V + MIX_W
EV_DT = EV_R + MIX_W
EV_GLR = EV_DT + LANES


def _pick_tn(n, cap=2048):
    best = LANES
    for k in range(1, n // LANES + 1):
        if (n // LANES) % k == 0 and k * LANES <= cap:
            best = k * LANES
    return best


def _rmsnorm_kernel(x_ref, g_ref, o_ref):
    x = x_ref[...]
    ms = jnp.mean(x * x, axis=-1, keepdims=True)
    o_ref[...] = (x * lax.rsqrt(ms + EPS) * g_ref[...]).astype(o_ref.dtype)


def _rmsnorm(x, g, out_dtype, tm=512):
    t_len, d = x.shape
    return pl.pallas_call(
        _rmsnorm_kernel,
        out_shape=jax.ShapeDtypeStruct((t_len, d), out_dtype),
        grid=(t_len // tm,),
        in_specs=[pl.BlockSpec((tm, d), lambda i: (i, 0)), pl.BlockSpec((1, d), lambda i: (0, 0))],
        out_specs=pl.BlockSpec((tm, d), lambda i: (i, 0)),
        compiler_params=pltpu.CompilerParams(dimension_semantics=("parallel",), vmem_limit_bytes=VMEM_LIMIT),
        name="rmsnorm",
    )(x, g.reshape(1, d))


def _proj_kernel(h_ref, wt_ref, o_ref):
    acc = lax.dot_general(h_ref[...], wt_ref[...], (((1,), (1,)), ((), ())), preferred_element_type=F32)
    o_ref[...] = acc.astype(o_ref.dtype)


def _proj_t_kernel(h_ref, wt_ref, rs_ref, o_ref):
    acc = lax.dot_general(wt_ref[...], h_ref[...], (((1,), (1,)), ((), ())), preferred_element_type=F32)
    o_ref[...] = (acc * rs_ref[...]).astype(o_ref.dtype)


PROJ_TM = 1024


def _proj(h, wt, layer, out_dtype):
    t_len, d = h.shape
    n = wt.shape[1]
    tm, tn = min(PROJ_TM, t_len), _pick_tn(n)
    return pl.pallas_call(
        _proj_kernel,
        out_shape=jax.ShapeDtypeStruct((t_len, n), out_dtype),
        grid=(n // tn, t_len // tm),
        in_specs=[pl.BlockSpec((tm, d), lambda j, i: (i, 0)), pl.BlockSpec((None, tn, d), lambda j, i: (layer, j, 0))],
        out_specs=pl.BlockSpec((tm, tn), lambda j, i: (i, j)),
        compiler_params=pltpu.CompilerParams(
            dimension_semantics=("parallel", "parallel"), vmem_limit_bytes=VMEM_LIMIT),
        name="proj",
    )(h, wt)


def _proj_t(h, wt, layer, rowscale, out_dtype):
    t_len, d = h.shape
    n = wt.shape[1]
    tm, tn = min(PROJ_TM, t_len), _pick_tn(n)
    return pl.pallas_call(
        _proj_t_kernel,
        out_shape=jax.ShapeDtypeStruct((n, t_len), out_dtype),
        grid=(n // tn, t_len // tm),
        in_specs=[pl.BlockSpec((tm, d), lambda j, i: (i, 0)),
                  pl.BlockSpec((None, tn, d), lambda j, i: (layer, j, 0)),
                  pl.BlockSpec((tn, 1), lambda j, i: (j, 0))],
        out_specs=pl.BlockSpec((tn, tm), lambda j, i: (j, i)),
        compiler_params=pltpu.CompilerParams(
            dimension_semantics=("parallel", "parallel"), vmem_limit_bytes=VMEM_LIMIT),
        name="proj_t",
    )(h, wt, rowscale)


def _out_proj_kernel(x_ref, ya_ref, yb_ref, wa_ref, wb_ref, o_ref, wa16_ref, wb16_ref):
    @pl.when(pl.program_id(1) == 0)
    def _():
        wa16_ref[...] = wa_ref[...].astype(BF16)
        wb16_ref[...] = wb_ref[...].astype(BF16)

    acc = jnp.dot(ya_ref[...], wa16_ref[...], preferred_element_type=F32)
    acc += jnp.dot(yb_ref[...], wb16_ref[...], preferred_element_type=F32)
    o_ref[...] = x_ref[...] + acc


def _out_proj(x, ya, yb, w, layer, tm=1024, tn=512):
    t_len, d = x.shape
    k = ya.shape[1]
    return pl.pallas_call(
        _out_proj_kernel,
        out_shape=jax.ShapeDtypeStruct((t_len, d), F32),
        grid=(d // tn, t_len // tm),
        in_specs=[
            pl.BlockSpec((tm, tn), lambda j, i: (i, j)),
            pl.BlockSpec((tm, k), lambda j, i: (i, 0)),
            pl.BlockSpec((tm, k), lambda j, i: (i, 0)),
            pl.BlockSpec((None, k, tn), lambda j, i: (layer, 0, j)),
            pl.BlockSpec((None, k, tn), lambda j, i: (layer, 1, j)),
        ],
        out_specs=pl.BlockSpec((tm, tn), lambda j, i: (i, j)),
        scratch_shapes=[pltpu.VMEM((k, tn), BF16), pltpu.VMEM((k, tn), BF16)],
        compiler_params=pltpu.CompilerParams(
            dimension_semantics=("parallel", "arbitrary"), vmem_limit_bytes=VMEM_LIMIT),
        name="out_proj",
    )(x, ya, yb, w, w)


NSA_TQ = 256
_NT = (((1,), (1,)), ((), ()))


def _nsa_compress_hidden(u_ref, pe_ref, w1_ref, nc):
    dh = NSA_HEAD_DIM
    acc_a = jnp.zeros((nc, dh), F32)
    acc_b = jnp.zeros((nc, dh), F32)
    for l in range(NSA_CMP_STRIDE):
        x = u_ref[pl.ds(l, nc, stride=NSA_CMP_STRIDE), :]
        xa = (x + pe_ref[l:l + 1, :]).astype(BF16)
        xb = (x + pe_ref[NSA_CMP_STRIDE + l:NSA_CMP_STRIDE + l + 1, :]).astype(BF16)
        acc_a += jnp.dot(xa, w1_ref[l * dh:(l + 1) * dh, :].astype(BF16), preferred_element_type=F32)
        acc_b += jnp.dot(xb, w1_ref[(NSA_CMP_STRIDE + l) * dh:(NSA_CMP_STRIDE + l + 1) * dh, :].astype(BF16),
                         preferred_element_type=F32)
    h = acc_a + pltpu.roll(acc_b, nc - 1, axis=0)
    return (h * jax.nn.sigmoid(h)).astype(BF16)


def _nsa_compress_kernel(uk_ref, uv_ref, pe_ref, w1_ref, w2k_ref, w2vt_ref, kc_ref, vct_ref):
    nc = kc_ref.shape[1]
    hk = _nsa_compress_hidden(uk_ref, pe_ref.at[0], w1_ref.at[0], nc)
    kc_ref[0] = jnp.dot(hk, w2k_ref[...].astype(BF16), preferred_element_type=F32).astype(kc_ref.dtype)
    hv = _nsa_compress_hidden(uv_ref, pe_ref.at[1], w1_ref.at[1], nc)
    vct_ref[0] = lax.dot_general(w2vt_ref[...].astype(BF16), hv, _NT,
                                 preferred_element_type=F32).astype(vct_ref.dtype)


def _nsa_compress(ub, pe, w1, w2):
    t_len = ub.shape[0]
    nc = t_len // NSA_CMP_STRIDE
    g_n, dh = NSA_KV_GROUPS, NSA_HEAD_DIM
    return pl.pallas_call(
        _nsa_compress_kernel,
        out_shape=(jax.ShapeDtypeStruct((g_n, nc, dh), BF16), jax.ShapeDtypeStruct((g_n, dh, nc), BF16)),
        grid=(g_n,),
        in_specs=[
            pl.BlockSpec((t_len, dh), lambda g: (0, OD_B_KCMP // dh + g)),
            pl.BlockSpec((t_len, dh), lambda g: (0, OD_B_VCMP // dh + g)),
            pl.BlockSpec((2, NSA_CMP_BLOCK, dh), lambda g: (0, 0, 0)),
            pl.BlockSpec((2, NSA_CMP_BLOCK * dh, dh), lambda g: (0, 0, 0)),
            pl.BlockSpec((dh, dh), lambda g: (0, 0)),
            pl.BlockSpec((dh, dh), lambda g: (0, 0)),
        ],
        out_specs=(pl.BlockSpec((1, nc, dh), lambda g: (g, 0, 0)), pl.BlockSpec((1, dh, nc), lambda g: (g, 0, 0))),
        compiler_params=pltpu.CompilerParams(dimension_semantics=("parallel",), vmem_limit_bytes=VMEM_LIMIT),
        name="nsa_compress",
    )(ub, ub, pe, w1, w2[0], w2[1].T)


def _nsa_cmp_kernel(qt_ref, kc_ref, vct_ref, ovt_ref, ocmp_ref, sel_ref):
    tq = qt_ref.shape[1]
    nc = kc_ref.shape[1]
    ns = ovt_ref.shape[0]
    dh = NSA_HEAD_DIM
    t = pl.program_id(1) * tq + lax.broadcasted_iota(jnp.int32, (1, tq), 1)
    n_idx = lax.broadcasted_iota(jnp.int32, (nc, 1), 0)
    valid = n_idx * NSA_CMP_STRIDE + (NSA_CMP_BLOCK - 1) <= t
    kc = kc_ref[0]
    vct = vct_ref[0]
    heads = range(NSA_GH)
    s = [jnp.where(valid, jnp.dot(kc, qt_ref[h * dh:(h + 1) * dh, :], preferred_element_type=F32), NEG)
         for h in heads]
    m = [jnp.maximum(jnp.max(s[h], axis=0, keepdims=True), M_INIT) for h in heads]
    p = [jnp.exp2(s[h] - m[h]) for h in heads]
    l = [jnp.sum(p[h], axis=0, keepdims=True) for h in heads]
    p = [p[h] * (1.0 / jnp.where(l[h] > 0.0, l[h], 1.0)) for h in heads]
    for h in heads:
        ocmp_ref[h * dh:(h + 1) * dh, :] = jnp.dot(vct, p[h].astype(BF16), preferred_element_type=F32)
    psum = sum(p[1:], p[0])
    hi = psum.astype(BF16)
    lo = (psum - hi.astype(F32)).astype(BF16)
    ovt = ovt_ref[...]
    imp = jnp.dot(ovt, hi, preferred_element_type=F32) + jnp.dot(ovt, lo, preferred_element_type=F32)
    j = lax.broadcasted_iota(jnp.int32, (ns, 1), 0)
    cur = lax.shift_right_logical(t, int(math.log2(NSA_SLC_BLOCK)))
    forced = (j == 0) | (j == cur) | (j == cur - 1)
    ok = j * NSA_SLC_BLOCK <= t
    imp = jnp.where(ok, imp + jnp.where(forced, NSA_FORCE_BONUS, 0.0), NEG)
    jf = j.astype(F32)

    def pick(_, left):
        mx = jnp.max(left, axis=0, keepdims=True)
        first = jnp.min(jnp.where(left == mx, jf, float(ns)), axis=0, keepdims=True)
        return jnp.where(jf == first, -jnp.inf, left)

    left = lax.fori_loop(0, min(NSA_TOPK, ns), pick, imp)
    sel_ref[0] = jnp.where((left == -jnp.inf) & ok, 1.0, 0.0).astype(sel_ref.dtype)


def _nsa_cmp(at, kc, vct, overlap_t):
    t_len = at.shape[1]
    tq = min(NSA_TQ, t_len)
    ns, nc = overlap_t.shape
    g_n, dh = NSA_KV_GROUPS, NSA_HEAD_DIM
    gw = NSA_GH * dh
    return pl.pallas_call(
        _nsa_cmp_kernel,
        out_shape=(jax.ShapeDtypeStruct((MIX_W, t_len), F32), jax.ShapeDtypeStruct((g_n, ns, t_len), BF16)),
        grid=(g_n, t_len // tq),
        in_specs=[
            pl.BlockSpec((gw, tq), lambda g, i: (g, i)),
            pl.BlockSpec((1, nc, dh), lambda g, i: (g, 0, 0)),
            pl.BlockSpec((1, dh, nc), lambda g, i: (g, 0, 0)),
            pl.BlockSpec((ns, nc), lambda g, i: (0, 0)),
        ],
        out_specs=(pl.BlockSpec((gw, tq), lambda g, i: (g, i)), pl.BlockSpec((1, ns, tq), lambda g, i: (g, 0, i))),
        compiler_params=pltpu.CompilerParams(
            dimension_semantics=("parallel", "parallel"), vmem_limit_bytes=VMEM_LIMIT),
        name="nsa_cmp",
    )(at, kc, vct, overlap_t)


def _flash_init(m_ref, l_ref, acc_ref):
    m_ref[...] = jnp.full(m_ref.shape, M_INIT, F32)
    l_ref[...] = jnp.zeros(l_ref.shape, F32)
    acc_ref[...] = jnp.zeros(acc_ref.shape, F32)


def _flash_tile(q_of, k_tile, vt_tile, mask, m_ref, l_ref, acc_ref):
    _flash_consume([jnp.dot(k_tile, q_of(h), preferred_element_type=F32) for h in range(NSA_GH)], vt_tile, mask,
                   m_ref, l_ref, acc_ref)


def _flash_consume(s, vt_tile, mask, m_ref, l_ref, acc_ref):
    heads = range(NSA_GH)
    if mask is not None:
        s = [jnp.where(mask, s[h], NEG) for h in heads]
    m_old = [m_ref[h] for h in heads]
    m_new = [jnp.maximum(m_old[h], jnp.max(s[h], axis=0, keepdims=True)) for h in heads]
    alpha = [jnp.exp2(m_old[h] - m_new[h]) for h in heads]
    p = [jnp.exp2(s[h] - m_new[h]) for h in heads]
    for h in heads:
        l_ref[h] = alpha[h] * l_ref[h] + jnp.sum(p[h], axis=0, keepdims=True)
        m_ref[h] = m_new[h]
    pv = [jnp.dot(vt_tile, p[h].astype(BF16), preferred_element_type=F32) for h in heads]
    for h in heads:
        acc_ref[h] = alpha[h] * acc_ref[h] + pv[h]


def _nsa_slc_kernel(qt_ref, k_ref, vt_ref, sel_ref, eb_ref, o_ref, qa_ref, s0_ref, s1_ref, m_ref, l_ref, acc_ref):
    tq = qt_ref.shape[1]
    dh = NSA_HEAD_DIM
    i = pl.program_id(1)
    t = i * tq + lax.broadcasted_iota(jnp.int32, (1, tq), 1)
    c_idx = lax.broadcasted_iota(jnp.int32, (tq, 1), 0)
    bias = ((sel_ref[0].astype(F32) - 1.0) * -NEG).astype(BF16)
    for h in range(NSA_GH):
        qa_ref[h, :dh, :] = qt_ref[h * dh:(h + 1) * dh, :]
        qa_ref[h, dh:, :] = bias
    _flash_init(m_ref, l_ref, acc_ref)

    def scores(jt, s_ref):
        k0 = pl.multiple_of(jt * tq, tq)
        k_aug = jnp.concatenate([k_ref[pl.ds(k0, tq), :], eb_ref[pl.ds(k0, tq), :]], axis=1)
        for h in range(NSA_GH):
            s_ref[h] = jnp.dot(k_aug, qa_ref[h], preferred_element_type=F32)

    def consume(jt, s_ref, causal):
        k0 = pl.multiple_of(jt * tq, tq)
        mask = (c_idx + k0 <= t) if causal else None
        _flash_consume([s_ref[h] for h in range(NSA_GH)], vt_ref[:, pl.ds(k0, tq)], mask, m_ref, l_ref, acc_ref)

    def pair(p, carry):
        scores(2 * p + 1, s1_ref)
        consume(2 * p, s0_ref, False)
        scores(2 * p + 2, s0_ref)
        consume(2 * p + 1, s1_ref, False)
        return carry

    scores(0, s0_ref)
    lax.fori_loop(0, i // 2, pair, 0)

    @pl.when(i % 2 == 0)
    def _():
        consume(i, s0_ref, True)

    @pl.when(i % 2 == 1)
    def _():
        scores(i, s1_ref)
        consume(i - 1, s0_ref, False)
        consume(i, s1_ref, True)

    for h in range(NSA_GH):
        o_ref[h * dh:(h + 1) * dh, :] = acc_ref[h] * (1.0 / l_ref[h])


def _nsa_win_kernel(qt_ref, k_ref, vt_ref, ocmp_ref, oslc_ref, gates_ref, gp_ref, o_ref, m_ref, l_ref, acc_ref):
    tq = qt_ref.shape[1]
    dh = NSA_HEAD_DIM
    i = pl.program_id(1)
    t = i * tq + lax.broadcasted_iota(jnp.int32, (1, tq), 1)
    c_idx = lax.broadcasted_iota(jnp.int32, (tq, 1), 0)
    _flash_init(m_ref, l_ref, acc_ref)

    def scores(jt):
        k_tile = k_ref[pl.ds(pl.multiple_of(jt * tq, tq), tq), :]
        return [jnp.dot(k_tile, qt_ref[h * dh:(h + 1) * dh, :], preferred_element_type=F32) for h in range(NSA_GH)]

    def consume(jt, s, mask_of):
        k0 = pl.multiple_of(jt * tq, tq)
        mask = None if mask_of is None else mask_of(c_idx + k0)
        _flash_consume(s, vt_ref[:, pl.ds(k0, tq)], mask, m_ref, l_ref, acc_ref)

    edge = lambda key: key > t - NSA_WINDOW
    causal = lambda key: key <= t

    @pl.when(i >= 2)
    def _():
        s_old, s_mid, s_new = scores(i - 2), scores(i - 1), scores(i)
        consume(i - 2, s_old, edge)
        consume(i - 1, s_mid, None)
        consume(i, s_new, causal)

    @pl.when(i == 1)
    def _():
        s_mid, s_new = scores(0), scores(1)
        consume(0, s_mid, None)
        consume(1, s_new, causal)

    @pl.when(i == 0)
    def _():
        consume(0, scores(0), causal)
    gates_t = jnp.transpose(jax.nn.sigmoid(gates_ref[...]))
    for h in range(NSA_GH):
        rows = slice(h * dh, (h + 1) * dh)
        o_t = (gates_t[h:h + 1, :] * ocmp_ref[rows, :]
               + gates_t[NSA_GH + h:NSA_GH + h + 1, :] * oslc_ref[rows, :]
               + gates_t[2 * NSA_GH + h:2 * NSA_GH + h + 1, :] * (acc_ref[h] * (1.0 / l_ref[h])))
        gp = gp_ref[:, rows]
        o_ref[:, rows] = (jnp.transpose(o_t) * (gp * jax.nn.sigmoid(gp))).astype(o_ref.dtype)


def _nsa_scratch(tq):
    return [pltpu.VMEM((NSA_GH, 1, tq), F32), pltpu.VMEM((NSA_GH, 1, tq), F32),
            pltpu.VMEM((NSA_GH, NSA_HEAD_DIM, tq), F32)]


def _nsa_slc(at, an, sel):
    t_len = at.shape[1]
    tq = NSA_TQ
    ns = sel.shape[1]
    g_n, dh = NSA_KV_GROUPS, NSA_HEAD_DIM
    gw = NSA_GH * dh
    block_of_key = jnp.asarray(np.arange(t_len)[:, None] // NSA_SLC_BLOCK == np.arange(ns)[None], dtype=BF16)
    return pl.pallas_call(
        _nsa_slc_kernel,
        out_shape=jax.ShapeDtypeStruct((MIX_W, t_len), F32),
        grid=(g_n, t_len // tq),
        in_specs=[
            pl.BlockSpec((gw, tq), lambda g, i: (g, i)),
            pl.BlockSpec((t_len, dh), lambda g, i: (0, OD_AN_KSLC + g)),
            pl.BlockSpec((dh, t_len), lambda g, i: (OD_AT_VSLC + g, 0)),
            pl.BlockSpec((1, ns, tq), lambda g, i: (g, 0, i)),
            pl.BlockSpec((t_len, ns), lambda g, i: (0, 0)),
        ],
        out_specs=pl.BlockSpec((gw, tq), lambda g, i: (g, i)),
        scratch_shapes=[pltpu.VMEM((NSA_GH, dh + ns, tq), BF16), pltpu.VMEM((NSA_GH, tq, tq), F32),
                        pltpu.VMEM((NSA_GH, tq, tq), F32)] + _nsa_scratch(tq),
        compiler_params=pltpu.CompilerParams(
            dimension_semantics=("parallel", "parallel"), vmem_limit_bytes=VMEM_LIMIT),
        name="nsa_slc",
    )(at, an, at, sel, block_of_key)


def _nsa_win(at, an, ub, ocmp, oslc):
    t_len = at.shape[1]
    tq = NSA_TQ
    assert NSA_WINDOW == 2 * tq
    g_n, dh = NSA_KV_GROUPS, NSA_HEAD_DIM
    gw = NSA_GH * dh
    return pl.pallas_call(
        _nsa_win_kernel,
        out_shape=jax.ShapeDtypeStruct((t_len, MIX_W), BF16),
        grid=(g_n, t_len // tq),
        in_specs=[
            pl.BlockSpec((gw, tq), lambda g, i: (g, i)),
            pl.BlockSpec((t_len, dh), lambda g, i: (0, OD_AN_KWIN + g)),
            pl.BlockSpec((dh, t_len), lambda g, i: (OD_AT_VWIN + g, 0)),
            pl.BlockSpec((gw, tq), lambda g, i: (g, i)),
            pl.BlockSpec((gw, tq), lambda g, i: (g, i)),
            pl.BlockSpec((tq, LANES), lambda g, i: (i, OD_B_GATES // LANES + g)),
            pl.BlockSpec((tq, gw), lambda g, i: (i, OD_B_GP // gw + g)),
        ],
        out_specs=pl.BlockSpec((tq, gw), lambda g, i: (i, g)),
        scratch_shapes=_nsa_scratch(tq),
        compiler_params=pltpu.CompilerParams(
            dimension_semantics=("parallel", "parallel"), vmem_limit_bytes=VMEM_LIMIT),
        name="nsa_win",
    )(at, an, at, ocmp, oslc, ub, ub)


def _nsa_overlap_t(t_len):
    nc = t_len // NSA_CMP_STRIDE
    n_cmp = (t_len - NSA_CMP_BLOCK) // NSA_CMP_STRIDE + 1
    cs = np.arange(n_cmp) * NSA_CMP_STRIDE
    ss = np.arange(t_len // NSA_SLC_BLOCK) * NSA_SLC_BLOCK
    ov = np.clip(np.minimum(cs[:, None] + NSA_CMP_BLOCK, ss[None] + NSA_SLC_BLOCK)
                 - np.maximum(cs[:, None], ss[None]), 0, None) / NSA_CMP_BLOCK
    ov = np.concatenate([ov, np.zeros((nc - n_cmp, ov.shape[1]))], axis=0)
    return jnp.asarray(ov.T, dtype=BF16)


def _nsa_mixer(at, an, ub, cmp_pe, cmp_w1, cmp_w2):
    t_len = an.shape[0]
    kc, vct = _nsa_compress(ub, cmp_pe, cmp_w1, cmp_w2)
    ocmp, sel = _nsa_cmp(at, kc, vct, _nsa_overlap_t(t_len))
    oslc = _nsa_slc(at, an, sel)
    return _nsa_win(at, an, ub, ocmp, oslc)


CONV_TAIL = 8


def _cumsum_rows(x):
    n = x.shape[0]
    row = lax.broadcasted_iota(jnp.int32, x.shape, 0)
    s = 1
    while s < n:
        x = x + jnp.where(row >= s, pltpu.roll(x, s, axis=0), 0.0)
        s *= 2
    return x


def _split_bf16(v, parts):
    out = []
    for _ in range(parts - 1):
        p = v.astype(BF16)
        out.append(p)
        v = v - p.astype(F32)
    out.append(v.astype(BF16))
    return out


def _expand_lanes(v, onehot):
    return sum(jnp.dot(p, onehot, preferred_element_type=F32) for p in _split_bf16(v, 3))


def _conv_silu(x, tail, w, b):
    row = lax.broadcasted_iota(jnp.int32, (CONV_TAIL, 1), 0)
    y = b + w[SSD_CONV - 1:SSD_CONV, :] * x
    for s in range(1, SSD_CONV):
        xs = pltpu.roll(x, s, axis=0)
        head = jnp.where(row < s, pltpu.roll(tail, s, axis=0), xs[:CONV_TAIL, :])
        y = y + w[SSD_CONV - 1 - s:SSD_CONV - s, :] * jnp.concatenate([head, xs[CONV_TAIL:, :]], axis=0)
    return y * jax.nn.sigmoid(y)


def _softplus(x):
    return jnp.maximum(x, 0.0) + jnp.log1p(jnp.exp(-jnp.abs(x)))


_TN = (((0,), (0,)), ((), ()))


def _ssd_kernel(z_ref, x_ref, b_ref, c_ref, dt_ref, cwx_ref, cwb_ref, cwc_ref, cbx_ref, cbb_ref, cbc_ref,
                dtb_ref, alog_ref, dsk_ref, nw_ref, e_ref, o_ref, tail_ref, state_ref, y_ref):
    L = SSD_CHUNK
    n, gw = SSD_STATE, (SSD_HEADS // SSD_GROUPS) * SSD_HEAD_DIM

    @pl.when(pl.program_id(0) == 0)
    def _():
        tail_ref[...] = jnp.zeros(tail_ref.shape, F32)
        state_ref[...] = jnp.zeros(state_ref.shape, F32)

    xr, br, cr = x_ref[...], b_ref[...], c_ref[...]
    x = _conv_silu(xr, tail_ref[:, :MIX_W], cwx_ref[...], cbx_ref[...])
    bm = _conv_silu(br, tail_ref[:, MIX_W:MIX_W + SSD_GN], cwb_ref[...], cbb_ref[...])
    cm = _conv_silu(cr, tail_ref[:, MIX_W + SSD_GN:], cwc_ref[...], cbc_ref[...])
    tail_ref[:, :MIX_W] = xr[L - CONV_TAIL:, :]
    tail_ref[:, MIX_W:MIX_W + SSD_GN] = br[L - CONV_TAIL:, :]
    tail_ref[:, MIX_W + SSD_GN:] = cr[L - CONV_TAIL:, :]

    dt = _softplus(dt_ref[...] + dtb_ref[...])
    cum = _cumsum_rows(dt * -jnp.exp(alog_ref[...]))
    cum_t = jnp.transpose(cum)
    onehot = e_ref[...]
    cum_e = _expand_lanes(cum, onehot)
    xdt = x * _expand_lanes(dt, onehot)
    last_e = cum_e[L - 1:L, :]
    xdt16 = xdt.astype(BF16)
    xw16 = (xdt * jnp.exp(last_e - cum_e)).astype(BF16)
    ecum_e = jnp.exp(cum_e)
    causal = lax.broadcasted_iota(jnp.int32, (L, L), 0) >= lax.broadcasted_iota(jnp.int32, (L, L), 1)
    lane = lax.broadcasted_iota(jnp.int32, (1, gw), 1)
    for g in range(SSD_GROUPS):
        cols = slice(g * gw, (g + 1) * gw)
        bg = bm[:, g * n:(g + 1) * n]
        cg16 = cm[:, g * n:(g + 1) * n].astype(BF16)
        sc = lax.dot_general(cg16, bg.astype(BF16), _NT, preferred_element_type=F32)
        state = state_ref[g]
        yg = jnp.dot(cg16, state.astype(BF16), preferred_element_type=F32) * ecum_e[:, cols]
        xg = xdt16[:, cols]
        for r in range(SSD_HEADS // SSD_GROUPS):
            h = g * (SSD_HEADS // SSD_GROUPS) + r
            decay = jnp.where(causal, jnp.exp(cum[:, h:h + 1] - cum_t[h:h + 1, :]), 0.0)
            in_head = (lane >= r * SSD_HEAD_DIM) & (lane < (r + 1) * SSD_HEAD_DIM)
            yg += jnp.dot((sc * decay).astype(BF16), jnp.where(in_head, xg, jnp.zeros_like(xg)),
                          preferred_element_type=F32)
        y_ref[:, cols] = yg
        state_ref[g] = state * jnp.exp(last_e[:, cols]) + lax.dot_general(
            bg.astype(BF16), xw16[:, cols], _TN, preferred_element_type=F32)

    y = y_ref[...] + x * dsk_ref[...]
    zz = z_ref[...]
    y = y * (zz * jax.nn.sigmoid(zz))
    ms = jnp.mean(y * y, axis=-1, keepdims=True)
    o_ref[...] = (y * lax.rsqrt(ms + EPS) * nw_ref[...]).astype(o_ref.dtype)


def _ssd_mixer(u, conv_w, conv_b, dt_bias, a_log, d_skip, norm_w):
    t_len = u.shape[0]
    L = SSD_CHUNK
    pad = lambda v: jnp.pad(v, (0, LANES - SSD_HEADS)).reshape(1, LANES)
    onehot = np.zeros((LANES, MIX_W), np.float32)
    onehot[np.arange(MIX_W) // SSD_HEAD_DIM, np.arange(MIX_W)] = 1.0
    row = lambda width, start: pl.BlockSpec((L, width), lambda c: (c, start // width))
    fixed = lambda rows, width, start: pl.BlockSpec((rows, width), lambda c: (0, start // width))
    return pl.pallas_call(
        _ssd_kernel,
        out_shape=jax.ShapeDtypeStruct((t_len, MIX_W), BF16),
        grid=(t_len // L,),
        in_specs=[
            row(MIX_W, EV_Z), row(MIX_W, EV_X), row(SSD_GN, EV_B), row(SSD_GN, EV_C), row(LANES, EV_DT),
            fixed(SSD_CONV, MIX_W, 0), fixed(SSD_CONV, SSD_GN, MIX_W), fixed(SSD_CONV, SSD_GN, MIX_W + SSD_GN),
            fixed(1, MIX_W, 0), fixed(1, SSD_GN, MIX_W), fixed(1, SSD_GN, MIX_W + SSD_GN),
            fixed(1, LANES, 0), fixed(1, LANES, 0), fixed(1, MIX_W, 0), fixed(1, MIX_W, 0),
            fixed(LANES, MIX_W, 0),
        ],
        out_specs=pl.BlockSpec((L, MIX_W), lambda c: (c, 0)),
        scratch_shapes=[pltpu.VMEM((CONV_TAIL, SSD_XBC), F32),
                        pltpu.VMEM((SSD_GROUPS, SSD_STATE, MIX_W // SSD_GROUPS), F32),
                        pltpu.VMEM((L, MIX_W), F32)],
        compiler_params=pltpu.CompilerParams(dimension_semantics=("arbitrary",), vmem_limit_bytes=VMEM_LIMIT),
        name="ssd_mixer",
    )(u, u, u, u, u, conv_w, conv_w, conv_w, conv_b.reshape(1, -1), conv_b.reshape(1, -1), conv_b.reshape(1, -1),
      pad(dt_bias), pad(a_log), jnp.repeat(d_skip, SSD_HEAD_DIM).reshape(1, MIX_W), norm_w.reshape(1, MIX_W),
      jnp.asarray(onehot, dtype=BF16))


def _gla_kernel(q_ref, k_ref, v_ref, r_ref, glr_ref, wg_ref, bg_ref, nw_ref, o_ref, state_ref):
    L, dk, dv = GLA_CHUNK, GLA_DK, GLA_DV

    @pl.when(pl.program_id(0) == 0)
    def _():
        state_ref[...] = jnp.zeros(state_ref.shape, F32)

    g_hi, g_lo = _split_bf16(glr_ref[...], 2)
    w_hi, w_lo = _split_bf16(wg_ref[...], 2)
    logits = (jnp.dot(g_hi, w_hi, preferred_element_type=F32) + jnp.dot(g_hi, w_lo, preferred_element_type=F32)
              + jnp.dot(g_lo, w_hi, preferred_element_type=F32) + bg_ref[...])
    cum = _cumsum_rows(-_softplus(-logits) * (1.0 / GLA_TAU))
    tot = cum[L - 1:L, :]
    half = 0.5 * tot
    q = q_ref[...] * dk ** -0.5
    k = k_ref[...]
    qe = (q * jnp.exp(cum - half)).astype(BF16)
    ke = (k * jnp.exp(half - cum)).astype(BF16)
    qd = (q * jnp.exp(cum)).astype(BF16)
    kd = k * jnp.exp(tot - cum)
    etot = jnp.exp(tot)
    causal = lax.broadcasted_iota(jnp.int32, (L, L), 0) >= lax.broadcasted_iota(jnp.int32, (L, L), 1)
    for h in range(GLA_HEADS):
        ks = slice(h * dk, (h + 1) * dk)
        vs = slice(h * dv, (h + 1) * dv)
        att = lax.dot_general(qe[:, ks], ke[:, ks], _NT, preferred_element_type=F32)
        att = jnp.where(causal, att, 0.0).astype(BF16)
        vh = v_ref[:, vs].astype(BF16)
        state = state_ref[h]
        y = jnp.dot(att, vh, preferred_element_type=F32) + jnp.dot(qd[:, ks], state.astype(BF16),
                                                                   preferred_element_type=F32)
        kd_t = jnp.transpose(kd[:, ks]).astype(BF16)
        decay_col = jnp.transpose(jnp.broadcast_to(etot[:, ks], (8, dk)))[:, 0:1]
        state_ref[h] = state * decay_col + jnp.dot(kd_t, vh, preferred_element_type=F32)
        ms = jnp.mean(y * y, axis=-1, keepdims=True)
        r = r_ref[:, vs]
        o_ref[:, vs] = (y * lax.rsqrt(ms + EPS) * nw_ref[...] * (r * jax.nn.sigmoid(r))).astype(o_ref.dtype)


def _gla_mixer(u, w_gate, b_gate, norm_w):
    t_len = u.shape[0]
    L = GLA_CHUNK
    row = lambda width, start: pl.BlockSpec((L, width), lambda c: (c, start // width))
    full = lambda a: pl.BlockSpec(a.shape, lambda c: (0, 0))
    wg = jnp.pad(w_gate, ((0, LANES - GLA_RANK), (0, 0)))
    bg = b_gate.reshape(1, GLA_DKT)
    nw = norm_w.reshape(1, GLA_DV)
    return pl.pallas_call(
        _gla_kernel,
        out_shape=jax.ShapeDtypeStruct((t_len, MIX_W), BF16),
        grid=(t_len // L,),
        in_specs=[row(GLA_DKT, EV_Q), row(GLA_DKT, EV_K), row(MIX_W, EV_V), row(MIX_W, EV_R), row(LANES, EV_GLR),
                  full(wg), full(bg), full(nw)],
        out_specs=pl.BlockSpec((L, MIX_W), lambda c: (c, 0)),
        scratch_shapes=[pltpu.VMEM((GLA_HEADS, GLA_DK, GLA_DV), F32)],
        compiler_params=pltpu.CompilerParams(dimension_semantics=("arbitrary",), vmem_limit_bytes=VMEM_LIMIT),
        name="gla_mixer",
    )(u, u, u, u, u, wg, bg, nw)


def _unit_lower_inverses(mats):
    n = mats[0].shape[0]
    eye = (lax.broadcasted_iota(jnp.int32, (n, n), 0) == lax.broadcasted_iota(jnp.int32, (n, n), 1)).astype(F32)
    ps = [eye - a for a in mats]
    xs = list(mats)
    for _ in range(int(math.log2(n)) - 1):
        x16 = [x.astype(BF16) for x in xs]
        xs = [jnp.dot(x, x, preferred_element_type=F32) for x in x16]
        ps = [p + jnp.dot(p.astype(BF16), x.astype(BF16), preferred_element_type=F32) for p, x in zip(ps, xs)]
    return ps


def _gdn_kernel(q_ref, k_ref, v_ref, z_ref, ab_ref, cwq_ref, cwk_ref, cwv_ref, cbq_ref, cbk_ref, cbv_ref,
                alog_ref, dtb_ref, nw_ref, o_ref, tail_ref, state_ref):
    L, dk, dv = GDN_CHUNK, GDN_DK, GDN_DV
    rep = GDN_V_HEADS // GDN_QK_HEADS

    @pl.when(pl.program_id(0) == 0)
    def _():
        tail_ref[...] = jnp.zeros(tail_ref.shape, F32)
        state_ref[...] = jnp.zeros(state_ref.shape, F32)

    qr, kr, vr = q_ref[...], k_ref[...], v_ref[...]
    q = _conv_silu(qr, tail_ref[:, :GDN_QKW], cwq_ref[...], cbq_ref[...])
    k = _conv_silu(kr, tail_ref[:, GDN_QKW:2 * GDN_QKW], cwk_ref[...], cbk_ref[...])
    v = _conv_silu(vr, tail_ref[:, 2 * GDN_QKW:], cwv_ref[...], cbv_ref[...])
    tail_ref[:, :GDN_QKW] = qr[L - CONV_TAIL:, :]
    tail_ref[:, GDN_QKW:2 * GDN_QKW] = kr[L - CONV_TAIL:, :]
    tail_ref[:, 2 * GDN_QKW:] = vr[L - CONV_TAIL:, :]

    ab = ab_ref[...]
    beta = jax.nn.sigmoid(ab)
    gc = _cumsum_rows(-jnp.exp(alog_ref[...]) * _softplus(ab + dtb_ref[...]))
    gc_t = jnp.transpose(gc)
    egc = jnp.exp(gc)
    last = gc[L - 1:L, :]
    e_end = jnp.exp(last - gc)
    e_last = jnp.exp(last)
    rows = lax.broadcasted_iota(jnp.int32, (L, L), 0)
    cols = lax.broadcasted_iota(jnp.int32, (L, L), 1)
    heads = range(GDN_V_HEADS)
    qn, kn, kk, qk = [], [], [], []
    for j in range(GDN_QK_HEADS):
        qj = q[:, j * dk:(j + 1) * dk]
        kj = k[:, j * dk:(j + 1) * dk]
        qn.append(qj * (lax.rsqrt(jnp.sum(qj * qj, axis=-1, keepdims=True) + EPS) * dk ** -0.5))
        kn.append(kj * lax.rsqrt(jnp.sum(kj * kj, axis=-1, keepdims=True) + EPS))
        kn16 = kn[j].astype(BF16)
        kk.append(lax.dot_general(kn16, kn16, _NT, preferred_element_type=F32))
        qk.append(lax.dot_general(qn[j].astype(BF16), kn16, _NT, preferred_element_type=F32))
    decay = [jnp.where(rows >= cols, jnp.exp(gc[:, h:h + 1] - gc_t[h:h + 1, :]), 0.0) for h in heads]
    b_col = [beta[:, GDN_V_HEADS + h:GDN_V_HEADS + h + 1] for h in heads]
    t_mat = _unit_lower_inverses([jnp.where(rows > cols, b_col[h] * kk[h // rep] * decay[h], 0.0) for h in heads])
    rhs = [jnp.concatenate([v[:, h * dv:(h + 1) * dv] * b_col[h], kn[h // rep] * (b_col[h] * egc[:, h:h + 1])],
                           axis=1).astype(BF16) for h in heads]
    uw = [jnp.dot(t_mat[h].astype(BF16), rhs[h], preferred_element_type=F32) for h in heads]
    state = [state_ref[h] for h in heads]
    s16 = [s.astype(BF16) for s in state]
    vn16 = [(uw[h][:, :dv] - jnp.dot(uw[h][:, dv:].astype(BF16), s16[h], preferred_element_type=F32)).astype(BF16)
            for h in heads]
    out = [jnp.dot((qn[h // rep] * egc[:, h:h + 1]).astype(BF16), s16[h], preferred_element_type=F32)
           + jnp.dot((qk[h // rep] * decay[h]).astype(BF16), vn16[h], preferred_element_type=F32) for h in heads]
    for h in heads:
        kd16 = (kn[h // rep] * e_end[:, h:h + 1]).astype(BF16)
        state_ref[h] = state[h] * e_last[:, h:h + 1] + lax.dot_general(kd16, vn16[h], _TN,
                                                                       preferred_element_type=F32)
    for h in heads:
        o = out[h]
        ms = jnp.mean(o * o, axis=-1, keepdims=True)
        zz = z_ref[:, h * dv:(h + 1) * dv]
        o_ref[:, h * dv:(h + 1) * dv] = (o * lax.rsqrt(ms + EPS) * nw_ref[...]
                                         * (zz * jax.nn.sigmoid(zz))).astype(o_ref.dtype)


def _gdn_mixer(ub, conv_w, conv_b, a_log, dt_bias, norm_w):
    t_len = ub.shape[0]
    L = GDN_CHUNK
    pad = lambda v: jnp.pad(v, (0, LANES - GDN_V_HEADS)).reshape(1, LANES)
    row = lambda width, start: pl.BlockSpec((L, width), lambda c: (c, start // width))
    fixed = lambda rows, width, start: pl.BlockSpec((rows, width), lambda c: (0, start // width))
    cb = conv_b.reshape(1, -1)
    return pl.pallas_call(
        _gdn_kernel,
        out_shape=jax.ShapeDtypeStruct((t_len, MIX_W), BF16),
        grid=(t_len // L,),
        in_specs=[
            row(GDN_QKW, OD_B_CONV), row(GDN_QKW, OD_B_CONV + GDN_QKW), row(MIX_W, OD_B_CONV + 2 * GDN_QKW),
            row(MIX_W, OD_B_Z), row(LANES, OD_B_AB),
            fixed(GDN_CONV, GDN_QKW, 0), fixed(GDN_CONV, GDN_QKW, GDN_QKW), fixed(GDN_CONV, MIX_W, 2 * GDN_QKW),
            fixed(1, GDN_QKW, 0), fixed(1, GDN_QKW, GDN_QKW), fixed(1, MIX_W, 2 * GDN_QKW),
            fixed(1, LANES, 0), fixed(1, LANES, 0), fixed(1, GDN_DV, 0),
        ],
        out_specs=pl.BlockSpec((L, MIX_W), lambda c: (c, 0)),
        scratch_shapes=[pltpu.VMEM((CONV_TAIL, GDN_CONV_CH), F32),
                        pltpu.VMEM((GDN_V_HEADS, GDN_DK, GDN_DV), F32)],
        compiler_params=pltpu.CompilerParams(dimension_semantics=("arbitrary",), vmem_limit_bytes=VMEM_LIMIT),
        name="gdn_mixer",
    )(ub, ub, ub, ub, ub, conv_w, conv_w, conv_w, cb, cb, cb, pad(a_log), pad(dt_bias),
      norm_w.reshape(1, GDN_DV))


def _even_layer(xs, i, ln_w, w_ev, w_out, conv_w, conv_b, dt_bias, a_log, d_skip, ssd_norm_w, w_gate, b_gate,
                gla_norm_w):
    u = _proj(_rmsnorm(xs, ln_w, BF16), w_ev, i, F32)
    ya = _ssd_mixer(u, conv_w, conv_b, dt_bias, a_log, d_skip, ssd_norm_w)
    yb = _gla_mixer(u, w_gate, b_gate, gla_norm_w)
    return _out_proj(xs, ya, yb, w_out, i)


def _odd_layer(xs, i, ln_w, w_at, w_an, w_b, w_out, cmp_pe, cmp_w1, cmp_w2, conv_w, conv_b, a_log, dt_bias,
               gdn_norm_w):
    q_scale = np.ones((OD_AT_W, 1), np.float32)
    q_scale[:MIX_W] = NSA_HEAD_DIM ** -0.5 * math.log2(math.e)
    h = _rmsnorm(xs, ln_w, BF16)
    at = _proj_t(h, w_at, i, jnp.asarray(q_scale), BF16)
    an = _proj(h, w_an, i, BF16)
    ub = _proj(h, w_b, i, F32)
    yc = _nsa_mixer(at, an, ub, cmp_pe, cmp_w1, cmp_w2)
    yd = _gdn_mixer(ub, conv_w, conv_b, a_log, dt_bias, gdn_norm_w)
    return _out_proj(xs, yc, yd, w_out, i)


def kernel(x, ln_w, final_ln_w, ev_w_in, ev_w_out, ssd_conv_w, ssd_conv_b, ssd_dt_bias, ssd_a_log, ssd_d,
           ssd_norm_w, gla_w_gate, gla_b_gate, gla_norm_w, od_w_in, od_w_out, nsa_cmp_pe, nsa_cmp_w1,
           nsa_cmp_w2, gdn_conv_w, gdn_conv_b, gdn_a_log, gdn_dt_bias, gdn_norm_w):
    bsz, t_len, d = x.shape
    assert bsz == 1 and d == D_MODEL
    xs = x.reshape(t_len, d)
    w_ev = _gather_cols(ev_w_in, _EV)
    w_at = _gather_cols(od_w_in, _OD_AT)
    w_an = _gather_cols(od_w_in, _OD_AN)
    w_b = _gather_cols(od_w_in, _OD_B)
    for layer in range(DEPTH):
        i = layer // 2
        if layer % 2 == 0:
            xs = _even_layer(xs, i, ln_w[layer], w_ev, ev_w_out, ssd_conv_w[i], ssd_conv_b[i], ssd_dt_bias[i],
                             ssd_a_log[i], ssd_d[i], ssd_norm_w[i], gla_w_gate[i], gla_b_gate[i], gla_norm_w[i])
        else:
            xs = _odd_layer(xs, i, ln_w[layer], w_at, w_an, w_b, od_w_out, nsa_cmp_pe[i], nsa_cmp_w1[i],
                            nsa_cmp_w2[i], gdn_conv_w[i], gdn_conv_b[i], gdn_a_log[i], gdn_dt_bias[i], gdn_norm_w[i])
    return _rmsnorm(xs, final_ln_w, F32).reshape(bsz, t_len, d)
```
